```python
import math
import jax, jax.numpy as jnp
from jax import lax
import numpy as np

D_MODEL = 1024
BATCH = 8
SEQ = 2048
DEPTH = 1

MEM_LEN = 256
RET_HEADS = 4
RET_DK = 256
RET_DV = 512
RET_CHUNK = 128
ROPE_BASE = 10000.0
DIFF_HEADS = 8
DIFF_D = 64
Q_BLOCK = 128
MEM_HEADS = 4
MEM_D = 256
D_FF = 2816
FFN_RES = 0.5
EPS = 1e-6
NEG = -1e30
N_BRANCH = 3

RET_QK_W = RET_HEADS * RET_DK
RET_V_W = RET_HEADS * RET_DV
DIFF_QK_W = DIFF_HEADS * 2 * DIFF_D
DIFF_V_W = DIFF_HEADS * 2 * DIFF_D
MEM_Q_W = MEM_HEADS * MEM_D
IN_SPLITS = (RET_QK_W, RET_QK_W, RET_V_W, RET_V_W, DIFF_QK_W, DIFF_QK_W, DIFF_V_W, MEM_Q_W, N_BRANCH * D_MODEL)
IN_WIDTH = sum(IN_SPLITS)

kernel_name = "hybrid_retention_diffattn_memory_macaron"


def rms_norm(x, g=None):
    xf = x.astype(jnp.float32)
    y = xf * lax.rsqrt(jnp.mean(xf * xf, axis=-1, keepdims=True) + EPS)
    if g is not None:
        y = y * g.astype(jnp.float32)
    return y.astype(x.dtype)


def swiglu_ffn(h, w_in, w_out):
    gate, up = jnp.split(h @ w_in, 2, axis=-1)
    return (jax.nn.silu(gate) * up) @ w_out


def rotary(x, positions):
    half = x.shape[-1] // 2
    inv = ROPE_BASE ** (-jnp.arange(half, dtype=jnp.float32) / half)
    ang = positions.astype(jnp.float32)[..., None] * inv
    cos = jnp.cos(ang)[:, :, None, :]
    sin = jnp.sin(ang)[:, :, None, :]
    x1, x2 = x[..., :half], x[..., half:]
    return jnp.concatenate([x1 * cos - x2 * sin, x1 * sin + x2 * cos], axis=-1)


def retention(q, k, v, positions):
    B, S, H, dk = q.shape
    dv = v.shape[-1]
    C = RET_CHUNK
    n = S // C
    q = rotary(q.astype(jnp.float32), positions)
    k = rotary(k.astype(jnp.float32), positions) * (dk ** -0.5)
    v = v.astype(jnp.float32)

    def chunks(t):
        return t.reshape(B, n, C, H, t.shape[-1]).transpose(1, 0, 3, 2, 4)

    log_g = jnp.log1p(-(2.0 ** (-5.0 - jnp.arange(H, dtype=jnp.float32))))
    idx = jnp.arange(C, dtype=jnp.float32)
    dist = idx[:, None] - idx[None, :]
    intra_decay = jnp.where(dist >= 0, jnp.exp(log_g[:, None, None] * jnp.maximum(dist, 0.0)), 0.0)
    q_decay = jnp.exp(log_g[:, None] * (idx + 1.0))[:, :, None]
    k_decay = jnp.exp(log_g[:, None] * (C - 1.0 - idx))[:, :, None]
    chunk_decay = jnp.exp(log_g * C)[:, None, None]

    def step(state, qkv):
        qc, kc, vc = qkv
        scores = jnp.einsum('bhid,bhjd->bhij', qc, kc) * intra_decay
        out = (jnp.einsum('bhij,bhje->bhie', scores, vc)
               + jnp.einsum('bhid,bhde->bhie', qc * q_decay, state))
        state = chunk_decay * state + jnp.einsum('bhjd,bhje->bhde', kc * k_decay, vc)
        return state, out

    state0 = jnp.zeros((B, H, dk, dv), jnp.float32)
    _, out = lax.scan(step, state0, (chunks(q), chunks(k), chunks(v)))
    return out.transpose(1, 0, 3, 2, 4).reshape(B, S, H, dv)


def diff_attention(q, k, v, lam):
    B, S, H, _, d = q.shape
    nb = S // Q_BLOCK
    kh = k.transpose(0, 2, 3, 1, 4)
    vh = v.transpose(0, 2, 1, 3)
    qb = q.transpose(0, 2, 3, 1, 4).reshape(B, H, 2, nb, Q_BLOCK, d).transpose(3, 0, 1, 2, 4, 5)
    starts = jnp.arange(nb) * Q_BLOCK
    kpos = jnp.arange(S)
    scale = d ** -0.5

    def block(args):
        qblk, start = args
        s = jnp.einsum('bhcqd,bhckd->bhcqk', qblk, kh).astype(jnp.float32) * scale
        qpos = start + jnp.arange(Q_BLOCK)
        s = jnp.where(kpos[None, :] <= qpos[:, None], s, NEG)
        p = jax.nn.softmax(s, axis=-1)
        a = p[:, :, 0] - lam * p[:, :, 1]
        return jnp.einsum('bhqk,bhke->bhqe', a.astype(vh.dtype), vh)

    out = lax.map(block, (qb, starts))
    return out.transpose(1, 0, 3, 2, 4).reshape(B, S, H, 2 * d)


def memory_attention(q, k, v):
    s = jnp.einsum('bshd,bmhd->bhsm', q, k).astype(jnp.float32) * (q.shape[-1] ** -0.5)
    p = jax.nn.softmax(s, axis=-1)
    return jnp.einsum('bhsm,bmhd->bshd', p.astype(v.dtype), v)


def setup_inputs(seed: int = 0) -> dict:
    key = jax.random.key(seed)
    ks = jax.random.split(key, 32)
    f32 = jnp.float32

    def normal(k, shape, scale):
        return jax.random.normal(k, shape, f32) * scale

    def gain(k, shape):
        return 1.0 + 0.02 * jax.random.normal(k, shape, f32)

    L = DEPTH
    start = jax.random.randint(ks[2], (BATCH, 1), 0, 1024)
    positions = (start + jnp.arange(SEQ)[None, :]).astype(jnp.int32)
    return {
        "x": normal(ks[0], (BATCH, SEQ, D_MODEL), 1.0),
        "mem": normal(ks[1], (BATCH, MEM_LEN, D_MODEL), 1.0),
        "positions": positions,
        "g_ffn1": gain(ks[3], (L, D_MODEL)),
        "w_ffn1_in": normal(ks[4], (L, D_MODEL, 2 * D_FF), D_MODEL ** -0.5),
        "w_ffn1_out": normal(ks[5], (L, D_FF, D_MODEL), D_FF ** -0.5),
        "g_mix": gain(ks[6], (L, D_MODEL)),
        "w_in": normal(ks[7], (L, D_MODEL, IN_WIDTH), D_MODEL ** -0.5),
        "g_diff_q": gain(ks[8], (L, DIFF_D)),
        "g_diff_k": gain(ks[9], (L, DIFF_D)),
        "lam_q1": normal(ks[10], (L, DIFF_D), 0.1),
        "lam_k1": normal(ks[11], (L, DIFF_D), 0.1),
        "lam_q2": normal(ks[12], (L, DIFF_D), 0.1),
        "lam_k2": normal(ks[13], (L, DIFF_D), 0.1),
        "g_diff_out": gain(ks[14], (L, 2 * DIFF_D)),
        "g_mem_q": gain(ks[15], (L, MEM_D)),
        "g_mem_k": gain(ks[16], (L, MEM_D)),
        "g_mem": gain(ks[17], (L, D_MODEL)),
        "w_mem_kv": normal(ks[18], (L, D_MODEL, 2 * MEM_Q_W), D_MODEL ** -0.5),
        "w_br_ret": normal(ks[19], (L, RET_V_W, D_MODEL), RET_V_W ** -0.5),
        "w_br_diff": normal(ks[20], (L, DIFF_V_W, D_MODEL), DIFF_V_W ** -0.5),
        "w_br_mem": normal(ks[21], (L, MEM_Q_W, D_MODEL), MEM_Q_W ** -0.5),
        "w_o": normal(ks[22], (L, D_MODEL, D_MODEL), D_MODEL ** -0.5),
        "g_ffn2": gain(ks[23], (L, D_MODEL)),
        "w_ffn2_in": normal(ks[24], (L, D_MODEL, 2 * D_FF), D_MODEL ** -0.5),
        "w_ffn2_out": normal(ks[25], (L, D_FF, D_MODEL), D_FF ** -0.5),
    }


def reference(x, mem, positions, g_ffn1, w_ffn1_in, w_ffn1_out, g_mix, w_in,
              g_diff_q, g_diff_k, lam_q1, lam_k1, lam_q2, lam_k2, g_diff_out,
              g_mem_q, g_mem_k, g_mem, w_mem_kv, w_br_ret, w_br_diff, w_br_mem,
              w_o, g_ffn2, w_ffn2_in, w_ffn2_out):
    B, S, _ = x.shape
    M = mem.shape[1]
    offsets = np.cumsum(IN_SPLITS)[:-1].tolist()
    for l in range(DEPTH):
        lam_init = 0.8 - 0.6 * math.exp(-0.3 * l)

        x = x + FFN_RES * swiglu_ffn(rms_norm(x, g_ffn1[l]), w_ffn1_in[l], w_ffn1_out[l])

        h = rms_norm(x, g_mix[l])
        rq, rk, rv, rg, dq, dk, dv, mq, gates = jnp.split(h @ w_in[l], offsets, axis=-1)

        ret = retention(rq.reshape(B, S, RET_HEADS, RET_DK), rk.reshape(B, S, RET_HEADS, RET_DK),
                        rv.reshape(B, S, RET_HEADS, RET_DV), positions)
        ret = rms_norm(ret) * jax.nn.silu(rg.reshape(B, S, RET_HEADS, RET_DV).astype(jnp.float32))
        ret = ret.reshape(B, S, RET_V_W).astype(x.dtype)

        lam = (jnp.exp(jnp.sum(lam_q1[l].astype(jnp.float32) * lam_k1[l].astype(jnp.float32)))
               - jnp.exp(jnp.sum(lam_q2[l].astype(jnp.float32) * lam_k2[l].astype(jnp.float32)))
               + lam_init)
        dqn = rms_norm(dq.reshape(B, S, DIFF_HEADS, 2, DIFF_D), g_diff_q[l])
        dkn = rms_norm(dk.reshape(B, S, DIFF_HEADS, 2, DIFF_D), g_diff_k[l])
        dif = diff_attention(dqn, dkn, dv.reshape(B, S, DIFF_HEADS, 2 * DIFF_D), lam)
        dif = (rms_norm(dif, g_diff_out[l]) * (1.0 - lam_init)).reshape(B, S, DIFF_V_W)

        mk, mv = jnp.split(rms_norm(mem, g_mem[l]) @ w_mem_kv[l], 2, axis=-1)
        mo = memory_attention(rms_norm(mq.reshape(B, S, MEM_HEADS, MEM_D), g_mem_q[l]),
                              rms_norm(mk.reshape(B, M, MEM_HEADS, MEM_D), g_mem_k[l]),
                              mv.reshape(B, M, MEM_HEADS, MEM_D)).reshape(B, S, MEM_Q_W)

        gt = jax.nn.sigmoid(gates.reshape(B, S, N_BRANCH, D_MODEL))
        y = (gt[:, :, 0] * (ret @ w_br_ret[l])
             + gt[:, :, 1] * (dif @ w_br_diff[l])
             + gt[:, :, 2] * (mo @ w_br_mem[l]))
        x = x + y @ w_o[l]

        x = x + FFN_RES * swiglu_ffn(rms_norm(x, g_ffn2[l]), w_ffn2_in[l], w_ffn2_out[l])
    return x
```

```python
import functools
import math

import jax
import jax.numpy as jnp
import numpy as np
from jax import lax
from jax.experimental import pallas as pl
from jax.experimental.pallas import tpu as pltpu

F32 = jnp.float32
BF16 = jnp.bfloat16

D_MODEL = 1024
D_FF = 2816
FFN_RES = 0.5
EPS = 1e-6
NEG = -1e30
ROPE_BASE = 10000.0

RET_HEADS = 4
RET_DK = 256
RET_DV = 512
DIFF_HEADS = 8
DIFF_D = 64
MEM_HEADS = 4
MEM_D = 256

V7X_VMEM_LIMIT_BYTES = 56 * 1024 * 1024

TM = 512
FFN_CHUNK = 256
RET_C = 256
DIFF_TQ = 256
MEM_TQ = 512


def _params(sem):
    return pltpu.CompilerParams(dimension_semantics=sem,
                                vmem_limit_bytes=V7X_VMEM_LIMIT_BYTES)


def _resident(shape):
    nd = len(shape)
    return pl.BlockSpec(shape, lambda *_: (0,) * nd, pipeline_mode=pl.Buffered(1))


def _rms(x):
    return x * lax.rsqrt(jnp.mean(x * x, axis=-1, keepdims=True) + EPS)


def _dot(a, b):
    return jnp.dot(a, b, preferred_element_type=F32)


def _dot_nt(a, b):
    return lax.dot_general(a, b, (((1,), (1,)), ((), ())), preferred_element_type=F32)


def _dot_tn(a, b):
    return lax.dot_general(a, b, (((0,), (0,)), ((), ())), preferred_element_type=F32)


def _ffn_kernel(*refs, emit_next):
    if emit_next:
        x_ref, g_ref, win_ref, wout_ref, gn_ref, o_ref, h_ref = refs
    else:
        x_ref, g_ref, win_ref, wout_ref, o_ref = refs
    x = x_ref[...]
    hb = (_rms(x) * g_ref[...]).astype(BF16)
    acc = jnp.zeros(x.shape, F32)
    for j in range(D_FF // FFN_CHUNK):
        lo = j * FFN_CHUNK
        a = _dot(hb, win_ref[:, lo:lo + FFN_CHUNK])
        b = _dot(hb, win_ref[:, D_FF + lo:D_FF + lo + FFN_CHUNK])
        act = (a * jax.nn.sigmoid(a) * b).astype(BF16)
        acc = acc + _dot(act, wout_ref[lo:lo + FFN_CHUNK, :])
    y = x + FFN_RES * acc
    o_ref[...] = y
    if emit_next:
        h_ref[...] = (_rms(y) * gn_ref[...]).astype(BF16)


def _ffn(x, g, w_in, w_out, g_next=None):
    t = x.shape[0]
    emit = g_next is not None
    row = pl.BlockSpec((TM, D_MODEL), lambda i: (i, 0))
    vec = pl.BlockSpec((1, D_MODEL), lambda i: (0, 0))
    in_specs = [row, vec, _resident(w_in.shape), _resident(w_out.shape)]
    args = [x, g, w_in, w_out]
    out_shape = [jax.ShapeDtypeStruct((t, D_MODEL), F32)]
    out_specs = [row]
    if emit:
        in_specs.append(vec)
        args.append(g_next)
        out_shape.append(jax.ShapeDtypeStruct((t, D_MODEL), BF16))
        out_specs.append(row)
    return pl.pallas_call(
        functools.partial(_ffn_kernel, emit_next=emit),
        grid=(t // TM,),
        in_specs=in_specs,
        out_specs=out_specs,
        out_shape=out_shape,
        compiler_params=_params(("parallel",)),
        name="ffn_next" if emit else "ffn",
    )(*args)


def _group_norm_128(y, gain, group):
    cols = y.shape[1]
    outs = []
    if group == 64:
        lane = lax.broadcasted_iota(jnp.int32, (1, 128), 1)
        first = lane < 64
        for c in range(cols // 128):
            blk = y[:, c * 128:(c + 1) * 128]
            sq = blk * blk
            s_lo = jnp.sum(jnp.where(first, sq, 0.0), axis=-1, keepdims=True)
            s_hi = jnp.sum(jnp.where(first, 0.0, sq), axis=-1, keepdims=True)
            r = jnp.where(first, lax.rsqrt(s_lo * (1.0 / 64) + EPS),
                          lax.rsqrt(s_hi * (1.0 / 64) + EPS))
            outs.append(blk * r)
    else:
        for c in range(cols // group):
            blk = y[:, c * group:(c + 1) * group]
            outs.append(_rms(blk))
    return jnp.concatenate(outs, axis=1) * gain


def _proj_kernel(*refs, kind):
    if kind == "rot":
        h_ref, w_ref, pos_ref, inv_ref, o_ref = refs
    elif kind in ("norm64", "norm256"):
        h_ref, w_ref, gain_ref, o_ref = refs
    else:
        h_ref, w_ref, o_ref = refs
    y = _dot(h_ref[...], w_ref[...])
    if kind == "rot":
        ang = pos_ref[...].astype(F32) * inv_ref[...]
        cos = jnp.cos(ang)
        sin = jnp.sin(ang)
        half = RET_DK // 2
        outs = []
        for hd in range(2 * RET_HEADS):
            x1 = y[:, hd * RET_DK:hd * RET_DK + half]
            x2 = y[:, hd * RET_DK + half:(hd + 1) * RET_DK]
            r1 = x1 * cos - x2 * sin
            r2 = x1 * sin + x2 * cos
            if hd >= RET_HEADS:
                r1 = r1 * (RET_DK ** -0.5)
                r2 = r2 * (RET_DK ** -0.5)
            outs += [r1, r2]
        y = jnp.concatenate(outs, axis=1)
    elif kind == "silu":
        y = y * jax.nn.sigmoid(y)
    elif kind == "sigmoid":
        y = jax.nn.sigmoid(y)
    elif kind == "norm64":
        y = _group_norm_128(y, gain_ref[...], 64)
    elif kind == "norm256":
        y = _group_norm_128(y, gain_ref[...], 256)
    o_ref[...] = y.astype(BF16)


def _proj(h, w_in, start_col, bw, nblk, kind, extra=()):
    t = h.shape[0]
    assert start_col % bw == 0
    sb = start_col // bw
    in_specs = [pl.BlockSpec((TM, D_MODEL), lambda j, i: (i, 0)),
                pl.BlockSpec((D_MODEL, bw), lambda j, i: (0, sb + j))]
    args = [h, w_in]
    if kind == "rot":
        pos, inv = extra
        in_specs += [pl.BlockSpec((TM, 1), lambda j, i: (i, 0)),
                     pl.BlockSpec((1, 128), lambda j, i: (0, 0))]
        args += [pos, inv]
    elif kind in ("norm64", "norm256"):
        (gain,) = extra
        in_specs.append(pl.BlockSpec((1, bw), lambda j, i: (0, j)))
        args.append(gain)
    return pl.pallas_call(
        functools.partial(_proj_kernel, kind=kind),
        grid=(nblk, t // TM),
        in_specs=in_specs,
        out_specs=pl.BlockSpec((TM, bw), lambda j, i: (i, j)),
        out_shape=jax.ShapeDtypeStruct((t, nblk * bw), BF16),
        compiler_params=_params(("parallel", "parallel")),
        name="proj_" + kind,
    )(*args)


def _ret_kernel(q_ref, k_ref, v_ref, g_ref, dec_ref, qd_ref, kd_ref, o_ref, st_ref):
    c = pl.program_id(2)

    @pl.when(c == 0)
    def _():
        st_ref[...] = jnp.zeros_like(st_ref)

    q = q_ref[...]
    k = k_ref[...]
    v = v_ref[...]
    qd = qd_ref[...]
    s = _dot_nt(q, k) * dec_ref[...]
    st = st_ref[...]
    out = _dot(s.astype(BF16), v) + _dot(q, st.astype(BF16)) * qd
    kdec = (k.astype(F32) * kd_ref[...]).astype(BF16)
    chunk_decay = qd[RET_C - 1:RET_C, :]
    st_ref[...] = chunk_decay * st + _dot_tn(kdec, v)
    o_ref[...] = (_rms(out) * g_ref[...].astype(F32)).astype(BF16)


def _retention(qk, v, gate, dec, qd, kd, batch, seq):
    nc = seq // RET_C
    t = batch * seq
    row = lambda b, h, c: (b * nc + c, h)
    return pl.pallas_call(
        _ret_kernel,
        grid=(batch, RET_HEADS, nc),
        in_specs=[pl.BlockSpec((RET_C, RET_DK), row),
                  pl.BlockSpec((RET_C, RET_DK), lambda b, h, c: (b * nc + c, RET_HEADS + h)),
                  pl.BlockSpec((RET_C, RET_DV), row),
                  pl.BlockSpec((RET_C, RET_DV), row),
                  pl.BlockSpec((None, RET_C, RET_C), lambda b, h, c: (h, 0, 0)),
                  pl.BlockSpec((None, RET_C, 1), lambda b, h, c: (h, 0, 0)),
                  pl.BlockSpec((None, RET_C, 1), lambda b, h, c: (h, 0, 0))],
        out_specs=pl.BlockSpec((RET_C, RET_DV), row),
        out_shape=jax.ShapeDtypeStruct((t, RET_HEADS * RET_DV), BF16),
        scratch_shapes=[pltpu.VMEM((RET_DK, RET_DV), F32)],
        compiler_params=_params(("parallel", "parallel", "arbitrary")),
        name="retention",
    )(qk, qk, v, gate, dec, qd, kd)


def _diff_kernel(q_ref, k_ref, v_ref, lq1_ref, lk1_ref, lq2_ref, lk2_ref, go_ref, o_ref,
                 m_ref, l_ref, acc_ref, *, lam_init):
    i = pl.program_id(2)
    tq = DIFF_TQ
    lane = lax.broadcasted_iota(jnp.int32, (1, 2 * DIFF_D), 1)
    q = q_ref[...]
    zero = jnp.zeros_like(q)
    qs = jnp.concatenate([jnp.where(lane < DIFF_D, q, zero),
                          jnp.where(lane < DIFF_D, zero, q)], axis=0)
    m_ref[...] = jnp.full(m_ref.shape, NEG, F32)
    l_ref[...] = jnp.zeros(l_ref.shape, F32)
    acc_ref[...] = jnp.zeros(acc_ref.shape, F32)

    def step(j, masked):
        off = pl.multiple_of(j * tq, tq)
        kb = k_ref[pl.ds(off, tq), :]
        vb = v_ref[pl.ds(off, tq), :]
        s = _dot_nt(qs, kb)
        if masked:
            r = lax.broadcasted_iota(jnp.int32, (tq, tq), 0)
            cidx = lax.broadcasted_iota(jnp.int32, (tq, tq), 1)
            keep = cidx <= r
            keep = jnp.concatenate([keep, keep], axis=0)
            s = jnp.where(keep, s, NEG)
        m_old = m_ref[...]
        m_new = jnp.maximum(m_old, jnp.max(s, axis=-1, keepdims=True))
        alpha = jnp.exp(m_old - m_new)
        p = jnp.exp(s - m_new)
        l_ref[...] = alpha * l_ref[...] + jnp.sum(p, axis=-1, keepdims=True)
        acc_ref[...] = alpha * acc_ref[...] + _dot(p.astype(BF16), vb)
        m_ref[...] = m_new

    def body(j, carry):
        step(j, False)
        return carry

    lax.fori_loop(0, i, body, 0)
    step(i, True)

    lam = (jnp.exp(jnp.sum(lq1_ref[...] * lk1_ref[...], axis=-1, keepdims=True))
           - jnp.exp(jnp.sum(lq2_ref[...] * lk2_ref[...], axis=-1, keepdims=True))
           + lam_init)
    o = acc_ref[...] / l_ref[...]
    d = o[:tq] - lam * o[tq:]
    o_ref[...] = (_rms(d) * go_ref[...]).astype(BF16)


def _diff_attention(qk, v, lq1, lk1, lq2, lk2, g_out, batch, seq, lam_init):
    nq = seq // DIFF_TQ
    t = batch * seq
    vec64 = pl.BlockSpec((1, DIFF_D), lambda b, h, i: (0, 0))
    return pl.pallas_call(
        functools.partial(_diff_kernel, lam_init=lam_init),
        grid=(batch, DIFF_HEADS, nq),
        in_specs=[pl.BlockSpec((DIFF_TQ, 2 * DIFF_D), lambda b, h, i: (b * nq + i, h)),
                  pl.BlockSpec((seq, 2 * DIFF_D), lambda b, h, i: (b, DIFF_HEADS + h)),
                  pl.BlockSpec((seq, 2 * DIFF_D), lambda b, h, i: (b, h)),
                  vec64, vec64, vec64, vec64,
                  pl.BlockSpec((1, 2 * DIFF_D), lambda b, h, i: (0, 0))],
        out_specs=pl.BlockSpec((DIFF_TQ, 2 * DIFF_D), lambda b, h, i: (b * nq + i, h)),
        out_shape=jax.ShapeDtypeStruct((t, DIFF_HEADS * 2 * DIFF_D), BF16),
        scratch_shapes=[pltpu.VMEM((2 * DIFF_TQ, 1), F32),
                        pltpu.VMEM((2 * DIFF_TQ, 1), F32),
                        pltpu.VMEM((2 * DIFF_TQ, 2 * DIFF_D), F32)],
        compiler_params=_params(("parallel", "parallel", "arbitrary")),
        name="diff_attention",
    )(qk, qk, v, lq1, lk1, lq2, lk2, g_out)


def _memkv_kernel(mem_ref, g_ref, w_ref, gk_ref, k_ref, v_ref):
    mn = (_rms(mem_ref[...]) * g_ref[...]).astype(BF16)
    kv = _dot(mn, w_ref[...])
    width = MEM_HEADS * MEM_D
    k_ref[...] = _group_norm_128(kv[:, :width], gk_ref[...], MEM_D).astype(BF16)
    v_ref[...] = kv[:, width:].astype(BF16)


def _memkv(mem2d, g_mem, w_kv, gk, batch, mlen):
    width = MEM_HEADS * MEM_D
    row = pl.BlockSpec((mlen, width), lambda b: (b, 0))
    vec = pl.BlockSpec((1, width), lambda b: (0, 0))
    return pl.pallas_call(
        _memkv_kernel,
        grid=(batch,),
        in_specs=[pl.BlockSpec((mlen, D_MODEL), lambda b: (b, 0)), vec,
                  _resident(w_kv.shape), vec],
        out_specs=[row, row],
        out_shape=[jax.ShapeDtypeStruct((batch * mlen, width), BF16)] * 2,
        compiler_params=_params(("parallel",)),
        name="memkv",
    )(mem2d, g_mem, w_kv, gk)


def _mem_kernel(q_ref, k_ref, v_ref, o_ref):
    outs = []
    for hd in range(MEM_HEADS):
        sl = slice(hd * MEM_D, (hd + 1) * MEM_D)
        s = _dot_nt(q_ref[:, sl], k_ref[:, sl])
        p = jnp.exp(s - jnp.max(s, axis=-1, keepdims=True))
        l = jnp.sum(p, axis=-1, keepdims=True)
        outs.append(_dot(p.astype(BF16), v_ref[:, sl]) / l)
    o_ref[...] = jnp.concatenate(outs, axis=1).astype(BF16)


def _mem_attention(q, k, v, batch, seq, mlen):
    nq = seq // MEM_TQ
    width = MEM_HEADS * MEM_D
    qrow = pl.BlockSpec((MEM_TQ, width), lambda b, i: (b * nq + i, 0))
    kv = pl.BlockSpec((mlen, width), lambda b, i: (b, 0))
    return pl.pallas_call(
        _mem_kernel,
        grid=(batch, nq),
        in_specs=[qrow, kv, kv],
        out_specs=qrow,
        out_shape=jax.ShapeDtypeStruct((batch * seq, width), BF16),
        compiler_params=_params(("parallel", "parallel")),
        name="mem_attention",
    )(q, k, v)


def _merge_kernel(x_ref, ret_ref, dif_ref, mo_ref, gt_ref, wr_ref, wd_ref, wm_ref, wo_ref, o_ref):
    gt = gt_ref[...].astype(F32)
    y = (gt[:, :D_MODEL] * _dot(ret_ref[...], wr_ref[...])
         + gt[:, D_MODEL:2 * D_MODEL] * _dot(dif_ref[...], wd_ref[...])
         + gt[:, 2 * D_MODEL:] * _dot(mo_ref[...], wm_ref[...]))
    o_ref[...] = x_ref[...] + _dot(y.astype(BF16), wo_ref[...])


def _merge(x, ret, dif, mo, gates, wr, wd, wm, wo):
    t = x.shape[0]
    row = lambda w: pl.BlockSpec((TM, w), lambda i: (i, 0))
    return pl.pallas_call(
        _merge_kernel,
        grid=(t // TM,),
        in_specs=[row(D_MODEL), row(ret.shape[1]), row(D_MODEL), row(D_MODEL), row(3 * D_MODEL),
                  _resident(wr.shape), _resident(wd.shape), _resident(wm.shape),
                  _resident(wo.shape)],
        out_specs=row(D_MODEL),
        out_shape=jax.ShapeDtypeStruct((t, D_MODEL), F32),
        compiler_params=_params(("parallel",)),
        name="merge",
    )(x, ret, dif, mo, gates, wr, wd, wm, wo)


def _retention_constants():
    h = np.arange(RET_HEADS, dtype=np.float64)
    log_g = np.log1p(-(2.0 ** (-5.0 - h)))
    idx = np.arange(RET_C, dtype=np.float64)
    dist = idx[:, None] - idx[None, :]
    dec = np.where(dist >= 0, np.exp(log_g[:, None, None] * np.maximum(dist, 0.0)), 0.0)
    qd = np.exp(log_g[:, None] * (idx + 1.0))[:, :, None]
    kd = np.exp(log_g[:, None] * (RET_C - 1.0 - idx))[:, :, None]
    return (jnp.asarray(dec, F32), jnp.asarray(qd, F32), jnp.asarray(kd, F32))


def kernel(x, mem, positions, g_ffn1, w_ffn1_in, w_ffn1_out, g_mix, w_in, g_diff_q, g_diff_k,
           lam_q1, lam_k1, lam_q2, lam_k2, g_diff_out, g_mem_q, g_mem_k, g_mem, w_mem_kv,
           w_br_ret, w_br_diff, w_br_mem, w_o, g_ffn2, w_ffn2_in, w_ffn2_out):
    batch, seq, _ = x.shape
    mlen = mem.shape[1]
    t = batch * seq
    depth = g_ffn1.shape[0]
    half = RET_DK // 2
    inv = jnp.asarray(ROPE_BASE ** (-np.arange(half, dtype=np.float64) / half), F32)[None, :]
    dec, qd, kd = _retention_constants()
    pos = positions.reshape(t, 1)
    mem2d = mem.reshape(batch * mlen, D_MODEL)
    xf = x.reshape(t, D_MODEL)
    bf = lambda w: w.astype(BF16)
    vec = lambda g: g.astype(F32)[None, :]

    for l in range(depth):
        lam_init = 0.8 - 0.6 * math.exp(-0.3 * l)
        x1, h = _ffn(xf, vec(g_ffn1[l]), bf(w_ffn1_in[l]), bf(w_ffn1_out[l]), vec(g_mix[l]))

        w = bf(w_in[l])
        rqk = _proj(h, w, 0, 2048, 1, "rot", (pos, inv))
        rv = _proj(h, w, 2048, 2048, 1, "plain")
        rg = _proj(h, w, 4096, 2048, 1, "silu")
        qk_gain = jnp.concatenate([jnp.tile(g_diff_q[l].astype(F32), 2 * DIFF_HEADS) * (DIFF_D ** -0.5),
                                   jnp.tile(g_diff_k[l].astype(F32), 2 * DIFF_HEADS)])[None, :]
        dqk = _proj(h, w, 6144, 2048, 1, "norm64", (qk_gain,))
        dv = _proj(h, w, 8192, 1024, 1, "plain")
        mq_gain = (jnp.tile(g_mem_q[l].astype(F32), MEM_HEADS) * (MEM_D ** -0.5))[None, :]
        mq = _proj(h, w, 9216, 1024, 1, "norm256", (mq_gain,))
        gates = _proj(h, w, 10240, 1024, 3, "sigmoid")

        ret = _retention(rqk, rv, rg, dec, qd, kd, batch, seq)
        go = (g_diff_out[l].astype(F32) * (1.0 - lam_init))[None, :]
        dif = _diff_attention(dqk, dv, vec(lam_q1[l]), vec(lam_k1[l]), vec(lam_q2[l]),
                              vec(lam_k2[l]), go, batch, seq, lam_init)
        mk, mv = _memkv(mem2d, vec(g_mem[l]), bf(w_mem_kv[l]),
                        jnp.tile(g_mem_k[l].astype(F32), MEM_HEADS)[None, :], batch, mlen)
        mo = _mem_attention(mq, mk, mv, batch, seq, mlen)

        x2 = _merge(x1, ret, dif, mo, gates, bf(w_br_ret[l]), bf(w_br_diff[l]),
                    bf(w_br_mem[l]), bf(w_o[l]))
        (xf,) = _ffn(x2, vec(g_ffn2[l]), bf(w_ffn2_in[l]), bf(w_ffn2_out[l]))
    return xf.reshape(batch, seq, D_MODEL)
```

```python
import functools
import math

import jax
import jax.numpy as jnp
import numpy as np
from jax import lax
from jax.experimental import pallas as pl
from jax.experimental.pallas import tpu as pltpu

F32 = jnp.float32
BF16 = jnp.bfloat16

D_MODEL = 1024
D_FF = 2816
FFN_RES = 0.5
EPS = 1e-6
NEG = -1e30
ROPE_BASE = 10000.0

RET_HEADS = 4
RET_DK = 256
RET_DV = 512
DIFF_HEADS = 8
DIFF_D = 64
MEM_HEADS = 4
MEM_D = 256

V7X_VMEM_LIMIT_BYTES = 56 * 1024 * 1024

TM = 512
FFN_CHUNK = 256
RET_C = 256
DIFF_TQ = 256
MEM_TQ = 512


def _params(sem):
    return pltpu.CompilerParams(dimension_semantics=sem,
                                vmem_limit_bytes=V7X_VMEM_LIMIT_BYTES)


def _resident(shape):
    nd = len(shape)
    return pl.BlockSpec(shape, lambda *_: (0,) * nd, pipeline_mode=pl.Buffered(1))


def _rms(x):
    return x * lax.rsqrt(jnp.mean(x * x, axis=-1, keepdims=True) + EPS)


def _dot(a, b):
    return jnp.dot(a, b, preferred_element_type=F32)


def _dot_nt(a, b):
    return lax.dot_general(a, b, (((1,), (1,)), ((), ())), preferred_element_type=F32)


def _dot_tn(a, b):
    return lax.dot_general(a, b, (((0,), (0,)), ((), ())), preferred_element_type=F32)


def _ffn_kernel(*refs, emit_next):
    if emit_next:
        x_ref, g_ref, win_ref, wout_ref, gn_ref, o_ref, h_ref = refs
    else:
        x_ref, g_ref, win_ref, wout_ref, o_ref = refs
    x = x_ref[...]
    hb = (_rms(x) * g_ref[...]).astype(BF16)
    acc = jnp.zeros(x.shape, F32)
    for j in range(D_FF // FFN_CHUNK):
        lo = j * FFN_CHUNK
        a = _dot(hb, win_ref[:, lo:lo + FFN_CHUNK])
        b = _dot(hb, win_ref[:, D_FF + lo:D_FF + lo + FFN_CHUNK])
        act = (a * jax.nn.sigmoid(a) * b).astype(BF16)
        acc = acc + _dot(act, wout_ref[lo:lo + FFN_CHUNK, :])
    y = x + FFN_RES * acc
    o_ref[...] = y
    if emit_next:
        h_ref[...] = (_rms(y) * gn_ref[...]).astype(BF16)


def _ffn(x, g, w_in, w_out, g_next=None):
    t = x.shape[0]
    emit = g_next is not None
    row = pl.BlockSpec((TM, D_MODEL), lambda i: (i, 0))
    vec = pl.BlockSpec((1, D_MODEL), lambda i: (0, 0))
    in_specs = [row, vec, _resident(w_in.shape), _resident(w_out.shape)]
    args = [x, g, w_in, w_out]
    out_shape = [jax.ShapeDtypeStruct((t, D_MODEL), F32)]
    out_specs = [row]
    if emit:
        in_specs.append(vec)
        args.append(g_next)
        out_shape.append(jax.ShapeDtypeStruct((t, D_MODEL), BF16))
        out_specs.append(row)
    return pl.pallas_call(
        functools.partial(_ffn_kernel, emit_next=emit),
        grid=(t // TM,),
        in_specs=in_specs,
        out_specs=out_specs,
        out_shape=out_shape,
        compiler_params=_params(("parallel",)),
        name="ffn_next" if emit else "ffn",
    )(*args)


def _group_norm_128(y, gain, group):
    cols = y.shape[1]
    outs = []
    if group == 64:
        lane = lax.broadcasted_iota(jnp.int32, (1, 128), 1)
        first = lane < 64
        for c in range(cols // 128):
            blk = y[:, c * 128:(c + 1) * 128]
            sq = blk * blk
            s_lo = jnp.sum(jnp.where(first, sq, 0.0), axis=-1, keepdims=True)
            s_hi = jnp.sum(jnp.where(first, 0.0, sq), axis=-1, keepdims=True)
            r = jnp.where(first, lax.rsqrt(s_lo * (1.0 / 64) + EPS),
                          lax.rsqrt(s_hi * (1.0 / 64) + EPS))
            outs.append(blk * r)
    else:
        for c in range(cols // group):
            blk = y[:, c * group:(c + 1) * group]
            outs.append(_rms(blk))
    return jnp.concatenate(outs, axis=1) * gain


def _proj_kernel(*refs, kind):
    if kind == "rot":
        h_ref, w_ref, pos_ref, inv_ref, o_ref = refs
    elif kind in ("norm64", "norm256"):
        h_ref, w_ref, gain_ref, o_ref = refs
    else:
        h_ref, w_ref, o_ref = refs
    y = _dot(h_ref[...], w_ref[...])
    if kind == "rot":
        ang = pos_ref[...].astype(F32) * inv_ref[...]
        cos = jnp.cos(ang)
        sin = jnp.sin(ang)
        half = RET_DK // 2
        outs = []
        for hd in range(2 * RET_HEADS):
            x1 = y[:, hd * RET_DK:hd * RET_DK + half]
            x2 = y[:, hd * RET_DK + half:(hd + 1) * RET_DK]
            r1 = x1 * cos - x2 * sin
            r2 = x1 * sin + x2 * cos
            if hd >= RET_HEADS:
                r1 = r1 * (RET_DK ** -0.5)
                r2 = r2 * (RET_DK ** -0.5)
            outs += [r1, r2]
        y = jnp.concatenate(outs, axis=1)
    elif kind == "silu":
        y = y * jax.nn.sigmoid(y)
    elif kind == "sigmoid":
        y = jax.nn.sigmoid(y)
    elif kind == "norm64":
        y = _group_norm_128(y, gain_ref[...], 64)
    elif kind == "norm256":
        y = _group_norm_128(y, gain_ref[...], 256)
    o_ref[...] = y.astype(BF16)


def _proj(h, w_in, start_col, bw, nblk, kind, extra=()):
    t = h.shape[0]
    assert start_col % bw == 0
    sb = start_col // bw
    in_specs = [pl.BlockSpec((TM, D_MODEL), lambda j, i: (i, 0)),
                pl.BlockSpec((D_MODEL, bw), lambda j, i: (0, sb + j))]
    args = [h, w_in]
    if kind == "rot":
        pos, inv = extra
        in_specs += [pl.BlockSpec((TM, 1), lambda j, i: (i, 0)),
                     pl.BlockSpec((1, 128), lambda j, i: (0, 0))]
        args += [pos, inv]
    elif kind in ("norm64", "norm256"):
        (gain,) = extra
        in_specs.append(pl.BlockSpec((1, bw), lambda j, i: (0, j)))
        args.append(gain)
    return pl.pallas_call(
        functools.partial(_proj_kernel, kind=kind),
        grid=(nblk, t // TM),
        in_specs=in_specs,
        out_specs=pl.BlockSpec((TM, bw), lambda j, i: (i, j)),
        out_shape=jax.ShapeDtypeStruct((t, nblk * bw), BF16),
        compiler_params=_params(("parallel", "parallel")),
        name="proj_" + kind,
    )(*args)


def _ret_kernel(q_ref, k_ref, v_ref, g_ref, dec_ref, qd_ref, kd_ref, o_ref, st_ref):
    c = pl.program_id(2)

    @pl.when(c == 0)
    def _():
        st_ref[...] = jnp.zeros_like(st_ref)

    q = q_ref[...]
    k = k_ref[...]
    v = v_ref[...]
    qd = qd_ref[...]
    s = _dot_nt(q, k) * dec_ref[...]
    st = st_ref[...]
    out = _dot(s.astype(BF16), v) + _dot(q, st.astype(BF16)) * qd
    kdec = (k.astype(F32) * kd_ref[...]).astype(BF16)
    chunk_decay = qd[RET_C - 1:RET_C, :]
    st_ref[...] = chunk_decay * st + _dot_tn(kdec, v)
    o_ref[...] = (_rms(out) * g_ref[...].astype(F32)).astype(BF16)


def _retention(qk, v, gate, dec, qd, kd, batch, seq):
    nc = seq // RET_C
    t = batch * seq
    row = lambda b, h, c: (b * nc + c, h)
    return pl.pallas_call(
        _ret_kernel,
        grid=(batch, RET_HEADS, nc),
        in_specs=[pl.BlockSpec((RET_C, RET_DK), row),
                  pl.BlockSpec((RET_C, RET_DK), lambda b, h, c: (b * nc + c, RET_HEADS + h)),
                  pl.BlockSpec((RET_C, RET_DV), row),
                  pl.BlockSpec((RET_C, RET_DV), row),
                  pl.BlockSpec((None, RET_C, RET_C), lambda b, h, c: (h, 0, 0)),
                  pl.BlockSpec((None, RET_C, 1), lambda b, h, c: (h, 0, 0)),
                  pl.BlockSpec((None, RET_C, 1), lambda b, h, c: (h, 0, 0))],
        out_specs=pl.BlockSpec((RET_C, RET_DV), row),
        out_shape=jax.ShapeDtypeStruct((t, RET_HEADS * RET_DV), BF16),
        scratch_shapes=[pltpu.VMEM((RET_DK, RET_DV), F32)],
        compiler_params=_params(("parallel", "parallel", "arbitrary")),
        name="retention",
    )(qk, qk, v, gate, dec, qd, kd)


def _diff_kernel(q_ref, k_ref, v_ref, lq1_ref, lk1_ref, lq2_ref, lk2_ref, go_ref, o_ref,
                 *, lam_init, seq):
    tq = DIFF_TQ
    lane = lax.broadcasted_iota(jnp.int32, (1, 2 * DIFF_D), 1)
    first_map = lane < DIFF_D
    r = lax.broadcasted_iota(jnp.int32, (tq, tq), 0)
    cidx = lax.broadcasted_iota(jnp.int32, (tq, tq), 1)
    keep = cidx <= r
    keep = jnp.concatenate([keep, keep], axis=0)
    lam = (jnp.exp(jnp.sum(lq1_ref[...] * lk1_ref[...], axis=-1, keepdims=True))
           - jnp.exp(jnp.sum(lq2_ref[...] * lk2_ref[...], axis=-1, keepdims=True))
           + lam_init)
    go = go_ref[...]
    for i in range(seq // tq):
        lo = i * tq
        q = q_ref[lo:lo + tq, :]
        zero = jnp.zeros_like(q)
        qs = jnp.concatenate([jnp.where(first_map, q, zero),
                              jnp.where(first_map, zero, q)], axis=0)
        s_d = jnp.where(keep, _dot_nt(qs, k_ref[lo:lo + tq, :]), NEG)
        m = jnp.max(s_d, axis=-1, keepdims=True)
        if i > 0:
            s_off = _dot_nt(qs, k_ref[0:lo, :])
            m = jnp.maximum(m, jnp.max(s_off, axis=-1, keepdims=True))
        p_d = jnp.exp(s_d - m)
        l = jnp.sum(p_d, axis=-1, keepdims=True)
        if i > 0:
            p_off = jnp.exp(s_off - m)
            l = l + jnp.sum(p_off, axis=-1, keepdims=True)
        c = 1.0 / l
        c1 = c[:tq]
        c2 = lam * c[tq:]
        a_d = (p_d[:tq] * c1 - p_d[tq:] * c2).astype(BF16)
        o = _dot(a_d, v_ref[lo:lo + tq, :])
        if i > 0:
            a_off = (p_off[:tq] * c1 - p_off[tq:] * c2).astype(BF16)
            o = o + _dot(a_off, v_ref[0:lo, :])
        o_ref[lo:lo + tq, :] = (_rms(o) * go).astype(BF16)


def _diff_attention(qk, v, lq1, lk1, lq2, lk2, g_out, batch, seq, lam_init):
    t = batch * seq
    vec64 = pl.BlockSpec((1, DIFF_D), lambda b, h: (0, 0))
    head = pl.BlockSpec((seq, 2 * DIFF_D), lambda b, h: (b, h))
    return pl.pallas_call(
        functools.partial(_diff_kernel, lam_init=lam_init, seq=seq),
        grid=(batch, DIFF_HEADS),
        in_specs=[head,
                  pl.BlockSpec((seq, 2 * DIFF_D), lambda b, h: (b, DIFF_HEADS + h)),
                  head,
                  vec64, vec64, vec64, vec64,
                  pl.BlockSpec((1, 2 * DIFF_D), lambda b, h: (0, 0))],
        out_specs=head,
        out_shape=jax.ShapeDtypeStruct((t, DIFF_HEADS * 2 * DIFF_D), BF16),
        compiler_params=_params(("parallel", "parallel")),
        name="diff_attention",
    )(qk, qk, v, lq1, lk1, lq2, lk2, g_out)


def _memkv_kernel(mem_ref, g_ref, w_ref, gk_ref, k_ref, v_ref):
    mn = (_rms(mem_ref[...]) * g_ref[...]).astype(BF16)
    kv = _dot(mn, w_ref[...])
    width = MEM_HEADS * MEM_D
    k_ref[...] = _group_norm_128(kv[:, :width], gk_ref[...], MEM_D).astype(BF16)
    v_ref[...] = kv[:, width:].astype(BF16)


def _memkv(mem2d, g_mem, w_kv, gk, batch, mlen):
    width = MEM_HEADS * MEM_D
    row = pl.BlockSpec((mlen, width), lambda b: (b, 0))
    vec = pl.BlockSpec((1, width), lambda b: (0, 0))
    return pl.pallas_call(
        _memkv_kernel,
        grid=(batch,),
        in_specs=[pl.BlockSpec((mlen, D_MODEL), lambda b: (b, 0)), vec,
                  _resident(w_kv.shape), vec],
        out_specs=[row, row],
        out_shape=[jax.ShapeDtypeStruct((batch * mlen, width), BF16)] * 2,
        compiler_params=_params(("parallel",)),
        name="memkv",
    )(mem2d, g_mem, w_kv, gk)


def _mem_kernel(q_ref, k_ref, v_ref, o_ref):
    outs = []
    for hd in range(MEM_HEADS):
        sl = slice(hd * MEM_D, (hd + 1) * MEM_D)
        s = _dot_nt(q_ref[:, sl], k_ref[:, sl])
        p = jnp.exp(s - jnp.max(s, axis=-1, keepdims=True))
        l = jnp.sum(p, axis=-1, keepdims=True)
        outs.append(_dot(p.astype(BF16), v_ref[:, sl]) / l)
    o_ref[...] = jnp.concatenate(outs, axis=1).astype(BF16)


def _mem_attention(q, k, v, batch, seq, mlen):
    nq = seq // MEM_TQ
    width = MEM_HEADS * MEM_D
    qrow = pl.BlockSpec((MEM_TQ, width), lambda b, i: (b * nq + i, 0))
    kv = pl.BlockSpec((mlen, width), lambda b, i: (b, 0))
    return pl.pallas_call(
        _mem_kernel,
        grid=(batch, nq),
        in_specs=[qrow, kv, kv],
        out_specs=qrow,
        out_shape=jax.ShapeDtypeStruct((batch * seq, width), BF16),
        compiler_params=_params(("parallel", "parallel")),
        name="mem_attention",
    )(q, k, v)


def _merge_kernel(x_ref, ret_ref, dif_ref, mo_ref, gt_ref, wr_ref, wd_ref, wm_ref, wo_ref, o_ref):
    gt = gt_ref[...].astype(F32)
    y = (gt[:, :D_MODEL] * _dot(ret_ref[...], wr_ref[...])
         + gt[:, D_MODEL:2 * D_MODEL] * _dot(dif_ref[...], wd_ref[...])
         + gt[:, 2 * D_MODEL:] * _dot(mo_ref[...], wm_ref[...]))
    o_ref[...] = x_ref[...] + _dot(y.astype(BF16), wo_ref[...])


def _merge(x, ret, dif, mo, gates, wr, wd, wm, wo):
    t = x.shape[0]
    row = lambda w: pl.BlockSpec((TM, w), lambda i: (i, 0))
    return pl.pallas_call(
        _merge_kernel,
        grid=(t // TM,),
        in_specs=[row(D_MODEL), row(ret.shape[1]), row(D_MODEL), row(D_MODEL), row(3 * D_MODEL),
                  _resident(wr.shape), _resident(wd.shape), _resident(wm.shape),
                  _resident(wo.shape)],
        out_specs=row(D_MODEL),
        out_shape=jax.ShapeDtypeStruct((t, D_MODEL), F32),
        compiler_params=_params(("parallel",)),
        name="merge",
    )(x, ret, dif, mo, gates, wr, wd, wm, wo)


def _retention_constants():
    h = np.arange(RET_HEADS, dtype=np.float64)
    log_g = np.log1p(-(2.0 ** (-5.0 - h)))
    idx = np.arange(RET_C, dtype=np.float64)
    dist = idx[:, None] - idx[None, :]
    dec = np.where(dist >= 0, np.exp(log_g[:, None, None] * np.maximum(dist, 0.0)), 0.0)
    qd = np.exp(log_g[:, None] * (idx + 1.0))[:, :, None]
    kd = np.exp(log_g[:, None] * (RET_C - 1.0 - idx))[:, :, None]
    return (jnp.asarray(dec, F32), jnp.asarray(qd, F32), jnp.asarray(kd, F32))


def kernel(x, mem, positions, g_ffn1, w_ffn1_in, w_ffn1_out, g_mix, w_in, g_diff_q, g_diff_k,
           lam_q1, lam_k1, lam_q2, lam_k2, g_diff_out, g_mem_q, g_mem_k, g_mem, w_mem_kv,
           w_br_ret, w_br_diff, w_br_mem, w_o, g_ffn2, w_ffn2_in, w_ffn2_out):
    batch, seq, _ = x.shape
    mlen = mem.shape[1]
    t = batch * seq
    depth = g_ffn1.shape[0]
    half = RET_DK // 2
    inv = jnp.asarray(ROPE_BASE ** (-np.arange(half, dtype=np.float64) / half), F32)[None, :]
    dec, qd, kd = _retention_constants()
    pos = positions.reshape(t, 1)
    mem2d = mem.reshape(batch * mlen, D_MODEL)
    xf = x.reshape(t, D_MODEL)
    bf = lambda w: w.astype(BF16)
    vec = lambda g: g.astype(F32)[None, :]

    for l in range(depth):
        lam_init = 0.8 - 0.6 * math.exp(-0.3 * l)
        x1, h = _ffn(xf, vec(g_ffn1[l]), bf(w_ffn1_in[l]), bf(w_ffn1_out[l]), vec(g_mix[l]))

        w = bf(w_in[l])
        rqk = _proj(h, w, 0, 2048, 1, "rot", (pos, inv))
        rv = _proj(h, w, 2048, 2048, 1, "plain")
        rg = _proj(h, w, 4096, 2048, 1, "silu")
        qk_gain = jnp.concatenate([jnp.tile(g_diff_q[l].astype(F32), 2 * DIFF_HEADS) * (DIFF_D ** -0.5),
                                   jnp.tile(g_diff_k[l].astype(F32), 2 * DIFF_HEADS)])[None, :]
        dqk = _proj(h, w, 6144, 2048, 1, "norm64", (qk_gain,))
        dv = _proj(h, w, 8192, 1024, 1, "plain")
        mq_gain = (jnp.tile(g_mem_q[l].astype(F32), MEM_HEADS) * (MEM_D ** -0.5))[None, :]
        mq = _proj(h, w, 9216, 1024, 1, "norm256", (mq_gain,))
        gates = _proj(h, w, 10240, 1024, 3, "sigmoid")

        ret = _retention(rqk, rv, rg, dec, qd, kd, batch, seq)
        go = (g_diff_out[l].astype(F32) * (1.0 - lam_init))[None, :]
        dif = _diff_attention(dqk, dv, vec(lam_q1[l]), vec(lam_k1[l]), vec(lam_q2[l]),
                              vec(lam_k2[l]), go, batch, seq, lam_init)
        mk, mv = _memkv(mem2d, vec(g_mem[l]), bf(w_mem_kv[l]),
                        jnp.tile(g_mem_k[l].astype(F32), MEM_HEADS)[None, :], batch, mlen)
        mo = _mem_attention(mq, mk, mv, batch, seq, mlen)

        x2 = _merge(x1, ret, dif, mo, gates, bf(w_br_ret[l]), bf(w_br_diff[l]),
                    bf(w_br_mem[l]), bf(w_o[l]))
        (xf,) = _ffn(x2, vec(g_ffn2[l]), bf(w_ffn2_in[l]), bf(w_ffn2_out[l]))
    return xf.reshape(batch, seq, D_MODEL)
```

```python
import functools
import math

import jax
import jax.numpy as jnp
import numpy as np
from jax import lax
from jax.experimental import pallas as pl
from jax.experimental.pallas import tpu as pltpu

F32 = jnp.float32
BF16 = jnp.bfloat16

D_MODEL = 1024
D_FF = 2816
FFN_RES = 0.5
EPS = 1e-6
NEG = -1e30
ROPE_BASE = 10000.0

RET_HEADS = 4
RET_DK = 256
RET_DV = 512
DIFF_HEADS = 8
DIFF_D = 64
MEM_HEADS = 4
MEM_D = 256

V7X_VMEM_LIMIT_BYTES = 56 * 1024 * 1024

TM = 512
FFN_CHUNK = 256
RET_C = 256
DIFF_TQ = 256
MEM_TQ = 512


def _params(sem):
    return pltpu.CompilerParams(dimension_semantics=sem,
                                vmem_limit_bytes=V7X_VMEM_LIMIT_BYTES)


def _resident(shape):
    nd = len(shape)
    return pl.BlockSpec(shape, lambda *_: (0,) * nd, pipeline_mode=pl.Buffered(1))


def _rms(x):
    return x * lax.rsqrt(jnp.mean(x * x, axis=-1, keepdims=True) + EPS)


def _dot(a, b):
    return jnp.dot(a, b, preferred_element_type=F32)


def _dot_nt(a, b):
    return lax.dot_general(a, b, (((1,), (1,)), ((), ())), preferred_element_type=F32)


def _dot_tn(a, b):
    return lax.dot_general(a, b, (((0,), (0,)), ((), ())), preferred_element_type=F32)


def _ffn_kernel(*refs, emit_next):
    if emit_next:
        x_ref, g_ref, win_ref, wout_ref, gn_ref, o_ref, h_ref = refs
    else:
        x_ref, g_ref, win_ref, wout_ref, o_ref = refs
    x = x_ref[...]
    hb = (_rms(x) * g_ref[...]).astype(BF16)
    acc = jnp.zeros(x.shape, F32)
    for j in range(D_FF // FFN_CHUNK):
        lo = j * FFN_CHUNK
        a = _dot(hb, win_ref[:, lo:lo + FFN_CHUNK])
        b = _dot(hb, win_ref[:, D_FF + lo:D_FF + lo + FFN_CHUNK])
        act = (a * jax.nn.sigmoid(a) * b).astype(BF16)
        acc = acc + _dot(act, wout_ref[lo:lo + FFN_CHUNK, :])
    y = x + FFN_RES * acc
    o_ref[...] = y
    if emit_next:
        h_ref[...] = (_rms(y) * gn_ref[...]).astype(BF16)


def _ffn(x, g, w_in, w_out, g_next=None):
    t = x.shape[0]
    emit = g_next is not None
    row = pl.BlockSpec((TM, D_MODEL), lambda i: (i, 0))
    vec = pl.BlockSpec((1, D_MODEL), lambda i: (0, 0))
    in_specs = [row, vec, _resident(w_in.shape), _resident(w_out.shape)]
    args = [x, g, w_in, w_out]
    out_shape = [jax.ShapeDtypeStruct((t, D_MODEL), F32)]
    out_specs = [row]
    if emit:
        in_specs.append(vec)
        args.append(g_next)
        out_shape.append(jax.ShapeDtypeStruct((t, D_MODEL), BF16))
        out_specs.append(row)
    return pl.pallas_call(
        functools.partial(_ffn_kernel, emit_next=emit),
        grid=(t // TM,),
        in_specs=in_specs,
        out_specs=out_specs,
        out_shape=out_shape,
        compiler_params=_params(("parallel",)),
        name="ffn_next" if emit else "ffn",
    )(*args)


def _group_norm_128(y, gain, group):
    cols = y.shape[1]
    outs = []
    if group == 64:
        lane = lax.broadcasted_iota(jnp.int32, (1, 128), 1)
        first = lane < 64
        for c in range(cols // 128):
            blk = y[:, c * 128:(c + 1) * 128]
            sq = blk * blk
            s_lo = jnp.sum(jnp.where(first, sq, 0.0), axis=-1, keepdims=True)
            s_hi = jnp.sum(jnp.where(first, 0.0, sq), axis=-1, keepdims=True)
            r = jnp.where(first, lax.rsqrt(s_lo * (1.0 / 64) + EPS),
                          lax.rsqrt(s_hi * (1.0 / 64) + EPS))
            outs.append(blk * r)
    else:
        for c in range(cols // group):
            blk = y[:, c * group:(c + 1) * group]
            outs.append(_rms(blk))
    return jnp.concatenate(outs, axis=1) * gain


RET_QK_W = RET_HEADS * RET_DK
RET_V_W = RET_HEADS * RET_DV
PROJ_RET_W = 2 * RET_QK_W + 2 * RET_V_W
DIFF_W = DIFF_HEADS * 2 * DIFF_D
MEM_W = MEM_HEADS * MEM_D
PROJ_MIX_W = 2 * DIFF_W + DIFF_W + MEM_W + 3 * D_MODEL
PROJ_CHUNK = 512


def _proj_ret_kernel(h_ref, w_ref, pos_ref, inv_ref, qk_ref, v_ref, g_ref):
    h = h_ref[...]
    ang = pos_ref[...].astype(F32) * inv_ref[...]
    cos = jnp.cos(ang)
    sin = jnp.sin(ang)
    half = RET_DK // 2
    for hd in range(2 * RET_HEADS):
        lo = hd * RET_DK
        y = _dot(h, w_ref[:, lo:lo + RET_DK])
        x1 = y[:, :half]
        x2 = y[:, half:]
        r1 = x1 * cos - x2 * sin
        r2 = x1 * sin + x2 * cos
        if hd >= RET_HEADS:
            r1 = r1 * (RET_DK ** -0.5)
            r2 = r2 * (RET_DK ** -0.5)
        qk_ref[:, lo:lo + half] = r1.astype(BF16)
        qk_ref[:, lo + half:lo + RET_DK] = r2.astype(BF16)
    for c in range(RET_V_W // PROJ_CHUNK):
        lo = c * PROJ_CHUNK
        v_ref[:, lo:lo + PROJ_CHUNK] = _dot(
            h, w_ref[:, 2 * RET_QK_W + lo:2 * RET_QK_W + lo + PROJ_CHUNK]).astype(BF16)
    for c in range(RET_V_W // PROJ_CHUNK):
        lo = c * PROJ_CHUNK
        wlo = 2 * RET_QK_W + RET_V_W + lo
        y = _dot(h, w_ref[:, wlo:wlo + PROJ_CHUNK])
        g_ref[:, lo:lo + PROJ_CHUNK] = (y * jax.nn.sigmoid(y)).astype(BF16)


def _proj_ret(h, w, pos, inv):
    t = h.shape[0]
    row = lambda width: pl.BlockSpec((TM, width), lambda i: (i, 0))
    return pl.pallas_call(
        _proj_ret_kernel,
        grid=(t // TM,),
        in_specs=[row(D_MODEL), _resident(w.shape), row(1),
                  pl.BlockSpec((1, 128), lambda i: (0, 0))],
        out_specs=[row(2 * RET_QK_W), row(RET_V_W), row(RET_V_W)],
        out_shape=[jax.ShapeDtypeStruct((t, 2 * RET_QK_W), BF16),
                   jax.ShapeDtypeStruct((t, RET_V_W), BF16),
                   jax.ShapeDtypeStruct((t, RET_V_W), BF16)],
        compiler_params=_params(("parallel",)),
        name="proj_ret",
    )(h, w, pos, inv)


def _proj_mix_kernel(h_ref, w_ref, gqk_ref, gmq_ref, qk_ref, v_ref, mq_ref, gt_ref):
    h = h_ref[...]
    for c in range(2 * DIFF_W // 256):
        lo = c * 256
        y = _dot(h, w_ref[:, lo:lo + 256])
        qk_ref[:, lo:lo + 256] = _group_norm_128(y, gqk_ref[:, lo:lo + 256], DIFF_D).astype(BF16)
    base = 2 * DIFF_W
    for c in range(DIFF_W // PROJ_CHUNK):
        lo = c * PROJ_CHUNK
        v_ref[:, lo:lo + PROJ_CHUNK] = _dot(
            h, w_ref[:, base + lo:base + lo + PROJ_CHUNK]).astype(BF16)
    base += DIFF_W
    for c in range(MEM_HEADS):
        lo = c * MEM_D
        y = _dot(h, w_ref[:, base + lo:base + lo + MEM_D])
        mq_ref[:, lo:lo + MEM_D] = (_rms(y) * gmq_ref[:, lo:lo + MEM_D]).astype(BF16)
    base += MEM_W
    for c in range(3 * D_MODEL // PROJ_CHUNK):
        lo = c * PROJ_CHUNK
        y = _dot(h, w_ref[:, base + lo:base + lo + PROJ_CHUNK])
        gt_ref[:, lo:lo + PROJ_CHUNK] = jax.nn.sigmoid(y).astype(BF16)


def _proj_mix(h, w, gqk, gmq):
    t = h.shape[0]
    row = lambda width: pl.BlockSpec((TM, width), lambda i: (i, 0))
    vec = lambda width: pl.BlockSpec((1, width), lambda i: (0, 0))
    return pl.pallas_call(
        _proj_mix_kernel,
        grid=(t // TM,),
        in_specs=[row(D_MODEL), _resident(w.shape), vec(2 * DIFF_W), vec(MEM_W)],
        out_specs=[row(2 * DIFF_W), row(DIFF_W), row(MEM_W), row(3 * D_MODEL)],
        out_shape=[jax.ShapeDtypeStruct((t, 2 * DIFF_W), BF16),
                   jax.ShapeDtypeStruct((t, DIFF_W), BF16),
                   jax.ShapeDtypeStruct((t, MEM_W), BF16),
                   jax.ShapeDtypeStruct((t, 3 * D_MODEL), BF16)],
        compiler_params=_params(("parallel",)),
        name="proj_mix",
    )(h, w, gqk, gmq)


def _ret_kernel(q_ref, k_ref, v_ref, g_ref, dec_ref, qd_ref, kd_ref, o_ref, *, seq):
    dec = dec_ref[...]
    qd = qd_ref[...]
    kd = kd_ref[...]
    chunk_decay = qd[RET_C - 1:RET_C, :]
    nc = seq // RET_C
    st = None
    for c in range(nc):
        lo = c * RET_C
        q = q_ref[lo:lo + RET_C, :]
        k = k_ref[lo:lo + RET_C, :]
        v = v_ref[lo:lo + RET_C, :]
        s = _dot_nt(q, k) * dec
        out = _dot(s.astype(BF16), v)
        if st is not None:
            out = out + _dot(q, st.astype(BF16)) * qd
        if c + 1 < nc:
            upd = _dot_tn((k.astype(F32) * kd).astype(BF16), v)
            st = upd if st is None else chunk_decay * st + upd
        o_ref[lo:lo + RET_C, :] = (_rms(out) * g_ref[lo:lo + RET_C, :].astype(F32)).astype(BF16)


def _retention(qk, v, gate, dec, qd, kd, batch, seq):
    t = batch * seq
    head = lambda b, h: (b, h)
    const = lambda b, h: (h, 0, 0)
    return pl.pallas_call(
        functools.partial(_ret_kernel, seq=seq),
        grid=(batch, RET_HEADS),
        in_specs=[pl.BlockSpec((seq, RET_DK), head),
                  pl.BlockSpec((seq, RET_DK), lambda b, h: (b, RET_HEADS + h)),
                  pl.BlockSpec((seq, RET_DV), head),
                  pl.BlockSpec((seq, RET_DV), head),
                  pl.BlockSpec((None, RET_C, RET_C), const),
                  pl.BlockSpec((None, RET_C, 1), const),
                  pl.BlockSpec((None, RET_C, 1), const)],
        out_specs=pl.BlockSpec((seq, RET_DV), head),
        out_shape=jax.ShapeDtypeStruct((t, RET_V_W), BF16),
        compiler_params=_params(("parallel", "parallel")),
        name="retention",
    )(qk, qk, v, gate, dec, qd, kd)


def _diff_kernel(q_ref, k_ref, v_ref, lq1_ref, lk1_ref, lq2_ref, lk2_ref, go_ref, o_ref,
                 *, lam_init, seq):
    tq = DIFF_TQ
    lane = lax.broadcasted_iota(jnp.int32, (1, 2 * DIFF_D), 1)
    first_map = lane < DIFF_D
    r = lax.broadcasted_iota(jnp.int32, (tq, tq), 0)
    cidx = lax.broadcasted_iota(jnp.int32, (tq, tq), 1)
    keep = cidx <= r
    keep = jnp.concatenate([keep, keep], axis=0)
    lam = (jnp.exp(jnp.sum(lq1_ref[...] * lk1_ref[...], axis=-1, keepdims=True))
           - jnp.exp(jnp.sum(lq2_ref[...] * lk2_ref[...], axis=-1, keepdims=True))
           + lam_init)
    go = go_ref[...]
    for i in range(seq // tq):
        lo = i * tq
        q = q_ref[lo:lo + tq, :]
        zero = jnp.zeros_like(q)
        qs = jnp.concatenate([jnp.where(first_map, q, zero),
                              jnp.where(first_map, zero, q)], axis=0)
        s_d = jnp.where(keep, _dot_nt(qs, k_ref[lo:lo + tq, :]), NEG)
        m = jnp.max(s_d, axis=-1, keepdims=True)
        if i > 0:
            s_off = _dot_nt(qs, k_ref[0:lo, :])
            m = jnp.maximum(m, jnp.max(s_off, axis=-1, keepdims=True))
        p_d = jnp.exp(s_d - m)
        l = jnp.sum(p_d, axis=-1, keepdims=True)
        if i > 0:
            p_off = jnp.exp(s_off - m)
            l = l + jnp.sum(p_off, axis=-1, keepdims=True)
        c = 1.0 / l
        c1 = c[:tq]
        c2 = lam * c[tq:]
        a_d = (p_d[:tq] * c1 - p_d[tq:] * c2).astype(BF16)
        o = _dot(a_d, v_ref[lo:lo + tq, :])
        if i > 0:
            a_off = (p_off[:tq] * c1 - p_off[tq:] * c2).astype(BF16)
            o = o + _dot(a_off, v_ref[0:lo, :])
        o_ref[lo:lo + tq, :] = (_rms(o) * go).astype(BF16)


def _diff_attention(qk, v, lq1, lk1, lq2, lk2, g_out, batch, seq, lam_init):
    t = batch * seq
    vec64 = pl.BlockSpec((1, DIFF_D), lambda b, h: (0, 0))
    head = pl.BlockSpec((seq, 2 * DIFF_D), lambda b, h: (b, h))
    return pl.pallas_call(
        functools.partial(_diff_kernel, lam_init=lam_init, seq=seq),
        grid=(batch, DIFF_HEADS),
        in_specs=[head,
                  pl.BlockSpec((seq, 2 * DIFF_D), lambda b, h: (b, DIFF_HEADS + h)),
                  head,
                  vec64, vec64, vec64, vec64,
                  pl.BlockSpec((1, 2 * DIFF_D), lambda b, h: (0, 0))],
        out_specs=head,
        out_shape=jax.ShapeDtypeStruct((t, DIFF_HEADS * 2 * DIFF_D), BF16),
        compiler_params=_params(("parallel", "parallel")),
        name="diff_attention",
    )(qk, qk, v, lq1, lk1, lq2, lk2, g_out)


def _memkv_kernel(mem_ref, g_ref, w_ref, gk_ref, k_ref, v_ref):
    mn = (_rms(mem_ref[...]) * g_ref[...]).astype(BF16)
    kv = _dot(mn, w_ref[...])
    width = MEM_HEADS * MEM_D
    k_ref[...] = _group_norm_128(kv[:, :width], gk_ref[...], MEM_D).astype(BF16)
    v_ref[...] = kv[:, width:].astype(BF16)


def _memkv(mem2d, g_mem, w_kv, gk, batch, mlen):
    width = MEM_HEADS * MEM_D
    row = pl.BlockSpec((mlen, width), lambda b: (b, 0))
    vec = pl.BlockSpec((1, width), lambda b: (0, 0))
    return pl.pallas_call(
        _memkv_kernel,
        grid=(batch,),
        in_specs=[pl.BlockSpec((mlen, D_MODEL), lambda b: (b, 0)), vec,
                  _resident(w_kv.shape), vec],
        out_specs=[row, row],
        out_shape=[jax.ShapeDtypeStruct((batch * mlen, width), BF16)] * 2,
        compiler_params=_params(("parallel",)),
        name="memkv",
    )(mem2d, g_mem, w_kv, gk)


def _mem_kernel(q_ref, k_ref, v_ref, o_ref):
    outs = []
    for hd in range(MEM_HEADS):
        sl = slice(hd * MEM_D, (hd + 1) * MEM_D)
        s = _dot_nt(q_ref[:, sl], k_ref[:, sl])
        p = jnp.exp(s - jnp.max(s, axis=-1, keepdims=True))
        l = jnp.sum(p, axis=-1, keepdims=True)
        outs.append(_dot(p.astype(BF16), v_ref[:, sl]) / l)
    o_ref[...] = jnp.concatenate(outs, axis=1).astype(BF16)


def _mem_attention(q, k, v, batch, seq, mlen):
    nq = seq // MEM_TQ
    width = MEM_HEADS * MEM_D
    qrow = pl.BlockSpec((MEM_TQ, width), lambda b, i: (b * nq + i, 0))
    kv = pl.BlockSpec((mlen, width), lambda b, i: (b, 0))
    return pl.pallas_call(
        _mem_kernel,
        grid=(batch, nq),
        in_specs=[qrow, kv, kv],
        out_specs=qrow,
        out_shape=jax.ShapeDtypeStruct((batch * seq, width), BF16),
        compiler_params=_params(("parallel", "parallel")),
        name="mem_attention",
    )(q, k, v)


def _merge_kernel(x_ref, ret_ref, dif_ref, mo_ref, gt_ref, wr_ref, wd_ref, wm_ref, wo_ref, o_ref):
    gt = gt_ref[...].astype(F32)
    y = (gt[:, :D_MODEL] * _dot(ret_ref[...], wr_ref[...])
         + gt[:, D_MODEL:2 * D_MODEL] * _dot(dif_ref[...], wd_ref[...])
         + gt[:, 2 * D_MODEL:] * _dot(mo_ref[...], wm_ref[...]))
    o_ref[...] = x_ref[...] + _dot(y.astype(BF16), wo_ref[...])


def _merge(x, ret, dif, mo, gates, wr, wd, wm, wo):
    t = x.shape[0]
    row = lambda w: pl.BlockSpec((TM, w), lambda i: (i, 0))
    return pl.pallas_call(
        _merge_kernel,
        grid=(t // TM,),
        in_specs=[row(D_MODEL), row(ret.shape[1]), row(D_MODEL), row(D_MODEL), row(3 * D_MODEL),
                  _resident(wr.shape), _resident(wd.shape), _resident(wm.shape),
                  _resident(wo.shape)],
        out_specs=row(D_MODEL),
        out_shape=jax.ShapeDtypeStruct((t, D_MODEL), F32),
        compiler_params=_params(("parallel",)),
        name="merge",
    )(x, ret, dif, mo, gates, wr, wd, wm, wo)


def _retention_constants():
    h = np.arange(RET_HEADS, dtype=np.float64)
    log_g = np.log1p(-(2.0 ** (-5.0 - h)))
    idx = np.arange(RET_C, dtype=np.float64)
    dist = idx[:, None] - idx[None, :]
    dec = np.where(dist >= 0, np.exp(log_g[:, None, None] * np.maximum(dist, 0.0)), 0.0)
    qd = np.exp(log_g[:, None] * (idx + 1.0))[:, :, None]
    kd = np.exp(log_g[:, None] * (RET_C - 1.0 - idx))[:, :, None]
    return (jnp.asarray(dec, F32), jnp.asarray(qd, F32), jnp.asarray(kd, F32))


def kernel(x, mem, positions, g_ffn1, w_ffn1_in, w_ffn1_out, g_mix, w_in, g_diff_q, g_diff_k,
           lam_q1, lam_k1, lam_q2, lam_k2, g_diff_out, g_mem_q, g_mem_k, g_mem, w_mem_kv,
           w_br_ret, w_br_diff, w_br_mem, w_o, g_ffn2, w_ffn2_in, w_ffn2_out):
    batch, seq, _ = x.shape
    mlen = mem.shape[1]
    t = batch * seq
    depth = g_ffn1.shape[0]
    half = RET_DK // 2
    inv = jnp.asarray(ROPE_BASE ** (-np.arange(half, dtype=np.float64) / half), F32)[None, :]
    dec, qd, kd = _retention_constants()
    pos = positions.reshape(t, 1)
    mem2d = mem.reshape(batch * mlen, D_MODEL)
    xf = x.reshape(t, D_MODEL)
    bf = lambda w: w.astype(BF16)
    vec = lambda g: g.astype(F32)[None, :]

    for l in range(depth):
        lam_init = 0.8 - 0.6 * math.exp(-0.3 * l)
        x1, h = _ffn(xf, vec(g_ffn1[l]), bf(w_ffn1_in[l]), bf(w_ffn1_out[l]), vec(g_mix[l]))

        rqk, rv, rg = _proj_ret(h, bf(w_in[l][:, :PROJ_RET_W]), pos, inv)
        qk_gain = jnp.concatenate([jnp.tile(g_diff_q[l].astype(F32), 2 * DIFF_HEADS) * (DIFF_D ** -0.5),
                                   jnp.tile(g_diff_k[l].astype(F32), 2 * DIFF_HEADS)])[None, :]
        mq_gain = (jnp.tile(g_mem_q[l].astype(F32), MEM_HEADS) * (MEM_D ** -0.5))[None, :]
        dqk, dv, mq, gates = _proj_mix(h, bf(w_in[l][:, PROJ_RET_W:]), qk_gain, mq_gain)

        ret = _retention(rqk, rv, rg, dec, qd, kd, batch, seq)
        go = (g_diff_out[l].astype(F32) * (1.0 - lam_init))[None, :]
        dif = _diff_attention(dqk, dv, vec(lam_q1[l]), vec(lam_k1[l]), vec(lam_q2[l]),
                              vec(lam_k2[l]), go, batch, seq, lam_init)
        mk, mv = _memkv(mem2d, vec(g_mem[l]), bf(w_mem_kv[l]),
                        jnp.tile(g_mem_k[l].astype(F32), MEM_HEADS)[None, :], batch, mlen)
        mo = _mem_attention(mq, mk, mv, batch, seq, mlen)

        x2 = _merge(x1, ret, dif, mo, gates, bf(w_br_ret[l]), bf(w_br_diff[l]),
                    bf(w_br_mem[l]), bf(w_o[l]))
        (xf,) = _ffn(x2, vec(g_ffn2[l]), bf(w_ffn2_in[l]), bf(w_ffn2_out[l]))
    return xf.reshape(batch, seq, D_MODEL)
```

```python
import functools
import math

import jax
import jax.numpy as jnp
import numpy as np
from jax import lax
from jax.experimental import pallas as pl
from jax.experimental.pallas import tpu as pltpu

F32 = jnp.float32
BF16 = jnp.bfloat16

D_MODEL = 1024
D_FF = 2816
FFN_RES = 0.5
EPS = 1e-6
NEG = -1e30
LOG2E = math.log2(math.e)
ROPE_BASE = 10000.0

RET_HEADS = 4
RET_DK = 256
RET_DV = 512
DIFF_HEADS = 8
DIFF_D = 64
MEM_HEADS = 4
MEM_D = 256

V7X_VMEM_LIMIT_BYTES = 56 * 1024 * 1024

TM = 512
FFN_CHUNK = 256
RET_C = 256
DIFF_TQ = 256
MEM_TQ = 512


def _params(sem):
    return pltpu.CompilerParams(dimension_semantics=sem,
                                vmem_limit_bytes=V7X_VMEM_LIMIT_BYTES)


def _resident(shape):
    nd = len(shape)
    return pl.BlockSpec(shape, lambda *_: (0,) * nd, pipeline_mode=pl.Buffered(1))


def _rms(x):
    return x * lax.rsqrt(jnp.mean(x * x, axis=-1, keepdims=True) + EPS)


def _dot(a, b):
    return jnp.dot(a, b, preferred_element_type=F32)


def _dot_nt(a, b):
    return lax.dot_general(a, b, (((1,), (1,)), ((), ())), preferred_element_type=F32)


def _dot_tn(a, b):
    return lax.dot_general(a, b, (((0,), (0,)), ((), ())), preferred_element_type=F32)


def _ffn_kernel(*refs, emit_next):
    if emit_next:
        x_ref, g_ref, win_ref, wout_ref, gn_ref, o_ref, h_ref = refs
    else:
        x_ref, g_ref, win_ref, wout_ref, o_ref = refs
    x = x_ref[...]
    hb = (_rms(x) * g_ref[...]).astype(BF16)
    acc = jnp.zeros(x.shape, F32)
    for j in range(D_FF // FFN_CHUNK):
        lo = j * FFN_CHUNK
        a = _dot(hb, win_ref[:, lo:lo + FFN_CHUNK])
        b = _dot(hb, win_ref[:, D_FF + lo:D_FF + lo + FFN_CHUNK])
        act = (a * jax.nn.sigmoid(a) * b).astype(BF16)
        acc = acc + _dot(act, wout_ref[lo:lo + FFN_CHUNK, :])
    y = x + FFN_RES * acc
    o_ref[...] = y
    if emit_next:
        h_ref[...] = (_rms(y) * gn_ref[...]).astype(BF16)


def _ffn(x, g, w_in, w_out, g_next=None):
    t = x.shape[0]
    emit = g_next is not None
    row = pl.BlockSpec((TM, D_MODEL), lambda i: (i, 0))
    vec = pl.BlockSpec((1, D_MODEL), lambda i: (0, 0))
    in_specs = [row, vec, _resident(w_in.shape), _resident(w_out.shape)]
    args = [x, g, w_in, w_out]
    out_shape = [jax.ShapeDtypeStruct((t, D_MODEL), F32)]
    out_specs = [row]
    if emit:
        in_specs.append(vec)
        args.append(g_next)
        out_shape.append(jax.ShapeDtypeStruct((t, D_MODEL), BF16))
        out_specs.append(row)
    return pl.pallas_call(
        functools.partial(_ffn_kernel, emit_next=emit),
        grid=(t // TM,),
        in_specs=in_specs,
        out_specs=out_specs,
        out_shape=out_shape,
        compiler_params=_params(("parallel",)),
        name="ffn_next" if emit else "ffn",
    )(*args)


def _group_norm_128(y, gain, group):
    cols = y.shape[1]
    outs = []
    if group == 64:
        lane = lax.broadcasted_iota(jnp.int32, (1, 128), 1)
        first = lane < 64
        for c in range(cols // 128):
            blk = y[:, c * 128:(c + 1) * 128]
            sq = blk * blk
            s_lo = jnp.sum(jnp.where(first, sq, 0.0), axis=-1, keepdims=True)
            s_hi = jnp.sum(jnp.where(first, 0.0, sq), axis=-1, keepdims=True)
            r = jnp.where(first, lax.rsqrt(s_lo * (1.0 / 64) + EPS),
                          lax.rsqrt(s_hi * (1.0 / 64) + EPS))
            outs.append(blk * r)
    else:
        for c in range(cols // group):
            blk = y[:, c * group:(c + 1) * group]
            outs.append(_rms(blk))
    return jnp.concatenate(outs, axis=1) * gain


RET_QK_W = RET_HEADS * RET_DK
RET_V_W = RET_HEADS * RET_DV
PROJ_RET_W = 2 * RET_QK_W + 2 * RET_V_W
DIFF_W = DIFF_HEADS * 2 * DIFF_D
MEM_W = MEM_HEADS * MEM_D
PROJ_MIX_W = 2 * DIFF_W + DIFF_W + MEM_W + 3 * D_MODEL
PROJ_CHUNK = 512


def _proj_ret_kernel(h_ref, w_ref, pos_ref, inv_ref, qk_ref, v_ref, g_ref):
    h = h_ref[...]
    ang = pos_ref[...].astype(F32) * inv_ref[...]
    cos = jnp.cos(ang)
    sin = jnp.sin(ang)
    half = RET_DK // 2
    for hd in range(2 * RET_HEADS):
        lo = hd * RET_DK
        y = _dot(h, w_ref[:, lo:lo + RET_DK])
        x1 = y[:, :half]
        x2 = y[:, half:]
        r1 = x1 * cos - x2 * sin
        r2 = x1 * sin + x2 * cos
        if hd >= RET_HEADS:
            r1 = r1 * (RET_DK ** -0.5)
            r2 = r2 * (RET_DK ** -0.5)
        qk_ref[:, lo:lo + half] = r1.astype(BF16)
        qk_ref[:, lo + half:lo + RET_DK] = r2.astype(BF16)
    for c in range(RET_V_W // PROJ_CHUNK):
        lo = c * PROJ_CHUNK
        v_ref[:, lo:lo + PROJ_CHUNK] = _dot(
            h, w_ref[:, 2 * RET_QK_W + lo:2 * RET_QK_W + lo + PROJ_CHUNK]).astype(BF16)
    for c in range(RET_V_W // PROJ_CHUNK):
        lo = c * PROJ_CHUNK
        wlo = 2 * RET_QK_W + RET_V_W + lo
        y = _dot(h, w_ref[:, wlo:wlo + PROJ_CHUNK])
        g_ref[:, lo:lo + PROJ_CHUNK] = (y * jax.nn.sigmoid(y)).astype(BF16)


def _proj_ret(h, w, pos, inv):
    t = h.shape[0]
    row = lambda width: pl.BlockSpec((TM, width), lambda i: (i, 0))
    return pl.pallas_call(
        _proj_ret_kernel,
        grid=(t // TM,),
        in_specs=[row(D_MODEL),
                  pl.BlockSpec((D_MODEL, PROJ_RET_W), lambda i: (0, 0),
                               pipeline_mode=pl.Buffered(1)),
                  row(1),
                  pl.BlockSpec((1, 128), lambda i: (0, 0))],
        out_specs=[row(2 * RET_QK_W), row(RET_V_W), row(RET_V_W)],
        out_shape=[jax.ShapeDtypeStruct((t, 2 * RET_QK_W), BF16),
                   jax.ShapeDtypeStruct((t, RET_V_W), BF16),
                   jax.ShapeDtypeStruct((t, RET_V_W), BF16)],
        compiler_params=_params(("parallel",)),
        name="proj_ret",
    )(h, w, pos, inv)


def _proj_mix_kernel(h_ref, w_ref, gqk_ref, gmq_ref, qk_ref, v_ref, mq_ref, gt_ref):
    h = h_ref[...]
    for c in range(2 * DIFF_W // 256):
        lo = c * 256
        y = _dot(h, w_ref[:, PROJ_RET_W + lo:PROJ_RET_W + lo + 256])
        qk_ref[:, lo:lo + 256] = _group_norm_128(y, gqk_ref[:, lo:lo + 256], DIFF_D).astype(BF16)
    base = PROJ_RET_W + 2 * DIFF_W
    for c in range(DIFF_W // PROJ_CHUNK):
        lo = c * PROJ_CHUNK
        v_ref[:, lo:lo + PROJ_CHUNK] = _dot(
            h, w_ref[:, base + lo:base + lo + PROJ_CHUNK]).astype(BF16)
    base += DIFF_W
    for c in range(MEM_HEADS):
        lo = c * MEM_D
        y = _dot(h, w_ref[:, base + lo:base + lo + MEM_D])
        mq_ref[:, lo:lo + MEM_D] = (_rms(y) * gmq_ref[:, lo:lo + MEM_D]).astype(BF16)
    base += MEM_W
    for c in range(3 * D_MODEL // PROJ_CHUNK):
        lo = c * PROJ_CHUNK
        y = _dot(h, w_ref[:, base + lo:base + lo + PROJ_CHUNK])
        gt_ref[:, lo:lo + PROJ_CHUNK] = jax.nn.sigmoid(y).astype(BF16)


def _proj_mix(h, w, gqk, gmq):
    t = h.shape[0]
    row = lambda width: pl.BlockSpec((TM, width), lambda i: (i, 0))
    vec = lambda width: pl.BlockSpec((1, width), lambda i: (0, 0))
    return pl.pallas_call(
        _proj_mix_kernel,
        grid=(t // TM,),
        in_specs=[row(D_MODEL), _resident(w.shape), vec(2 * DIFF_W), vec(MEM_W)],
        out_specs=[row(2 * DIFF_W), row(DIFF_W), row(MEM_W), row(3 * D_MODEL)],
        out_shape=[jax.ShapeDtypeStruct((t, 2 * DIFF_W), BF16),
                   jax.ShapeDtypeStruct((t, DIFF_W), BF16),
                   jax.ShapeDtypeStruct((t, MEM_W), BF16),
                   jax.ShapeDtypeStruct((t, 3 * D_MODEL), BF16)],
        compiler_params=_params(("parallel",)),
        name="proj_mix",
    )(h, w, gqk, gmq)


def _ret_kernel(q_ref, k_ref, v_ref, g_ref, dec_ref, qd_ref, kd_ref, o_ref, *, seq):
    dec = dec_ref[...]
    qd = qd_ref[...]
    kd = kd_ref[...]
    chunk_decay = qd[RET_C - 1:RET_C, :]
    nc = seq // RET_C
    st = None
    for c in range(nc):
        lo = c * RET_C
        q = q_ref[lo:lo + RET_C, :]
        k = k_ref[lo:lo + RET_C, :]
        v = v_ref[lo:lo + RET_C, :]
        s = _dot_nt(q, k) * dec
        out = _dot(s.astype(BF16), v)
        if st is not None:
            out = out + _dot(q, st.astype(BF16)) * qd
        if c + 1 < nc:
            upd = _dot_tn((k.astype(F32) * kd).astype(BF16), v)
            st = upd if st is None else chunk_decay * st + upd
        o_ref[lo:lo + RET_C, :] = (_rms(out) * g_ref[lo:lo + RET_C, :].astype(F32)).astype(BF16)


def _retention(qk, v, gate, dec, qd, kd, batch, seq):
    t = batch * seq
    head = lambda b, h: (b, h)
    const = lambda b, h: (h, 0, 0)
    return pl.pallas_call(
        functools.partial(_ret_kernel, seq=seq),
        grid=(batch, RET_HEADS),
        in_specs=[pl.BlockSpec((seq, RET_DK), head),
                  pl.BlockSpec((seq, RET_DK), lambda b, h: (b, RET_HEADS + h)),
                  pl.BlockSpec((seq, RET_DV), head),
                  pl.BlockSpec((seq, RET_DV), head),
                  pl.BlockSpec((None, RET_C, RET_C), const),
                  pl.BlockSpec((None, RET_C, 1), const),
                  pl.BlockSpec((None, RET_C, 1), const)],
        out_specs=pl.BlockSpec((seq, RET_DV), head),
        out_shape=jax.ShapeDtypeStruct((t, RET_V_W), BF16),
        compiler_params=_params(("parallel", "parallel")),
        name="retention",
    )(qk, qk, v, gate, dec, qd, kd)


def _diff_kernel(q_ref, k_ref, v_ref, lq1_ref, lk1_ref, lq2_ref, lk2_ref, go_ref, o_ref,
                 *, lam_init, seq):
    tq = DIFF_TQ
    lane = lax.broadcasted_iota(jnp.int32, (1, 2 * DIFF_D), 1)
    first_map = lane < DIFF_D
    r = lax.broadcasted_iota(jnp.int32, (tq, tq), 0)
    cidx = lax.broadcasted_iota(jnp.int32, (tq, tq), 1)
    keep = cidx <= r
    keep = jnp.concatenate([keep, keep], axis=0)
    lam = (jnp.exp(jnp.sum(lq1_ref[...] * lk1_ref[...], axis=-1, keepdims=True))
           - jnp.exp(jnp.sum(lq2_ref[...] * lk2_ref[...], axis=-1, keepdims=True))
           + lam_init)
    go = go_ref[...]
    nq = seq // tq
    v_ext = jnp.concatenate([v_ref[...], jnp.ones((seq, 2 * DIFF_D), BF16)], axis=1)

    def finish(i, o_ext):
        r_ = o_ext[:, :2 * DIFF_D] / o_ext[:, 2 * DIFF_D:]
        d = r_[:tq] - lam * r_[tq:]
        o_ref[i * tq:(i + 1) * tq, :] = (_rms(d) * go).astype(BF16)

    prev = None
    for i in range(nq + 1):
        tiles = []
        run_max = None
        acc = None
        if i < nq:
            q = q_ref[i * tq:(i + 1) * tq, :]
            zero = jnp.zeros_like(q)
            qs = jnp.concatenate([jnp.where(first_map, q, zero),
                                  jnp.where(first_map, zero, q)], axis=0)
        for j in range(i + 1):
            if i < nq:
                s = _dot_nt(qs, k_ref[j * tq:(j + 1) * tq, :])
                if j == i:
                    s = jnp.where(keep, s, NEG)
                tiles.append(s)
                run_max = s if run_max is None else jnp.maximum(run_max, s)
            if prev is not None and j < len(prev[1]):
                p = jnp.exp2(prev[1][j] - prev[2]).astype(BF16)
                pv = _dot(p, v_ext[j * tq:(j + 1) * tq, :])
                acc = pv if acc is None else acc + pv
        if prev is not None:
            finish(prev[0], acc)
        if i < nq:
            prev = (i, tiles, jnp.max(run_max, axis=-1, keepdims=True))


def _diff_attention(qk, v, lq1, lk1, lq2, lk2, g_out, batch, seq, lam_init):
    t = batch * seq
    vec64 = pl.BlockSpec((1, DIFF_D), lambda b, h: (0, 0))
    head = pl.BlockSpec((seq, 2 * DIFF_D), lambda b, h: (b, h))
    return pl.pallas_call(
        functools.partial(_diff_kernel, lam_init=lam_init, seq=seq),
        grid=(batch, DIFF_HEADS),
        in_specs=[head,
                  pl.BlockSpec((seq, 2 * DIFF_D), lambda b, h: (b, DIFF_HEADS + h)),
                  head,
                  vec64, vec64, vec64, vec64,
                  pl.BlockSpec((1, 2 * DIFF_D), lambda b, h: (0, 0))],
        out_specs=head,
        out_shape=jax.ShapeDtypeStruct((t, DIFF_HEADS * 2 * DIFF_D), BF16),
        compiler_params=_params(("parallel", "parallel")),
        name="diff_attention",
    )(qk, qk, v, lq1, lk1, lq2, lk2, g_out)


def _memkv_kernel(mem_ref, g_ref, w_ref, gk_ref, k_ref, v_ref):
    mn = (_rms(mem_ref[...]) * g_ref[...]).astype(BF16)
    kv = _dot(mn, w_ref[...])
    width = MEM_HEADS * MEM_D
    k_ref[...] = _group_norm_128(kv[:, :width], gk_ref[...], MEM_D).astype(BF16)
    v_ref[...] = kv[:, width:].astype(BF16)


def _memkv(mem2d, g_mem, w_kv, gk, batch, mlen):
    width = MEM_HEADS * MEM_D
    row = pl.BlockSpec((mlen, width), lambda b: (b, 0))
    vec = pl.BlockSpec((1, width), lambda b: (0, 0))
    return pl.pallas_call(
        _memkv_kernel,
        grid=(batch,),
        in_specs=[pl.BlockSpec((mlen, D_MODEL), lambda b: (b, 0)), vec,
                  _resident(w_kv.shape), vec],
        out_specs=[row, row],
        out_shape=[jax.ShapeDtypeStruct((batch * mlen, width), BF16)] * 2,
        compiler_params=_params(("parallel",)),
        name="memkv",
    )(mem2d, g_mem, w_kv, gk)


def _mem_kernel(q_ref, k_ref, v_ref, o_ref):
    outs = []
    for hd in range(MEM_HEADS):
        sl = slice(hd * MEM_D, (hd + 1) * MEM_D)
        s = _dot_nt(q_ref[:, sl], k_ref[:, sl])
        p = jnp.exp(s - jnp.max(s, axis=-1, keepdims=True))
        l = jnp.sum(p, axis=-1, keepdims=True)
        outs.append(_dot(p.astype(BF16), v_ref[:, sl]) / l)
    o_ref[...] = jnp.concatenate(outs, axis=1).astype(BF16)


def _mem_attention(q, k, v, batch, seq, mlen):
    nq = seq // MEM_TQ
    width = MEM_HEADS * MEM_D
    qrow = pl.BlockSpec((MEM_TQ, width), lambda b, i: (b * nq + i, 0))
    kv = pl.BlockSpec((mlen, width), lambda b, i: (b, 0))
    return pl.pallas_call(
        _mem_kernel,
        grid=(batch, nq),
        in_specs=[qrow, kv, kv],
        out_specs=qrow,
        out_shape=jax.ShapeDtypeStruct((batch * seq, width), BF16),
        compiler_params=_params(("parallel", "parallel")),
        name="mem_attention",
    )(q, k, v)


def _merge_kernel(x_ref, ret_ref, dif_ref, mo_ref, gt_ref, wr_ref, wd_ref, wm_ref, wo_ref, o_ref):
    gt = gt_ref[...].astype(F32)
    y = (gt[:, :D_MODEL] * _dot(ret_ref[...], wr_ref[...])
         + gt[:, D_MODEL:2 * D_MODEL] * _dot(dif_ref[...], wd_ref[...])
         + gt[:, 2 * D_MODEL:] * _dot(mo_ref[...], wm_ref[...]))
    o_ref[...] = x_ref[...] + _dot(y.astype(BF16), wo_ref[...])


def _merge(x, ret, dif, mo, gates, wr, wd, wm, wo):
    t = x.shape[0]
    row = lambda w: pl.BlockSpec((TM, w), lambda i: (i, 0))
    return pl.pallas_call(
        _merge_kernel,
        grid=(t // TM,),
        in_specs=[row(D_MODEL), row(ret.shape[1]), row(D_MODEL), row(D_MODEL), row(3 * D_MODEL),
                  _resident(wr.shape), _resident(wd.shape), _resident(wm.shape),
                  _resident(wo.shape)],
        out_specs=row(D_MODEL),
        out_shape=jax.ShapeDtypeStruct((t, D_MODEL), F32),
        compiler_params=_params(("parallel",)),
        name="merge",
    )(x, ret, dif, mo, gates, wr, wd, wm, wo)


def _retention_constants():
    h = np.arange(RET_HEADS, dtype=np.float64)
    log_g = np.log1p(-(2.0 ** (-5.0 - h)))
    idx = np.arange(RET_C, dtype=np.float64)
    dist = idx[:, None] - idx[None, :]
    dec = np.where(dist >= 0, np.exp(log_g[:, None, None] * np.maximum(dist, 0.0)), 0.0)
    qd = np.exp(log_g[:, None] * (idx + 1.0))[:, :, None]
    kd = np.exp(log_g[:, None] * (RET_C - 1.0 - idx))[:, :, None]
    return (jnp.asarray(dec, F32), jnp.asarray(qd, F32), jnp.asarray(kd, F32))


def kernel(x, mem, positions, g_ffn1, w_ffn1_in, w_ffn1_out, g_mix, w_in, g_diff_q, g_diff_k,
           lam_q1, lam_k1, lam_q2, lam_k2, g_diff_out, g_mem_q, g_mem_k, g_mem, w_mem_kv,
           w_br_ret, w_br_diff, w_br_mem, w_o, g_ffn2, w_ffn2_in, w_ffn2_out):
    batch, seq, _ = x.shape
    mlen = mem.shape[1]
    t = batch * seq
    depth = g_ffn1.shape[0]
    half = RET_DK // 2
    inv = jnp.asarray(ROPE_BASE ** (-np.arange(half, dtype=np.float64) / half), F32)[None, :]
    dec, qd, kd = _retention_constants()
    pos = positions.reshape(t, 1)
    mem2d = mem.reshape(batch * mlen, D_MODEL)
    xf = x.reshape(t, D_MODEL)
    bf = lambda w: w.astype(BF16)
    vec = lambda g: g.astype(F32)[None, :]

    for l in range(depth):
        lam_init = 0.8 - 0.6 * math.exp(-0.3 * l)
        x1, h = _ffn(xf, vec(g_ffn1[l]), bf(w_ffn1_in[l]), bf(w_ffn1_out[l]), vec(g_mix[l]))

        w = bf(w_in[l])
        rqk, rv, rg = _proj_ret(h, w, pos, inv)
        qk_gain = jnp.concatenate([jnp.tile(g_diff_q[l].astype(F32), 2 * DIFF_HEADS) * (DIFF_D ** -0.5 * LOG2E),
                                   jnp.tile(g_diff_k[l].astype(F32), 2 * DIFF_HEADS)])[None, :]
        mq_gain = (jnp.tile(g_mem_q[l].astype(F32), MEM_HEADS) * (MEM_D ** -0.5))[None, :]
        dqk, dv, mq, gates = _proj_mix(h, w, qk_gain, mq_gain)

        ret = _retention(rqk, rv, rg, dec, qd, kd, batch, seq)
        go = (g_diff_out[l].astype(F32) * (1.0 - lam_init))[None, :]
        dif = _diff_attention(dqk, dv, vec(lam_q1[l]), vec(lam_k1[l]), vec(lam_q2[l]),
                              vec(lam_k2[l]), go, batch, seq, lam_init)
        mk, mv = _memkv(mem2d, vec(g_mem[l]), bf(w_mem_kv[l]),
                        jnp.tile(g_mem_k[l].astype(F32), MEM_HEADS)[None, :], batch, mlen)
        mo = _mem_attention(mq, mk, mv, batch, seq, mlen)

        x2 = _merge(x1, ret, dif, mo, gates, bf(w_br_ret[l]), bf(w_br_diff[l]),
                    bf(w_br_mem[l]), bf(w_o[l]))
        (xf,) = _ffn(x2, vec(g_ffn2[l]), bf(w_ffn2_in[l]), bf(w_ffn2_out[l]))
    return xf.reshape(batch, seq, D_MODEL)
```

```python
import functools
import math

import jax
import jax.numpy as jnp
import numpy as np
from jax import lax
from jax.experimental import pallas as pl
from jax.experimental.pallas import tpu as pltpu

F32 = jnp.float32
BF16 = jnp.bfloat16

D_MODEL = 1024
D_FF = 2816
FFN_RES = 0.5
EPS = 1e-6
NEG = -1e30
LOG2E = math.log2(math.e)
ROPE_BASE = 10000.0

RET_HEADS = 4
RET_DK = 256
RET_DV = 512
DIFF_HEADS = 8
DIFF_D = 64
MEM_HEADS = 4
MEM_D = 256

V7X_VMEM_LIMIT_BYTES = 56 * 1024 * 1024

TM = 512
TM_FFN = 1024
FFN_CHUNK = 256
RET_C = 256
DIFF_TQ = 256


def _params(sem):
    return pltpu.CompilerParams(dimension_semantics=sem,
                                vmem_limit_bytes=V7X_VMEM_LIMIT_BYTES)


def _resident(shape):
    nd = len(shape)
    return pl.BlockSpec(shape, lambda *_: (0,) * nd, pipeline_mode=pl.Buffered(1))


def _rms(x):
    return x * lax.rsqrt(jnp.mean(x * x, axis=-1, keepdims=True) + EPS)


def _after(x, *deps):
    folded = None
    for d in deps:
        t = jnp.sum(d.astype(F32), axis=0, keepdims=True)
        for c in range(t.shape[1] // 128):
            blk = t[:, c * 128:(c + 1) * 128]
            folded = blk if folded is None else folded + blk
    bits = lax.bitcast_convert_type(folded, jnp.uint32)
    zero = lax.shift_right_logical(lax.shift_right_logical(bits, jnp.uint32(16)), jnp.uint32(16))
    zero = zero.astype(F32).astype(x.dtype)
    zero = jnp.concatenate([zero] * (x.shape[1] // 128), axis=1)
    return jnp.concatenate([x[:16] + zero, x[16:]], axis=0)


def _dot(a, b):
    return jnp.dot(a, b, preferred_element_type=F32)


def _dot_nt(a, b):
    return lax.dot_general(a, b, (((1,), (1,)), ((), ())), preferred_element_type=F32)


def _dot_tn(a, b):
    return lax.dot_general(a, b, (((0,), (0,)), ((), ())), preferred_element_type=F32)


def _ffn_kernel(*refs, emit_next):
    if emit_next:
        x_ref, g_ref, win_ref, wout_ref, gn_ref, o_ref, h_ref = refs
    else:
        x_ref, g_ref, win_ref, wout_ref, o_ref = refs
    x = x_ref[...]
    hb = (_rms(x) * g_ref[...]).astype(BF16)
    acc = jnp.zeros(x.shape, F32)
    for j in range(D_FF // FFN_CHUNK):
        lo = j * FFN_CHUNK
        a = _dot(hb, win_ref[:, lo:lo + FFN_CHUNK])
        b = _dot(hb, win_ref[:, D_FF + lo:D_FF + lo + FFN_CHUNK])
        act = (a * jax.nn.sigmoid(a) * b).astype(BF16)
        acc = acc + _dot(act, wout_ref[lo:lo + FFN_CHUNK, :])
    y = x + FFN_RES * acc
    o_ref[...] = y
    if emit_next:
        h_ref[...] = (_rms(y) * gn_ref[...]).astype(BF16)


def _ffn(x, g, w_in, w_out, g_next=None):
    t = x.shape[0]
    emit = g_next is not None
    row = pl.BlockSpec((TM_FFN, D_MODEL), lambda i: (i, 0))
    vec = pl.BlockSpec((1, D_MODEL), lambda i: (0, 0))
    in_specs = [row, vec, _resident(w_in.shape), _resident(w_out.shape)]
    args = [x, g, w_in, w_out]
    out_shape = [jax.ShapeDtypeStruct((t, D_MODEL), F32)]
    out_specs = [row]
    if emit:
        in_specs.append(vec)
        args.append(g_next)
        out_shape.append(jax.ShapeDtypeStruct((t, D_MODEL), BF16))
        out_specs.append(row)
    return pl.pallas_call(
        functools.partial(_ffn_kernel, emit_next=emit),
        grid=(t // TM_FFN,),
        in_specs=in_specs,
        out_specs=out_specs,
        out_shape=out_shape,
        compiler_params=_params(("parallel",)),
        name="ffn_next" if emit else "ffn",
    )(*args)


def _group_norm_128(y, gain, group):
    cols = y.shape[1]
    outs = []
    if group == 64:
        lane = lax.broadcasted_iota(jnp.int32, (1, 128), 1)
        first = lane < 64
        for c in range(cols // 128):
            blk = y[:, c * 128:(c + 1) * 128]
            sq = blk * blk
            s_lo = jnp.sum(jnp.where(first, sq, 0.0), axis=-1, keepdims=True)
            s_hi = jnp.sum(jnp.where(first, 0.0, sq), axis=-1, keepdims=True)
            r = jnp.where(first, lax.rsqrt(s_lo * (1.0 / 64) + EPS),
                          lax.rsqrt(s_hi * (1.0 / 64) + EPS))
            outs.append(blk * r)
    else:
        for c in range(cols // group):
            blk = y[:, c * group:(c + 1) * group]
            outs.append(_rms(blk))
    return jnp.concatenate(outs, axis=1) * gain


RET_QK_W = RET_HEADS * RET_DK
RET_V_W = RET_HEADS * RET_DV
PROJ_RET_W = 2 * RET_QK_W + 2 * RET_V_W
DIFF_W = DIFF_HEADS * 2 * DIFF_D
MEM_W = MEM_HEADS * MEM_D
PROJ_MIX_W = 2 * DIFF_W + DIFF_W + MEM_W + 3 * D_MODEL
PROJ_CHUNK = 512

def _proj_ret_kernel(h_ref, w_ref, pos_ref, inv_ref, qk_ref, v_ref, g_ref):
    h = h_ref[...]
    tm = h.shape[0]
    n_piece = 8
    rows = tm // n_piece
    cos_p, sin_p = [], []
    for r in range(n_piece):
        ang = pos_ref[r * rows:(r + 1) * rows, :].astype(F32) * inv_ref[...]
        cos_p.append(jnp.cos(ang))
        sin_p.append(jnp.sin(ang))
    cos = jnp.concatenate(cos_p, axis=0)
    sin = jnp.concatenate(sin_p, axis=0)
    half = RET_DK // 2

    def rot(lhs, hd):
        lo = hd * RET_DK
        y = _dot(lhs, w_ref[:, lo:lo + RET_DK])
        x1 = y[:, :half]
        x2 = y[:, half:]
        r1 = x1 * cos - x2 * sin
        r2 = x1 * sin + x2 * cos
        if hd >= RET_HEADS:
            r1 = r1 * (RET_DK ** -0.5)
            r2 = r2 * (RET_DK ** -0.5)
        r1 = r1.astype(BF16)
        r2 = r2.astype(BF16)
        qk_ref[:, lo:lo + half] = r1
        qk_ref[:, lo + half:lo + RET_DK] = r2
        return r1, r2

    def val(lhs, c):
        lo = c * PROJ_CHUNK
        v_ref[:, lo:lo + PROJ_CHUNK] = _dot(
            lhs, w_ref[:, 2 * RET_QK_W + lo:2 * RET_QK_W + lo + PROJ_CHUNK]).astype(BF16)

    def gate(lhs, c):
        lo = c * PROJ_CHUNK
        wlo = 2 * RET_QK_W + RET_V_W + lo
        y = _dot(lhs, w_ref[:, wlo:wlo + PROJ_CHUNK])
        g_ref[:, lo:lo + PROJ_CHUNK] = (y * jax.nn.sigmoid(y)).astype(BF16)

    plain = [(val, 0), (gate, 0), (val, 1), (gate, 1), (val, 2), (gate, 2), (val, 3), (gate, 3)]
    for r, (fn, idx) in enumerate(plain):
        fn(h if r == 0 else _after(h, cos_p[r - 1], sin_p[r - 1]), idx)
    for hd in range(2 * RET_HEADS):
        rot(h, hd)


def _proj_ret(h, w, pos, inv):
    t = h.shape[0]
    row = lambda width: pl.BlockSpec((TM, width), lambda i: (i, 0))
    return pl.pallas_call(
        _proj_ret_kernel,
        grid=(t // TM,),
        in_specs=[row(D_MODEL),
                  pl.BlockSpec((D_MODEL, PROJ_RET_W), lambda i: (0, 0),
                               pipeline_mode=pl.Buffered(1)),
                  row(1),
                  pl.BlockSpec((1, 128), lambda i: (0, 0))],
        out_specs=[row(2 * RET_QK_W), row(RET_V_W), row(RET_V_W)],
        out_shape=[jax.ShapeDtypeStruct((t, 2 * RET_QK_W), BF16),
                   jax.ShapeDtypeStruct((t, RET_V_W), BF16),
                   jax.ShapeDtypeStruct((t, RET_V_W), BF16)],
        compiler_params=_params(("parallel",)),
        name="proj_ret",
    )(h, w, pos, inv)


def _proj_mix_kernel(h_ref, w_ref, gqk_ref, gmq_ref, mk_ref, mv_ref,
                     qk_ref, v_ref, mo_ref, gt_ref):
    h = h_ref[...]
    off_qk = PROJ_RET_W
    off_v = off_qk + 2 * DIFF_W
    off_mq = off_v + DIFF_W
    off_gt = off_mq + MEM_W

    mem_q, mem_p = {}, {}

    def qk_norm(lhs, c):
        lo = c * 256
        y = _dot(lhs, w_ref[:, off_qk + lo:off_qk + lo + 256])
        out = _group_norm_128(y, gqk_ref[:, lo:lo + 256], DIFF_D).astype(BF16)
        qk_ref[:, lo:lo + 256] = out
        return out

    def val(lhs, c):
        lo = c * PROJ_CHUNK
        out = _dot(lhs, w_ref[:, off_v + lo:off_v + lo + PROJ_CHUNK]).astype(BF16)
        v_ref[:, lo:lo + PROJ_CHUNK] = out
        return out

    def gate(lhs, c):
        lo = c * PROJ_CHUNK
        y = _dot(lhs, w_ref[:, off_gt + lo:off_gt + lo + PROJ_CHUNK])
        out = jax.nn.sigmoid(y).astype(BF16)
        gt_ref[:, lo:lo + PROJ_CHUNK] = out
        return out

    def mem_query(lhs, hd):
        sl = slice(hd * MEM_D, (hd + 1) * MEM_D)
        y = _dot(lhs, w_ref[:, off_mq + hd * MEM_D:off_mq + (hd + 1) * MEM_D])
        mem_q[hd] = (_rms(y) * gmq_ref[:, sl]).astype(BF16)
        return mem_q[hd]

    def mem_scores(lhs, hd):
        sl = slice(hd * MEM_D, (hd + 1) * MEM_D)
        s = _dot_nt(mem_q[hd], mk_ref[:, sl])
        p = jnp.exp2(s - jnp.max(s, axis=-1, keepdims=True))
        mem_p[hd] = (p.astype(BF16), jnp.sum(p, axis=-1, keepdims=True))
        return mem_p[hd][0]

    def mem_values(lhs, hd):
        sl = slice(hd * MEM_D, (hd + 1) * MEM_D)
        p, l = mem_p[hd]
        out = (_dot(p, mv_ref[:, sl]) / l).astype(BF16)
        mo_ref[:, sl] = out
        return out

    order = [(gate, 0), (mem_query, 0), (qk_norm, 0), (mem_scores, 0), (qk_norm, 1),
             (mem_values, 0), (gate, 1), (mem_query, 1), (qk_norm, 2), (mem_scores, 1),
             (qk_norm, 3), (mem_values, 1), (val, 0), (mem_query, 2), (qk_norm, 4),
             (mem_scores, 2), (qk_norm, 5), (mem_values, 2), (gate, 2), (mem_query, 3),
             (qk_norm, 6), (mem_scores, 3), (qk_norm, 7), (mem_values, 3), (val, 1),
             (gate, 3), (gate, 4), (gate, 5)]
    for fn, idx in order:
        fn(h, idx)


def _proj_mix(h, w, gqk, gmq, mk, mv, seq):
    t = h.shape[0]
    mlen = mk.shape[0] // (t // seq)
    tiles_per_batch = seq // TM
    row = lambda width: pl.BlockSpec((TM, width), lambda i: (i, 0))
    vec = lambda width: pl.BlockSpec((1, width), lambda i: (0, 0))
    kv = pl.BlockSpec((mlen, MEM_W), lambda i: (i // tiles_per_batch, 0))
    return pl.pallas_call(
        _proj_mix_kernel,
        grid=(t // TM,),
        in_specs=[row(D_MODEL), _resident(w.shape), vec(2 * DIFF_W), vec(MEM_W), kv, kv],
        out_specs=[row(2 * DIFF_W), row(DIFF_W), row(MEM_W), row(3 * D_MODEL)],
        out_shape=[jax.ShapeDtypeStruct((t, 2 * DIFF_W), BF16),
                   jax.ShapeDtypeStruct((t, DIFF_W), BF16),
                   jax.ShapeDtypeStruct((t, MEM_W), BF16),
                   jax.ShapeDtypeStruct((t, 3 * D_MODEL), BF16)],
        compiler_params=_params(("parallel",)),
        name="proj_mix",
    )(h, w, gqk, gmq, mk, mv)


def _ret_kernel(q_ref, k_ref, v_ref, g_ref, dec_ref, qd_ref, kd_ref, o_ref, *, seq):
    dec = dec_ref[...]
    qd = qd_ref[...]
    kd = kd_ref[...]
    chunk_decay = qd[RET_C - 1:RET_C, :]
    nc = seq // RET_C
    st = None
    for c in range(nc):
        lo = c * RET_C
        q = q_ref[lo:lo + RET_C, :]
        k = k_ref[lo:lo + RET_C, :]
        v = v_ref[lo:lo + RET_C, :]
        s = _dot_nt(q, k) * dec
        out = _dot(s.astype(BF16), v)
        if st is not None:
            out = out + _dot(q, st.astype(BF16)) * qd
        if c + 1 < nc:
            upd = _dot_tn((k.astype(F32) * kd).astype(BF16), v)
            st = upd if st is None else chunk_decay * st + upd
        o_ref[lo:lo + RET_C, :] = (_rms(out) * g_ref[lo:lo + RET_C, :].astype(F32)).astype(BF16)


def _retention(qk, v, gate, dec, qd, kd, batch, seq):
    t = batch * seq
    head = lambda b, h: (b, h)
    const = lambda b, h: (h, 0, 0)
    return pl.pallas_call(
        functools.partial(_ret_kernel, seq=seq),
        grid=(batch, RET_HEADS),
        in_specs=[pl.BlockSpec((seq, RET_DK), head),
                  pl.BlockSpec((seq, RET_DK), lambda b, h: (b, RET_HEADS + h)),
                  pl.BlockSpec((seq, RET_DV), head),
                  pl.BlockSpec((seq, RET_DV), head),
                  pl.BlockSpec((None, RET_C, RET_C), const),
                  pl.BlockSpec((None, RET_C, 1), const),
                  pl.BlockSpec((None, RET_C, 1), const)],
        out_specs=pl.BlockSpec((seq, RET_DV), head),
        out_shape=jax.ShapeDtypeStruct((t, RET_V_W), BF16),
        compiler_params=_params(("parallel", "parallel")),
        name="retention",
    )(qk, qk, v, gate, dec, qd, kd)


def _diff_kernel(q_ref, k_ref, v_ref, lq1_ref, lk1_ref, lq2_ref, lk2_ref, go_ref, o_ref,
                 *, lam_init, seq):
    tq = DIFF_TQ
    lane = lax.broadcasted_iota(jnp.int32, (1, 2 * DIFF_D), 1)
    first_map = lane < DIFF_D
    r = lax.broadcasted_iota(jnp.int32, (tq, tq), 0)
    cidx = lax.broadcasted_iota(jnp.int32, (tq, tq), 1)
    keep = cidx <= r
    keep = jnp.concatenate([keep, keep], axis=0)
    lam = (jnp.exp(jnp.sum(lq1_ref[...] * lk1_ref[...], axis=-1, keepdims=True))
           - jnp.exp(jnp.sum(lq2_ref[...] * lk2_ref[...], axis=-1, keepdims=True))
           + lam_init)
    go = go_ref[...]
    nq = seq // tq
    v_ext = jnp.concatenate([v_ref[...], jnp.ones((seq, 2 * DIFF_D), BF16)], axis=1)

    def finish(i, o_ext):
        r_ = o_ext[:, :2 * DIFF_D] / o_ext[:, 2 * DIFF_D:]
        d = r_[:tq] - lam * r_[tq:]
        o_ref[i * tq:(i + 1) * tq, :] = (_rms(d) * go).astype(BF16)

    prev = None
    for i in range(nq + 1):
        tiles = []
        run_max = None
        acc = None
        if i < nq:
            q = q_ref[i * tq:(i + 1) * tq, :]
            zero = jnp.zeros_like(q)
            qs = jnp.concatenate([jnp.where(first_map, q, zero),
                                  jnp.where(first_map, zero, q)], axis=0)
        for j in range(i + 1):
            if i < nq:
                s = _dot_nt(qs, k_ref[j * tq:(j + 1) * tq, :])
                if j == i:
                    s = jnp.where(keep, s, NEG)
                tiles.append(s)
                run_max = s if run_max is None else jnp.maximum(run_max, s)
            if prev is not None and j < len(prev[1]):
                p = jnp.exp2(prev[1][j] - prev[2]).astype(BF16)
                pv = _dot(p, v_ext[j * tq:(j + 1) * tq, :])
                acc = pv if acc is None else acc + pv
        if prev is not None:
            finish(prev[0], acc)
        if i < nq:
            prev = (i, tiles, jnp.max(run_max, axis=-1, keepdims=True))


def _diff_attention(qk, v, lq1, lk1, lq2, lk2, g_out, batch, seq, lam_init):
    t = batch * seq
    vec64 = pl.BlockSpec((1, DIFF_D), lambda b, h: (0, 0))
    head = pl.BlockSpec((seq, 2 * DIFF_D), lambda b, h: (b, h))
    return pl.pallas_call(
        functools.partial(_diff_kernel, lam_init=lam_init, seq=seq),
        grid=(batch, DIFF_HEADS),
        in_specs=[head,
                  pl.BlockSpec((seq, 2 * DIFF_D), lambda b, h: (b, DIFF_HEADS + h)),
                  head,
                  vec64, vec64, vec64, vec64,
                  pl.BlockSpec((1, 2 * DIFF_D), lambda b, h: (0, 0))],
        out_specs=head,
        out_shape=jax.ShapeDtypeStruct((t, DIFF_HEADS * 2 * DIFF_D), BF16),
        compiler_params=_params(("parallel", "parallel")),
        name="diff_attention",
    )(qk, qk, v, lq1, lk1, lq2, lk2, g_out)


def _memkv_kernel(mem_ref, g_ref, w_ref, gk_ref, k_ref, v_ref):
    mn = (_rms(mem_ref[...]) * g_ref[...]).astype(BF16)
    kv = _dot(mn, w_ref[...])
    width = MEM_HEADS * MEM_D
    k_ref[...] = _group_norm_128(kv[:, :width], gk_ref[...], MEM_D).astype(BF16)
    v_ref[...] = kv[:, width:].astype(BF16)


def _memkv(mem2d, g_mem, w_kv, gk, batch, mlen):
    width = MEM_HEADS * MEM_D
    row = pl.BlockSpec((mlen, width), lambda b: (b, 0))
    vec = pl.BlockSpec((1, width), lambda b: (0, 0))
    return pl.pallas_call(
        _memkv_kernel,
        grid=(batch,),
        in_specs=[pl.BlockSpec((mlen, D_MODEL), lambda b: (b, 0)), vec,
                  _resident(w_kv.shape), vec],
        out_specs=[row, row],
        out_shape=[jax.ShapeDtypeStruct((batch * mlen, width), BF16)] * 2,
        compiler_params=_params(("parallel",)),
        name="memkv",
    )(mem2d, g_mem, w_kv, gk)


def _merge_kernel(x_ref, ret_ref, dif_ref, mo_ref, gt_ref, wr_ref, wd_ref, wm_ref, wo_ref, o_ref):
    gt = gt_ref[...].astype(F32)
    y = (gt[:, :D_MODEL] * _dot(ret_ref[...], wr_ref[...])
         + gt[:, D_MODEL:2 * D_MODEL] * _dot(dif_ref[...], wd_ref[...])
         + gt[:, 2 * D_MODEL:] * _dot(mo_ref[...], wm_ref[...]))
    o_ref[...] = x_ref[...] + _dot(y.astype(BF16), wo_ref[...])


def _merge(x, ret, dif, mo, gates, wr, wd, wm, wo):
    t = x.shape[0]
    row = lambda w: pl.BlockSpec((TM, w), lambda i: (i, 0))
    return pl.pallas_call(
        _merge_kernel,
        grid=(t // TM,),
        in_specs=[row(D_MODEL), row(ret.shape[1]), row(D_MODEL), row(D_MODEL), row(3 * D_MODEL),
                  _resident(wr.shape), _resident(wd.shape), _resident(wm.shape),
                  _resident(wo.shape)],
        out_specs=row(D_MODEL),
        out_shape=jax.ShapeDtypeStruct((t, D_MODEL), F32),
        compiler_params=_params(("parallel",)),
        name="merge",
    )(x, ret, dif, mo, gates, wr, wd, wm, wo)


def _retention_constants():
    h = np.arange(RET_HEADS, dtype=np.float64)
    log_g = np.log1p(-(2.0 ** (-5.0 - h)))
    idx = np.arange(RET_C, dtype=np.float64)
    dist = idx[:, None] - idx[None, :]
    dec = np.where(dist >= 0, np.exp(log_g[:, None, None] * np.maximum(dist, 0.0)), 0.0)
    qd = np.exp(log_g[:, None] * (idx + 1.0))[:, :, None]
    kd = np.exp(log_g[:, None] * (RET_C - 1.0 - idx))[:, :, None]
    return (jnp.asarray(dec, F32), jnp.asarray(qd, F32), jnp.asarray(kd, F32))


def kernel(x, mem, positions, g_ffn1, w_ffn1_in, w_ffn1_out, g_mix, w_in, g_diff_q, g_diff_k,
           lam_q1, lam_k1, lam_q2, lam_k2, g_diff_out, g_mem_q, g_mem_k, g_mem, w_mem_kv,
           w_br_ret, w_br_diff, w_br_mem, w_o, g_ffn2, w_ffn2_in, w_ffn2_out):
    batch, seq, _ = x.shape
    mlen = mem.shape[1]
    t = batch * seq
    depth = g_ffn1.shape[0]
    half = RET_DK // 2
    inv = jnp.asarray(ROPE_BASE ** (-np.arange(half, dtype=np.float64) / half), F32)[None, :]
    dec, qd, kd = _retention_constants()
    pos = positions.reshape(t, 1)
    mem2d = mem.reshape(batch * mlen, D_MODEL)
    xf = x.reshape(t, D_MODEL)
    bf = lambda w: w.astype(BF16)
    vec = lambda g: g.astype(F32)[None, :]

    for l in range(depth):
        lam_init = 0.8 - 0.6 * math.exp(-0.3 * l)
        x1, h = _ffn(xf, vec(g_ffn1[l]), bf(w_ffn1_in[l]), bf(w_ffn1_out[l]), vec(g_mix[l]))

        w = bf(w_in[l])
        rqk, rv, rg = _proj_ret(h, w, pos, inv)
        qk_gain = jnp.concatenate([jnp.tile(g_diff_q[l].astype(F32), 2 * DIFF_HEADS) * (DIFF_D ** -0.5 * LOG2E),
                                   jnp.tile(g_diff_k[l].astype(F32), 2 * DIFF_HEADS)])[None, :]
        mq_gain = (jnp.tile(g_mem_q[l].astype(F32), MEM_HEADS) * (MEM_D ** -0.5 * LOG2E))[None, :]
        mk, mv = _memkv(mem2d, vec(g_mem[l]), bf(w_mem_kv[l]),
                        jnp.tile(g_mem_k[l].astype(F32), MEM_HEADS)[None, :], batch, mlen)
        dqk, dv, mo, gates = _proj_mix(h, w, qk_gain, mq_gain, mk, mv, seq)

        ret = _retention(rqk, rv, rg, dec, qd, kd, batch, seq)
        go = (g_diff_out[l].astype(F32) * (1.0 - lam_init))[None, :]
        dif = _diff_attention(dqk, dv, vec(lam_q1[l]), vec(lam_k1[l]), vec(lam_q2[l]),
                              vec(lam_k2[l]), go, batch, seq, lam_init)

        x2 = _merge(x1, ret, dif, mo, gates, bf(w_br_ret[l]), bf(w_br_diff[l]),
                    bf(w_br_mem[l]), bf(w_o[l]))
        (xf,) = _ffn(x2, vec(g_ffn2[l]), bf(w_ffn2_in[l]), bf(w_ffn2_out[l]))
    return xf.reshape(batch, seq, D_MODEL)
```

```python
import functools
import math

import jax
import jax.numpy as jnp
import numpy as np
from jax import lax
from jax.experimental import pallas as pl
from jax.experimental.pallas import tpu as pltpu

F32 = jnp.float32
BF16 = jnp.bfloat16

D_MODEL = 1024
D_FF = 2816
FFN_RES = 0.5
EPS = 1e-6
NEG = -1e30
LOG2E = math.log2(math.e)
ROPE_BASE = 10000.0

RET_HEADS = 4
RET_DK = 256
RET_DV = 512
DIFF_HEADS = 8
DIFF_D = 64
MEM_HEADS = 4
MEM_D = 256

V7X_VMEM_LIMIT_BYTES = 56 * 1024 * 1024

TM = 512
TM_FFN = 1024
FFN_CHUNK = 256
RET_C = 256
RET_HEADS_PER_STEP = 2
DIFF_TQ = 256
DIFF_HEADS_PER_STEP = 2


def _params(sem):
    return pltpu.CompilerParams(dimension_semantics=sem,
                                vmem_limit_bytes=V7X_VMEM_LIMIT_BYTES)


def _resident(shape):
    nd = len(shape)
    return pl.BlockSpec(shape, lambda *_: (0,) * nd, pipeline_mode=pl.Buffered(1))


def _rms(x):
    return x * lax.rsqrt(jnp.mean(x * x, axis=-1, keepdims=True) + EPS)


def _after(x, *deps):
    folded = None
    for d in deps:
        t = jnp.sum(d.astype(F32), axis=0, keepdims=True)
        for c in range(t.shape[1] // 128):
            blk = t[:, c * 128:(c + 1) * 128]
            folded = blk if folded is None else folded + blk
    bits = lax.bitcast_convert_type(folded, jnp.uint32)
    zero = lax.shift_right_logical(lax.shift_right_logical(bits, jnp.uint32(16)), jnp.uint32(16))
    zero = zero.astype(F32).astype(x.dtype)
    zero = jnp.concatenate([zero] * (x.shape[1] // 128), axis=1)
    return jnp.concatenate([x[:16] + zero, x[16:]], axis=0)


def _dot(a, b):
    return jnp.dot(a, b, preferred_element_type=F32)


def _dot_nt(a, b):
    return lax.dot_general(a, b, (((1,), (1,)), ((), ())), preferred_element_type=F32)


def _dot_tn(a, b):
    return lax.dot_general(a, b, (((0,), (0,)), ((), ())), preferred_element_type=F32)


def _ffn_kernel(*refs, emit_next):
    if emit_next:
        x_ref, g_ref, win_ref, wout_ref, gn_ref, o_ref, h_ref = refs
    else:
        x_ref, g_ref, win_ref, wout_ref, o_ref = refs
    x = x_ref[...]
    hb = (_rms(x) * g_ref[...]).astype(BF16)
    acc = jnp.zeros(x.shape, F32)
    for j in range(D_FF // FFN_CHUNK):
        lo = j * FFN_CHUNK
        a = _dot(hb, win_ref[:, lo:lo + FFN_CHUNK])
        b = _dot(hb, win_ref[:, D_FF + lo:D_FF + lo + FFN_CHUNK])
        act = (a * jax.nn.sigmoid(a) * b).astype(BF16)
        acc = acc + _dot(act, wout_ref[lo:lo + FFN_CHUNK, :])
    y = x + FFN_RES * acc
    o_ref[...] = y
    if emit_next:
        h_ref[...] = (_rms(y) * gn_ref[...]).astype(BF16)


def _ffn(x, g, w_in, w_out, g_next=None):
    t = x.shape[0]
    emit = g_next is not None
    row = pl.BlockSpec((TM_FFN, D_MODEL), lambda i: (i, 0))
    vec = pl.BlockSpec((1, D_MODEL), lambda i: (0, 0))
    in_specs = [row, vec, _resident(w_in.shape), _resident(w_out.shape)]
    args = [x, g, w_in, w_out]
    out_shape = [jax.ShapeDtypeStruct((t, D_MODEL), F32)]
    out_specs = [row]
    if emit:
        in_specs.append(vec)
        args.append(g_next)
        out_shape.append(jax.ShapeDtypeStruct((t, D_MODEL), BF16))
        out_specs.append(row)
    return pl.pallas_call(
        functools.partial(_ffn_kernel, emit_next=emit),
        grid=(t // TM_FFN,),
        in_specs=in_specs,
        out_specs=out_specs,
        out_shape=out_shape,
        compiler_params=_params(("parallel",)),
        name="ffn_next" if emit else "ffn",
    )(*args)


def _group_norm_128(y, gain, group):
    cols = y.shape[1]
    outs = []
    if group == 64:
        lane = lax.broadcasted_iota(jnp.int32, (1, 128), 1)
        first = lane < 64
        for c in range(cols // 128):
            blk = y[:, c * 128:(c + 1) * 128]
            sq = blk * blk
            s_lo = jnp.sum(jnp.where(first, sq, 0.0), axis=-1, keepdims=True)
            s_hi = jnp.sum(jnp.where(first, 0.0, sq), axis=-1, keepdims=True)
            r = jnp.where(first, lax.rsqrt(s_lo * (1.0 / 64) + EPS),
                          lax.rsqrt(s_hi * (1.0 / 64) + EPS))
            outs.append(blk * r)
    else:
        for c in range(cols // group):
            blk = y[:, c * group:(c + 1) * group]
            outs.append(_rms(blk))
    return jnp.concatenate(outs, axis=1) * gain


RET_QK_W = RET_HEADS * RET_DK
RET_V_W = RET_HEADS * RET_DV
PROJ_RET_W = 2 * RET_QK_W + 2 * RET_V_W
DIFF_W = DIFF_HEADS * 2 * DIFF_D
MEM_W = MEM_HEADS * MEM_D
PROJ_MIX_W = 2 * DIFF_W + DIFF_W + MEM_W + 3 * D_MODEL
PROJ_CHUNK = 512

def _proj_ret_kernel(h_ref, w_ref, pos_ref, inv_ref, qk_ref, v_ref, g_ref):
    h = h_ref[...]
    tm = h.shape[0]
    n_piece = 8
    rows = tm // n_piece
    cos_p, sin_p = [], []
    for r in range(n_piece):
        ang = pos_ref[r * rows:(r + 1) * rows, :].astype(F32) * inv_ref[...]
        cos_p.append(jnp.cos(ang))
        sin_p.append(jnp.sin(ang))
    cos = jnp.concatenate(cos_p, axis=0)
    sin = jnp.concatenate(sin_p, axis=0)
    half = RET_DK // 2

    def rot(lhs, hd):
        lo = hd * RET_DK
        y = _dot(lhs, w_ref[:, lo:lo + RET_DK])
        x1 = y[:, :half]
        x2 = y[:, half:]
        r1 = x1 * cos - x2 * sin
        r2 = x1 * sin + x2 * cos
        if hd >= RET_HEADS:
            r1 = r1 * (RET_DK ** -0.5)
            r2 = r2 * (RET_DK ** -0.5)
        r1 = r1.astype(BF16)
        r2 = r2.astype(BF16)
        qk_ref[:, lo:lo + half] = r1
        qk_ref[:, lo + half:lo + RET_DK] = r2
        return r1, r2

    def val(lhs, c):
        lo = c * PROJ_CHUNK
        v_ref[:, lo:lo + PROJ_CHUNK] = _dot(
            lhs, w_ref[:, 2 * RET_QK_W + lo:2 * RET_QK_W + lo + PROJ_CHUNK]).astype(BF16)

    def gate(lhs, c):
        lo = c * PROJ_CHUNK
        wlo = 2 * RET_QK_W + RET_V_W + lo
        y = _dot(lhs, w_ref[:, wlo:wlo + PROJ_CHUNK])
        g_ref[:, lo:lo + PROJ_CHUNK] = (y * jax.nn.sigmoid(y)).astype(BF16)

    plain = [(val, 0), (gate, 0), (val, 1), (gate, 1), (val, 2), (gate, 2), (val, 3), (gate, 3)]
    for r, (fn, idx) in enumerate(plain):
        fn(h if r == 0 else _after(h, cos_p[r - 1], sin_p[r - 1]), idx)
    for hd in range(2 * RET_HEADS):
        rot(h, hd)


def _proj_ret(h, w, pos, inv):
    t = h.shape[0]
    row = lambda width: pl.BlockSpec((TM, width), lambda i: (i, 0))
    return pl.pallas_call(
        _proj_ret_kernel,
        grid=(t // TM,),
        in_specs=[row(D_MODEL),
                  pl.BlockSpec((D_MODEL, PROJ_RET_W), lambda i: (0, 0),
                               pipeline_mode=pl.Buffered(1)),
                  row(1),
                  pl.BlockSpec((1, 128), lambda i: (0, 0))],
        out_specs=[row(2 * RET_QK_W), row(RET_V_W), row(RET_V_W)],
        out_shape=[jax.ShapeDtypeStruct((t, 2 * RET_QK_W), BF16),
                   jax.ShapeDtypeStruct((t, RET_V_W), BF16),
                   jax.ShapeDtypeStruct((t, RET_V_W), BF16)],
        compiler_params=_params(("parallel",)),
        name="proj_ret",
    )(h, w, pos, inv)


def _proj_mix_kernel(h_ref, w_ref, gqk_ref, gmq_ref, mk_ref, mv_ref,
                     qk_ref, v_ref, mo_ref, gt_ref):
    h = h_ref[...]
    off_qk = PROJ_RET_W
    off_v = off_qk + 2 * DIFF_W
    off_mq = off_v + DIFF_W
    off_gt = off_mq + MEM_W

    mem_q, mem_p = {}, {}

    def qk_norm(lhs, c):
        lo = c * 256
        y = _dot(lhs, w_ref[:, off_qk + lo:off_qk + lo + 256])
        out = _group_norm_128(y, gqk_ref[:, lo:lo + 256], DIFF_D).astype(BF16)
        qk_ref[:, lo:lo + 256] = out
        return out

    def val(lhs, c):
        lo = c * PROJ_CHUNK
        out = _dot(lhs, w_ref[:, off_v + lo:off_v + lo + PROJ_CHUNK]).astype(BF16)
        v_ref[:, lo:lo + PROJ_CHUNK] = out
        return out

    def gate(lhs, c):
        lo = c * PROJ_CHUNK
        y = _dot(lhs, w_ref[:, off_gt + lo:off_gt + lo + PROJ_CHUNK])
        out = jax.nn.sigmoid(y).astype(BF16)
        gt_ref[:, lo:lo + PROJ_CHUNK] = out
        return out

    def mem_query(lhs, hd):
        sl = slice(hd * MEM_D, (hd + 1) * MEM_D)
        y = _dot(lhs, w_ref[:, off_mq + hd * MEM_D:off_mq + (hd + 1) * MEM_D])
        mem_q[hd] = (_rms(y) * gmq_ref[:, sl]).astype(BF16)
        return mem_q[hd]

    def mem_scores(lhs, hd):
        sl = slice(hd * MEM_D, (hd + 1) * MEM_D)
        s = _dot_nt(mem_q[hd], mk_ref[:, sl])
        p = jnp.exp2(s - jnp.max(s, axis=-1, keepdims=True))
        mem_p[hd] = (p.astype(BF16), jnp.sum(p, axis=-1, keepdims=True))
        return mem_p[hd][0]

    def mem_values(lhs, hd):
        sl = slice(hd * MEM_D, (hd + 1) * MEM_D)
        p, l = mem_p[hd]
        out = (_dot(p, mv_ref[:, sl]) / l).astype(BF16)
        mo_ref[:, sl] = out
        return out

    order = [(gate, 0), (mem_query, 0), (qk_norm, 0), (mem_scores, 0), (qk_norm, 1),
             (mem_values, 0), (gate, 1), (mem_query, 1), (qk_norm, 2), (mem_scores, 1),
             (qk_norm, 3), (mem_values, 1), (val, 0), (mem_query, 2), (qk_norm, 4),
             (mem_scores, 2), (qk_norm, 5), (mem_values, 2), (gate, 2), (mem_query, 3),
             (qk_norm, 6), (mem_scores, 3), (qk_norm, 7), (mem_values, 3), (val, 1),
             (gate, 3), (gate, 4), (gate, 5)]
    for fn, idx in order:
        fn(h, idx)


def _proj_mix(h, w, gqk, gmq, mk, mv, seq):
    t = h.shape[0]
    mlen = mk.shape[0] // (t // seq)
    tiles_per_batch = seq // TM
    row = lambda width: pl.BlockSpec((TM, width), lambda i: (i, 0))
    vec = lambda width: pl.BlockSpec((1, width), lambda i: (0, 0))
    kv = pl.BlockSpec((mlen, MEM_W), lambda i: (i // tiles_per_batch, 0))
    return pl.pallas_call(
        _proj_mix_kernel,
        grid=(t // TM,),
        in_specs=[row(D_MODEL), _resident(w.shape), vec(2 * DIFF_W), vec(MEM_W), kv, kv],
        out_specs=[row(2 * DIFF_W), row(DIFF_W), row(MEM_W), row(3 * D_MODEL)],
        out_shape=[jax.ShapeDtypeStruct((t, 2 * DIFF_W), BF16),
                   jax.ShapeDtypeStruct((t, DIFF_W), BF16),
                   jax.ShapeDtypeStruct((t, MEM_W), BF16),
                   jax.ShapeDtypeStruct((t, 3 * D_MODEL), BF16)],
        compiler_params=_params(("parallel",)),
        name="proj_mix",
    )(h, w, gqk, gmq, mk, mv)


def _ret_kernel(q_ref, k_ref, v_ref, g_ref, dec_ref, qd_ref, kd_ref, o_ref, *, seq):
    nc = seq // RET_C
    heads = RET_HEADS_PER_STEP
    st = [None] * heads
    for c in range(nc):
        lo = c * RET_C
        for hd in range(heads):
            qd = qd_ref[hd]
            q = q_ref[lo:lo + RET_C, hd * RET_DK:(hd + 1) * RET_DK]
            k = k_ref[lo:lo + RET_C, hd * RET_DK:(hd + 1) * RET_DK]
            v = v_ref[lo:lo + RET_C, hd * RET_DV:(hd + 1) * RET_DV]
            s = _dot_nt(q, k) * dec_ref[hd]
            out = _dot(s.astype(BF16), v)
            if st[hd] is not None:
                out = out + _dot(q, st[hd].astype(BF16)) * qd
            if c + 1 < nc:
                upd = _dot_tn((k.astype(F32) * kd_ref[hd]).astype(BF16), v)
                chunk_decay = qd[RET_C - 1:RET_C, :]
                st[hd] = upd if st[hd] is None else chunk_decay * st[hd] + upd
            gate = g_ref[lo:lo + RET_C, hd * RET_DV:(hd + 1) * RET_DV].astype(F32)
            o_ref[lo:lo + RET_C, hd * RET_DV:(hd + 1) * RET_DV] = (_rms(out) * gate).astype(BF16)


def _retention(qk, v, gate, dec, qd, kd, batch, seq):
    t = batch * seq
    hp = RET_HEADS_PER_STEP
    groups = RET_HEADS // hp
    head = lambda b, h: (b, h)
    const = lambda b, h: (h, 0, 0)
    return pl.pallas_call(
        functools.partial(_ret_kernel, seq=seq),
        grid=(batch, groups),
        in_specs=[pl.BlockSpec((seq, hp * RET_DK), head),
                  pl.BlockSpec((seq, hp * RET_DK), lambda b, h: (b, groups + h)),
                  pl.BlockSpec((seq, hp * RET_DV), head),
                  pl.BlockSpec((seq, hp * RET_DV), head),
                  pl.BlockSpec((hp, RET_C, RET_C), const),
                  pl.BlockSpec((hp, RET_C, 1), const),
                  pl.BlockSpec((hp, RET_C, 1), const)],
        out_specs=pl.BlockSpec((seq, hp * RET_DV), head),
        out_shape=jax.ShapeDtypeStruct((t, RET_V_W), BF16),
        compiler_params=_params(("parallel", "parallel")),
        name="retention",
    )(qk, qk, v, gate, dec, qd, kd)


def _diff_kernel(q_ref, k_ref, v_ref, lq1_ref, lk1_ref, lq2_ref, lk2_ref, go_ref, o_ref,
                 *, lam_init, seq):
    tq = DIFF_TQ
    lane = lax.broadcasted_iota(jnp.int32, (1, 2 * DIFF_D), 1)
    first_map = lane < DIFF_D
    r = lax.broadcasted_iota(jnp.int32, (tq, tq), 0)
    cidx = lax.broadcasted_iota(jnp.int32, (tq, tq), 1)
    keep = cidx <= r
    keep = jnp.concatenate([keep, keep], axis=0)
    lam = (jnp.exp(jnp.sum(lq1_ref[...] * lk1_ref[...], axis=-1, keepdims=True))
           - jnp.exp(jnp.sum(lq2_ref[...] * lk2_ref[...], axis=-1, keepdims=True))
           + lam_init)
    go = go_ref[...]
    nq = seq // tq
    hw = 2 * DIFF_D
    heads = DIFF_HEADS_PER_STEP
    ones = jnp.ones((seq, hw), BF16)
    v_ext = [jnp.concatenate([v_ref[:, hd * hw:(hd + 1) * hw], ones], axis=1)
             for hd in range(heads)]

    def finish(hd, i, o_ext):
        r_ = o_ext[:, :hw] / o_ext[:, hw:]
        d = r_[:tq] - lam * r_[tq:]
        o_ref[i * tq:(i + 1) * tq, hd * hw:(hd + 1) * hw] = (_rms(d) * go).astype(BF16)

    prev = [None] * heads
    for i in list(range(nq)) + [None]:
        tiles = [[] for _ in range(heads)]
        run_max = [None] * heads
        acc = [None] * heads
        qs = [None] * heads
        if i is not None:
            for hd in range(heads):
                q = q_ref[i * tq:(i + 1) * tq, hd * hw:(hd + 1) * hw]
                zero = jnp.zeros_like(q)
                qs[hd] = jnp.concatenate([jnp.where(first_map, q, zero),
                                          jnp.where(first_map, zero, q)], axis=0)
        n_cur = 0 if i is None else i + 1
        n_prev = 0 if prev[0] is None else len(prev[0][1])
        for j in range(max(n_cur, n_prev)):
            for hd in range(heads):
                if j < n_cur:
                    s = _dot_nt(qs[hd], k_ref[j * tq:(j + 1) * tq, hd * hw:(hd + 1) * hw])
                    if j == i:
                        s = jnp.where(keep, s, NEG)
                    tiles[hd].append(s)
                    run_max[hd] = s if run_max[hd] is None else jnp.maximum(run_max[hd], s)
                if j < n_prev:
                    p = jnp.exp2(prev[hd][1][j] - prev[hd][2]).astype(BF16)
                    pv = _dot(p, v_ext[hd][j * tq:(j + 1) * tq, :])
                    acc[hd] = pv if acc[hd] is None else acc[hd] + pv
        for hd in range(heads):
            if prev[hd] is not None:
                finish(hd, prev[hd][0], acc[hd])
            prev[hd] = (None if i is None else
                        (i, tiles[hd], jnp.max(run_max[hd], axis=-1, keepdims=True)))


def _diff_attention(qk, v, lq1, lk1, lq2, lk2, g_out, batch, seq, lam_init):
    t = batch * seq
    width = DIFF_HEADS_PER_STEP * 2 * DIFF_D
    groups = DIFF_HEADS // DIFF_HEADS_PER_STEP
    vec64 = pl.BlockSpec((1, DIFF_D), lambda b, h: (0, 0))
    head = pl.BlockSpec((seq, width), lambda b, h: (b, h))
    return pl.pallas_call(
        functools.partial(_diff_kernel, lam_init=lam_init, seq=seq),
        grid=(batch, groups),
        in_specs=[head,
                  pl.BlockSpec((seq, width), lambda b, h: (b, groups + h)),
                  head,
                  vec64, vec64, vec64, vec64,
                  pl.BlockSpec((1, 2 * DIFF_D), lambda b, h: (0, 0))],
        out_specs=head,
        out_shape=jax.ShapeDtypeStruct((t, DIFF_HEADS * 2 * DIFF_D), BF16),
        compiler_params=_params(("parallel", "parallel")),
        name="diff_attention",
    )(qk, qk, v, lq1, lk1, lq2, lk2, g_out)


def _memkv_kernel(mem_ref, g_ref, w_ref, gk_ref, k_ref, v_ref):
    mn = (_rms(mem_ref[...]) * g_ref[...]).astype(BF16)
    kv = _dot(mn, w_ref[...])
    width = MEM_HEADS * MEM_D
    k_ref[...] = _group_norm_128(kv[:, :width], gk_ref[...], MEM_D).astype(BF16)
    v_ref[...] = kv[:, width:].astype(BF16)


def _memkv(mem2d, g_mem, w_kv, gk, batch, mlen):
    width = MEM_HEADS * MEM_D
    row = pl.BlockSpec((mlen, width), lambda b: (b, 0))
    vec = pl.BlockSpec((1, width), lambda b: (0, 0))
    return pl.pallas_call(
        _memkv_kernel,
        grid=(batch,),
        in_specs=[pl.BlockSpec((mlen, D_MODEL), lambda b: (b, 0)), vec,
                  _resident(w_kv.shape), vec],
        out_specs=[row, row],
        out_shape=[jax.ShapeDtypeStruct((batch * mlen, width), BF16)] * 2,
        compiler_params=_params(("parallel",)),
        name="memkv",
    )(mem2d, g_mem, w_kv, gk)


def _merge_kernel(x_ref, ret_ref, dif_ref, mo_ref, gt_ref, wr_ref, wd_ref, wm_ref, wo_ref, o_ref):
    gt = gt_ref[...].astype(F32)
    y = (gt[:, :D_MODEL] * _dot(ret_ref[...], wr_ref[...])
         + gt[:, D_MODEL:2 * D_MODEL] * _dot(dif_ref[...], wd_ref[...])
         + gt[:, 2 * D_MODEL:] * _dot(mo_ref[...], wm_ref[...]))
    o_ref[...] = x_ref[...] + _dot(y.astype(BF16), wo_ref[...])


def _merge(x, ret, dif, mo, gates, wr, wd, wm, wo):
    t = x.shape[0]
    row = lambda w: pl.BlockSpec((TM, w), lambda i: (i, 0))
    return pl.pallas_call(
        _merge_kernel,
        grid=(t // TM,),
        in_specs=[row(D_MODEL), row(ret.shape[1]), row(D_MODEL), row(D_MODEL), row(3 * D_MODEL),
                  _resident(wr.shape), _resident(wd.shape), _resident(wm.shape),
                  _resident(wo.shape)],
        out_specs=row(D_MODEL),
        out_shape=jax.ShapeDtypeStruct((t, D_MODEL), F32),
        compiler_params=_params(("parallel",)),
        name="merge",
    )(x, ret, dif, mo, gates, wr, wd, wm, wo)


def _retention_constants():
    h = np.arange(RET_HEADS, dtype=np.float64)
    log_g = np.log1p(-(2.0 ** (-5.0 - h)))
    idx = np.arange(RET_C, dtype=np.float64)
    dist = idx[:, None] - idx[None, :]
    dec = np.where(dist >= 0, np.exp(log_g[:, None, None] * np.maximum(dist, 0.0)), 0.0)
    qd = np.exp(log_g[:, None] * (idx + 1.0))[:, :, None]
    kd = np.exp(log_g[:, None] * (RET_C - 1.0 - idx))[:, :, None]
    return (jnp.asarray(dec, F32), jnp.asarray(qd, F32), jnp.asarray(kd, F32))


def kernel(x, mem, positions, g_ffn1, w_ffn1_in, w_ffn1_out, g_mix, w_in, g_diff_q, g_diff_k,
           lam_q1, lam_k1, lam_q2, lam_k2, g_diff_out, g_mem_q, g_mem_k, g_mem, w_mem_kv,
           w_br_ret, w_br_diff, w_br_mem, w_o, g_ffn2, w_ffn2_in, w_ffn2_out):
    batch, seq, _ = x.shape
    mlen = mem.shape[1]
    t = batch * seq
    depth = g_ffn1.shape[0]
    half = RET_DK // 2
    inv = jnp.asarray(ROPE_BASE ** (-np.arange(half, dtype=np.float64) / half), F32)[None, :]
    dec, qd, kd = _retention_constants()
    pos = positions.reshape(t, 1)
    mem2d = mem.reshape(batch * mlen, D_MODEL)
    xf = x.reshape(t, D_MODEL)
    bf = lambda w: w.astype(BF16)
    vec = lambda g: g.astype(F32)[None, :]

    for l in range(depth):
        lam_init = 0.8 - 0.6 * math.exp(-0.3 * l)
        x1, h = _ffn(xf, vec(g_ffn1[l]), bf(w_ffn1_in[l]), bf(w_ffn1_out[l]), vec(g_mix[l]))

        w = bf(w_in[l])
        rqk, rv, rg = _proj_ret(h, w, pos, inv)
        qk_gain = jnp.concatenate([jnp.tile(g_diff_q[l].astype(F32), 2 * DIFF_HEADS) * (DIFF_D ** -0.5 * LOG2E),
                                   jnp.tile(g_diff_k[l].astype(F32), 2 * DIFF_HEADS)])[None, :]
        mq_gain = (jnp.tile(g_mem_q[l].astype(F32), MEM_HEADS) * (MEM_D ** -0.5 * LOG2E))[None, :]
        mk, mv = _memkv(mem2d, vec(g_mem[l]), bf(w_mem_kv[l]),
                        jnp.tile(g_mem_k[l].astype(F32), MEM_HEADS)[None, :], batch, mlen)
        dqk, dv, mo, gates = _proj_mix(h, w, qk_gain, mq_gain, mk, mv, seq)

        ret = _retention(rqk, rv, rg, dec, qd, kd, batch, seq)
        go = (g_diff_out[l].astype(F32) * (1.0 - lam_init))[None, :]
        dif = _diff_attention(dqk, dv, vec(lam_q1[l]), vec(lam_k1[l]), vec(lam_q2[l]),
                              vec(lam_k2[l]), go, batch, seq, lam_init)

        x2 = _merge(x1, ret, dif, mo, gates, bf(w_br_ret[l]), bf(w_br_diff[l]),
                    bf(w_br_mem[l]), bf(w_o[l]))
        (xf,) = _ffn(x2, vec(g_ffn2[l]), bf(w_ffn2_in[l]), bf(w_ffn2_out[l]))
    return xf.reshape(batch, seq, D_MODEL)
```

```python
import functools
import math

import jax
import jax.numpy as jnp
import numpy as np
from jax import lax
from jax.experimental import pallas as pl
from jax.experimental.pallas import tpu as pltpu

F32 = jnp.float32
BF16 = jnp.bfloat16

D_MODEL = 1024
D_FF = 2816
FFN_RES = 0.5
EPS = 1e-6
NEG = -1e30
LOG2E = math.log2(math.e)
ROPE_BASE = 10000.0

RET_HEADS = 4
RET_DK = 256
RET_DV = 512
DIFF_HEADS = 8
DIFF_D = 64
MEM_HEADS = 4
MEM_D = 256

V7X_VMEM_LIMIT_BYTES = 56 * 1024 * 1024

TM = 512
TM_FFN = 1024
FFN_CHUNK = 256
RET_C = 256
RET_HEADS_PER_STEP = 2
DIFF_TQ = 256
DIFF_HEADS_PER_STEP = 2


def _params(sem):
    return pltpu.CompilerParams(dimension_semantics=sem,
                                vmem_limit_bytes=V7X_VMEM_LIMIT_BYTES)


def _resident(shape):
    nd = len(shape)
    return pl.BlockSpec(shape, lambda *_: (0,) * nd, pipeline_mode=pl.Buffered(1))


def _rms(x):
    return x * lax.rsqrt(jnp.mean(x * x, axis=-1, keepdims=True) + EPS)


def _after(x, *deps):
    folded = None
    for d in deps:
        t = jnp.sum(d.astype(F32), axis=0, keepdims=True)
        for c in range(t.shape[1] // 128):
            blk = t[:, c * 128:(c + 1) * 128]
            folded = blk if folded is None else folded + blk
    bits = lax.bitcast_convert_type(folded, jnp.uint32)
    zero = lax.shift_right_logical(lax.shift_right_logical(bits, jnp.uint32(16)), jnp.uint32(16))
    zero = zero.astype(F32).astype(x.dtype)
    zero = jnp.concatenate([zero] * (x.shape[1] // 128), axis=1)
    return jnp.concatenate([x[:16] + zero, x[16:]], axis=0)


def _dot(a, b):
    return jnp.dot(a, b, preferred_element_type=F32)


def _dot_nt(a, b):
    return lax.dot_general(a, b, (((1,), (1,)), ((), ())), preferred_element_type=F32)


def _dot_tn(a, b):
    return lax.dot_general(a, b, (((0,), (0,)), ((), ())), preferred_element_type=F32)


STAGE_BYTES = 1536 * 1024


def _stage_rows(width):
    return max(16, (STAGE_BYTES // (4 * width)) // 16 * 16)


def _weight_scratch(rows, width):
    return [pltpu.VMEM((rows, width), BF16),
            pltpu.VMEM((2, _stage_rows(width), width), F32),
            pltpu.SemaphoreType.DMA((2,))]


def _load_weight(w_hbm, col0, w_vmem, stage, sem):
    rows, width = w_vmem.shape
    step = stage.shape[1]
    starts = list(range(0, rows, step))

    def copy(c):
        n = min(step, rows - starts[c])
        return pltpu.make_async_copy(
            w_hbm.at[pl.ds(starts[c], n), pl.ds(col0, width)],
            stage.at[c % 2, pl.ds(0, n), :], sem.at[c % 2])

    copy(0).start()
    for c in range(len(starts)):
        if c + 1 < len(starts):
            copy(c + 1).start()
        copy(c).wait()
        n = min(step, rows - starts[c])
        w_vmem[starts[c]:starts[c] + n, :] = stage[c % 2, 0:n, :].astype(BF16)


_ANY = pl.BlockSpec(memory_space=pl.ANY)


def _ffn_kernel(*refs, emit_next):
    if emit_next:
        x_ref, g_ref, win_hbm, wout_hbm, gn_ref, o_ref, h_ref = refs[:7]
    else:
        x_ref, g_ref, win_hbm, wout_hbm, o_ref = refs[:5]
    win_ref, win_stage, win_sem, wout_ref, wout_stage, wout_sem = refs[-6:]

    @pl.when(pl.program_id(0) == 0)
    def _():
        _load_weight(win_hbm, 0, win_ref, win_stage, win_sem)
        _load_weight(wout_hbm, 0, wout_ref, wout_stage, wout_sem)

    x = x_ref[...]
    hb = (_rms(x) * g_ref[...]).astype(BF16)
    acc = jnp.zeros(x.shape, F32)
    for j in range(D_FF // FFN_CHUNK):
        lo = j * FFN_CHUNK
        a = _dot(hb, win_ref[:, lo:lo + FFN_CHUNK])
        b = _dot(hb, win_ref[:, D_FF + lo:D_FF + lo + FFN_CHUNK])
        act = (a * jax.nn.sigmoid(a) * b).astype(BF16)
        acc = acc + _dot(act, wout_ref[lo:lo + FFN_CHUNK, :])
    y = x + FFN_RES * acc
    o_ref[...] = y
    if emit_next:
        h_ref[...] = (_rms(y) * gn_ref[...]).astype(BF16)


def _ffn(x, g, w_in, w_out, g_next=None):
    t = x.shape[0]
    emit = g_next is not None
    row = pl.BlockSpec((TM_FFN, D_MODEL), lambda i: (i, 0))
    vec = pl.BlockSpec((1, D_MODEL), lambda i: (0, 0))
    in_specs = [row, vec, _ANY, _ANY]
    args = [x, g, w_in, w_out]
    out_shape = [jax.ShapeDtypeStruct((t, D_MODEL), F32)]
    out_specs = [row]
    if emit:
        in_specs.append(vec)
        args.append(g_next)
        out_shape.append(jax.ShapeDtypeStruct((t, D_MODEL), BF16))
        out_specs.append(row)
    return pl.pallas_call(
        functools.partial(_ffn_kernel, emit_next=emit),
        grid=(t // TM_FFN,),
        in_specs=in_specs,
        out_specs=out_specs,
        out_shape=out_shape,
        scratch_shapes=_weight_scratch(*w_in.shape) + _weight_scratch(*w_out.shape),
        compiler_params=_params(("arbitrary",)),
        name="ffn_next" if emit else "ffn",
    )(*args)


def _group_norm_128(y, gain, group):
    cols = y.shape[1]
    outs = []
    if group == 64:
        lane = lax.broadcasted_iota(jnp.int32, (1, 128), 1)
        first = lane < 64
        for c in range(cols // 128):
            blk = y[:, c * 128:(c + 1) * 128]
            sq = blk * blk
            s_lo = jnp.sum(jnp.where(first, sq, 0.0), axis=-1, keepdims=True)
            s_hi = jnp.sum(jnp.where(first, 0.0, sq), axis=-1, keepdims=True)
            r = jnp.where(first, lax.rsqrt(s_lo * (1.0 / 64) + EPS),
                          lax.rsqrt(s_hi * (1.0 / 64) + EPS))
            outs.append(blk * r)
    else:
        for c in range(cols // group):
            blk = y[:, c * group:(c + 1) * group]
            outs.append(_rms(blk))
    return jnp.concatenate(outs, axis=1) * gain


RET_QK_W = RET_HEADS * RET_DK
RET_V_W = RET_HEADS * RET_DV
PROJ_RET_W = 2 * RET_QK_W + 2 * RET_V_W
DIFF_W = DIFF_HEADS * 2 * DIFF_D
MEM_W = MEM_HEADS * MEM_D
PROJ_MIX_W = 2 * DIFF_W + DIFF_W + MEM_W + 3 * D_MODEL
PROJ_CHUNK = 512

def _proj_ret_kernel(h_ref, w_hbm, pos_ref, inv_ref, qk_ref, v_ref, g_ref, w_ref, w_stage, w_sem):
    @pl.when(pl.program_id(0) == 0)
    def _():
        _load_weight(w_hbm, 0, w_ref, w_stage, w_sem)

    h = h_ref[...]
    tm = h.shape[0]
    n_piece = 8
    rows = tm // n_piece
    cos_p, sin_p = [], []
    for r in range(n_piece):
        ang = pos_ref[r * rows:(r + 1) * rows, :].astype(F32) * inv_ref[...]
        cos_p.append(jnp.cos(ang))
        sin_p.append(jnp.sin(ang))
    cos = jnp.concatenate(cos_p, axis=0)
    sin = jnp.concatenate(sin_p, axis=0)
    half = RET_DK // 2

    def rot(lhs, hd):
        lo = hd * RET_DK
        y = _dot(lhs, w_ref[:, lo:lo + RET_DK])
        x1 = y[:, :half]
        x2 = y[:, half:]
        r1 = x1 * cos - x2 * sin
        r2 = x1 * sin + x2 * cos
        if hd >= RET_HEADS:
            r1 = r1 * (RET_DK ** -0.5)
            r2 = r2 * (RET_DK ** -0.5)
        r1 = r1.astype(BF16)
        r2 = r2.astype(BF16)
        qk_ref[:, lo:lo + half] = r1
        qk_ref[:, lo + half:lo + RET_DK] = r2
        return r1, r2

    def val(lhs, c):
        lo = c * PROJ_CHUNK
        v_ref[:, lo:lo + PROJ_CHUNK] = _dot(
            lhs, w_ref[:, 2 * RET_QK_W + lo:2 * RET_QK_W + lo + PROJ_CHUNK]).astype(BF16)

    def gate(lhs, c):
        lo = c * PROJ_CHUNK
        wlo = 2 * RET_QK_W + RET_V_W + lo
        y = _dot(lhs, w_ref[:, wlo:wlo + PROJ_CHUNK])
        g_ref[:, lo:lo + PROJ_CHUNK] = (y * jax.nn.sigmoid(y)).astype(BF16)

    plain = [(val, 0), (gate, 0), (val, 1), (gate, 1), (val, 2), (gate, 2), (val, 3), (gate, 3)]
    for r, (fn, idx) in enumerate(plain):
        fn(h if r == 0 else _after(h, cos_p[r - 1], sin_p[r - 1]), idx)
    for hd in range(2 * RET_HEADS):
        rot(h, hd)


def _proj_ret(h, w, pos, inv):
    t = h.shape[0]
    row = lambda width: pl.BlockSpec((TM, width), lambda i: (i, 0))
    return pl.pallas_call(
        _proj_ret_kernel,
        grid=(t // TM,),
        in_specs=[row(D_MODEL), _ANY, row(1), pl.BlockSpec((1, 128), lambda i: (0, 0))],
        out_specs=[row(2 * RET_QK_W), row(RET_V_W), row(RET_V_W)],
        out_shape=[jax.ShapeDtypeStruct((t, 2 * RET_QK_W), BF16),
                   jax.ShapeDtypeStruct((t, RET_V_W), BF16),
                   jax.ShapeDtypeStruct((t, RET_V_W), BF16)],
        scratch_shapes=_weight_scratch(D_MODEL, PROJ_RET_W),
        compiler_params=_params(("arbitrary",)),
        name="proj_ret",
    )(h, w, pos, inv)


def _proj_mix_kernel(h_ref, w_hbm, gqk_ref, gmq_ref, mk_ref, mv_ref,
                     qk_ref, v_ref, mo_ref, gt_ref, w_ref, w_stage, w_sem):
    @pl.when(pl.program_id(0) == 0)
    def _():
        _load_weight(w_hbm, PROJ_RET_W, w_ref, w_stage, w_sem)

    h = h_ref[...]
    off_qk = 0
    off_v = off_qk + 2 * DIFF_W
    off_mq = off_v + DIFF_W
    off_gt = off_mq + MEM_W

    mem_q, mem_p = {}, {}

    def qk_norm(lhs, c):
        lo = c * 256
        y = _dot(lhs, w_ref[:, off_qk + lo:off_qk + lo + 256])
        out = _group_norm_128(y, gqk_ref[:, lo:lo + 256], DIFF_D).astype(BF16)
        qk_ref[:, lo:lo + 256] = out
        return out

    def val(lhs, c):
        lo = c * PROJ_CHUNK
        out = _dot(lhs, w_ref[:, off_v + lo:off_v + lo + PROJ_CHUNK]).astype(BF16)
        v_ref[:, lo:lo + PROJ_CHUNK] = out
        return out

    def gate(lhs, c):
        lo = c * PROJ_CHUNK
        y = _dot(lhs, w_ref[:, off_gt + lo:off_gt + lo + PROJ_CHUNK])
        out = jax.nn.sigmoid(y).astype(BF16)
        gt_ref[:, lo:lo + PROJ_CHUNK] = out
        return out

    def mem_query(lhs, hd):
        sl = slice(hd * MEM_D, (hd + 1) * MEM_D)
        y = _dot(lhs, w_ref[:, off_mq + hd * MEM_D:off_mq + (hd + 1) * MEM_D])
        mem_q[hd] = (_rms(y) * gmq_ref[:, sl]).astype(BF16)
        return mem_q[hd]

    def mem_scores(lhs, hd):
        sl = slice(hd * MEM_D, (hd + 1) * MEM_D)
        s = _dot_nt(mem_q[hd], mk_ref[:, sl])
        p = jnp.exp2(s - jnp.max(s, axis=-1, keepdims=True))
        mem_p[hd] = (p.astype(BF16), jnp.sum(p, axis=-1, keepdims=True))
        return mem_p[hd][0]

    def mem_values(lhs, hd):
        sl = slice(hd * MEM_D, (hd + 1) * MEM_D)
        p, l = mem_p[hd]
        out = (_dot(p, mv_ref[:, sl]) / l).astype(BF16)
        mo_ref[:, sl] = out
        return out

    order = [(gate, 0), (mem_query, 0), (qk_norm, 0), (mem_scores, 0), (qk_norm, 1),
             (mem_values, 0), (gate, 1), (mem_query, 1), (qk_norm, 2), (mem_scores, 1),
             (qk_norm, 3), (mem_values, 1), (val, 0), (mem_query, 2), (qk_norm, 4),
             (mem_scores, 2), (qk_norm, 5), (mem_values, 2), (gate, 2), (mem_query, 3),
             (qk_norm, 6), (mem_scores, 3), (qk_norm, 7), (mem_values, 3), (val, 1),
             (gate, 3), (gate, 4), (gate, 5)]
    for fn, idx in order:
        fn(h, idx)


def _proj_mix(h, w, gqk, gmq, mk, mv, seq):
    t = h.shape[0]
    mlen = mk.shape[0] // (t // seq)
    tiles_per_batch = seq // TM
    row = lambda width: pl.BlockSpec((TM, width), lambda i: (i, 0))
    vec = lambda width: pl.BlockSpec((1, width), lambda i: (0, 0))
    kv = pl.BlockSpec((mlen, MEM_W), lambda i: (i // tiles_per_batch, 0))
    return pl.pallas_call(
        _proj_mix_kernel,
        grid=(t // TM,),
        in_specs=[row(D_MODEL), _ANY, vec(2 * DIFF_W), vec(MEM_W), kv, kv],
        out_specs=[row(2 * DIFF_W), row(DIFF_W), row(MEM_W), row(3 * D_MODEL)],
        out_shape=[jax.ShapeDtypeStruct((t, 2 * DIFF_W), BF16),
                   jax.ShapeDtypeStruct((t, DIFF_W), BF16),
                   jax.ShapeDtypeStruct((t, MEM_W), BF16),
                   jax.ShapeDtypeStruct((t, 3 * D_MODEL), BF16)],
        scratch_shapes=_weight_scratch(D_MODEL, PROJ_MIX_W),
        compiler_params=_params(("arbitrary",)),
        name="proj_mix",
    )(h, w, gqk, gmq, mk, mv)


def _ret_kernel(q_ref, k_ref, v_ref, g_ref, dec_ref, qd_ref, kd_ref, o_ref, *, seq):
    nc = seq // RET_C
    heads = RET_HEADS_PER_STEP
    st = [None] * heads
    for c in range(nc):
        lo = c * RET_C
        for hd in range(heads):
            qd = qd_ref[hd]
            q = q_ref[lo:lo + RET_C, hd * RET_DK:(hd + 1) * RET_DK]
            k = k_ref[lo:lo + RET_C, hd * RET_DK:(hd + 1) * RET_DK]
            v = v_ref[lo:lo + RET_C, hd * RET_DV:(hd + 1) * RET_DV]
            s = _dot_nt(q, k) * dec_ref[hd]
            out = _dot(s.astype(BF16), v)
            if st[hd] is not None:
                out = out + _dot(q, st[hd].astype(BF16)) * qd
            if c + 1 < nc:
                upd = _dot_tn((k.astype(F32) * kd_ref[hd]).astype(BF16), v)
                chunk_decay = qd[RET_C - 1:RET_C, :]
                st[hd] = upd if st[hd] is None else chunk_decay * st[hd] + upd
            gate = g_ref[lo:lo + RET_C, hd * RET_DV:(hd + 1) * RET_DV].astype(F32)
            o_ref[lo:lo + RET_C, hd * RET_DV:(hd + 1) * RET_DV] = (_rms(out) * gate).astype(BF16)


def _retention(qk, v, gate, dec, qd, kd, batch, seq):
    t = batch * seq
    hp = RET_HEADS_PER_STEP
    groups = RET_HEADS // hp
    head = lambda b, h: (b, h)
    const = lambda b, h: (h, 0, 0)
    return pl.pallas_call(
        functools.partial(_ret_kernel, seq=seq),
        grid=(batch, groups),
        in_specs=[pl.BlockSpec((seq, hp * RET_DK), head),
                  pl.BlockSpec((seq, hp * RET_DK), lambda b, h: (b, groups + h)),
                  pl.BlockSpec((seq, hp * RET_DV), head),
                  pl.BlockSpec((seq, hp * RET_DV), head),
                  pl.BlockSpec((hp, RET_C, RET_C), const),
                  pl.BlockSpec((hp, RET_C, 1), const),
                  pl.BlockSpec((hp, RET_C, 1), const)],
        out_specs=pl.BlockSpec((seq, hp * RET_DV), head),
        out_shape=jax.ShapeDtypeStruct((t, RET_V_W), BF16),
        compiler_params=_params(("parallel", "parallel")),
        name="retention",
    )(qk, qk, v, gate, dec, qd, kd)


def _diff_kernel(q_ref, k_ref, v_ref, lq1_ref, lk1_ref, lq2_ref, lk2_ref, go_ref, o_ref,
                 *, lam_init, seq):
    tq = DIFF_TQ
    lane = lax.broadcasted_iota(jnp.int32, (1, 2 * DIFF_D), 1)
    first_map = lane < DIFF_D
    r = lax.broadcasted_iota(jnp.int32, (tq, tq), 0)
    cidx = lax.broadcasted_iota(jnp.int32, (tq, tq), 1)
    keep = cidx <= r
    keep = jnp.concatenate([keep, keep], axis=0)
    lam = (jnp.exp(jnp.sum(lq1_ref[...] * lk1_ref[...], axis=-1, keepdims=True))
           - jnp.exp(jnp.sum(lq2_ref[...] * lk2_ref[...], axis=-1, keepdims=True))
           + lam_init)
    go = go_ref[...]
    nq = seq // tq
    hw = 2 * DIFF_D
    heads = DIFF_HEADS_PER_STEP
    ones = jnp.ones((seq, hw), BF16)
    v_ext = [jnp.concatenate([v_ref[:, hd * hw:(hd + 1) * hw], ones], axis=1)
             for hd in range(heads)]

    def finish(hd, i, o_ext):
        r_ = o_ext[:, :hw] / o_ext[:, hw:]
        d = r_[:tq] - lam * r_[tq:]
        o_ref[i * tq:(i + 1) * tq, hd * hw:(hd + 1) * hw] = (_rms(d) * go).astype(BF16)

    prev = [None] * heads
    for i in list(range(nq)) + [None]:
        tiles = [[] for _ in range(heads)]
        run_max = [None] * heads
        acc = [None] * heads
        qs = [None] * heads
        if i is not None:
            for hd in range(heads):
                q = q_ref[i * tq:(i + 1) * tq, hd * hw:(hd + 1) * hw]
                zero = jnp.zeros_like(q)
                qs[hd] = jnp.concatenate([jnp.where(first_map, q, zero),
                                          jnp.where(first_map, zero, q)], axis=0)
        n_cur = 0 if i is None else i + 1
        n_prev = 0 if prev[0] is None else len(prev[0][1])
        for j in range(max(n_cur, n_prev)):
            for hd in range(heads):
                if j < n_cur:
                    s = _dot_nt(qs[hd], k_ref[j * tq:(j + 1) * tq, hd * hw:(hd + 1) * hw])
                    if j == i:
                        s = jnp.where(keep, s, NEG)
                    tiles[hd].append(s)
                    run_max[hd] = s if run_max[hd] is None else jnp.maximum(run_max[hd], s)
                if j < n_prev:
                    p = jnp.exp2(prev[hd][1][j] - prev[hd][2]).astype(BF16)
                    pv = _dot(p, v_ext[hd][j * tq:(j + 1) * tq, :])
                    acc[hd] = pv if acc[hd] is None else acc[hd] + pv
        for hd in range(heads):
            if prev[hd] is not None:
                finish(hd, prev[hd][0], acc[hd])
            prev[hd] = (None if i is None else
                        (i, tiles[hd], jnp.max(run_max[hd], axis=-1, keepdims=True)))


def _diff_attention(qk, v, lq1, lk1, lq2, lk2, g_out, batch, seq, lam_init):
    t = batch * seq
    width = DIFF_HEADS_PER_STEP * 2 * DIFF_D
    groups = DIFF_HEADS // DIFF_HEADS_PER_STEP
    vec64 = pl.BlockSpec((1, DIFF_D), lambda b, h: (0, 0))
    head = pl.BlockSpec((seq, width), lambda b, h: (b, h))
    return pl.pallas_call(
        functools.partial(_diff_kernel, lam_init=lam_init, seq=seq),
        grid=(batch, groups),
        in_specs=[head,
                  pl.BlockSpec((seq, width), lambda b, h: (b, groups + h)),
                  head,
                  vec64, vec64, vec64, vec64,
                  pl.BlockSpec((1, 2 * DIFF_D), lambda b, h: (0, 0))],
        out_specs=head,
        out_shape=jax.ShapeDtypeStruct((t, DIFF_HEADS * 2 * DIFF_D), BF16),
        compiler_params=_params(("parallel", "parallel")),
        name="diff_attention",
    )(qk, qk, v, lq1, lk1, lq2, lk2, g_out)


def _memkv_kernel(mem_ref, g_ref, w_ref, gk_ref, k_ref, v_ref):
    mn = (_rms(mem_ref[...]) * g_ref[...]).astype(BF16)
    kv = _dot(mn, w_ref[...])
    width = MEM_HEADS * MEM_D
    k_ref[...] = _group_norm_128(kv[:, :width], gk_ref[...], MEM_D).astype(BF16)
    v_ref[...] = kv[:, width:].astype(BF16)


def _memkv(mem2d, g_mem, w_kv, gk, batch, mlen):
    width = MEM_HEADS * MEM_D
    row = pl.BlockSpec((mlen, width), lambda b: (b, 0))
    vec = pl.BlockSpec((1, width), lambda b: (0, 0))
    return pl.pallas_call(
        _memkv_kernel,
        grid=(batch,),
        in_specs=[pl.BlockSpec((mlen, D_MODEL), lambda b: (b, 0)), vec,
                  _resident(w_kv.shape), vec],
        out_specs=[row, row],
        out_shape=[jax.ShapeDtypeStruct((batch * mlen, width), BF16)] * 2,
        compiler_params=_params(("parallel",)),
        name="memkv",
    )(mem2d, g_mem, w_kv, gk)


def _merge_kernel(x_ref, ret_ref, dif_ref, mo_ref, gt_ref, wr_hbm, wd_hbm, wm_hbm, wo_hbm, o_ref,
                  wr_ref, wd_ref, wm_ref, wo_ref, stage, sem):
    @pl.when(pl.program_id(0) == 0)
    def _():
        for src, dst in ((wr_hbm, wr_ref), (wd_hbm, wd_ref), (wm_hbm, wm_ref), (wo_hbm, wo_ref)):
            _load_weight(src, 0, dst, stage, sem)

    gt = gt_ref[...].astype(F32)
    y = (gt[:, :D_MODEL] * _dot(ret_ref[...], wr_ref[...])
         + gt[:, D_MODEL:2 * D_MODEL] * _dot(dif_ref[...], wd_ref[...])
         + gt[:, 2 * D_MODEL:] * _dot(mo_ref[...], wm_ref[...]))
    o_ref[...] = x_ref[...] + _dot(y.astype(BF16), wo_ref[...])


def _merge(x, ret, dif, mo, gates, wr, wd, wm, wo):
    t = x.shape[0]
    row = lambda w: pl.BlockSpec((TM, w), lambda i: (i, 0))
    return pl.pallas_call(
        _merge_kernel,
        grid=(t // TM,),
        in_specs=[row(D_MODEL), row(ret.shape[1]), row(D_MODEL), row(D_MODEL), row(3 * D_MODEL),
                  _ANY, _ANY, _ANY, _ANY],
        out_specs=row(D_MODEL),
        out_shape=jax.ShapeDtypeStruct((t, D_MODEL), F32),
        scratch_shapes=[pltpu.VMEM(wr.shape, BF16), pltpu.VMEM(wd.shape, BF16),
                        pltpu.VMEM(wm.shape, BF16), pltpu.VMEM(wo.shape, BF16)]
        + _weight_scratch(D_MODEL, D_MODEL)[1:],
        compiler_params=_params(("arbitrary",)),
        name="merge",
    )(x, ret, dif, mo, gates, wr, wd, wm, wo)


def _retention_constants():
    h = np.arange(RET_HEADS, dtype=np.float64)
    log_g = np.log1p(-(2.0 ** (-5.0 - h)))
    idx = np.arange(RET_C, dtype=np.float64)
    dist = idx[:, None] - idx[None, :]
    dec = np.where(dist >= 0, np.exp(log_g[:, None, None] * np.maximum(dist, 0.0)), 0.0)
    qd = np.exp(log_g[:, None] * (idx + 1.0))[:, :, None]
    kd = np.exp(log_g[:, None] * (RET_C - 1.0 - idx))[:, :, None]
    return (jnp.asarray(dec, F32), jnp.asarray(qd, F32), jnp.asarray(kd, F32))


def kernel(x, mem, positions, g_ffn1, w_ffn1_in, w_ffn1_out, g_mix, w_in, g_diff_q, g_diff_k,
           lam_q1, lam_k1, lam_q2, lam_k2, g_diff_out, g_mem_q, g_mem_k, g_mem, w_mem_kv,
           w_br_ret, w_br_diff, w_br_mem, w_o, g_ffn2, w_ffn2_in, w_ffn2_out):
    batch, seq, _ = x.shape
    mlen = mem.shape[1]
    t = batch * seq
    depth = g_ffn1.shape[0]
    half = RET_DK // 2
    inv = jnp.asarray(ROPE_BASE ** (-np.arange(half, dtype=np.float64) / half), F32)[None, :]
    dec, qd, kd = _retention_constants()
    pos = positions.reshape(t, 1)
    mem2d = mem.reshape(batch * mlen, D_MODEL)
    xf = x.reshape(t, D_MODEL)
    bf = lambda w: w.astype(BF16)
    f32 = lambda w: w.astype(F32)
    vec = lambda g: g.astype(F32)[None, :]

    for l in range(depth):
        lam_init = 0.8 - 0.6 * math.exp(-0.3 * l)
        x1, h = _ffn(xf, vec(g_ffn1[l]), f32(w_ffn1_in[l]), f32(w_ffn1_out[l]), vec(g_mix[l]))

        w = f32(w_in[l])
        rqk, rv, rg = _proj_ret(h, w, pos, inv)
        qk_gain = jnp.concatenate([jnp.tile(g_diff_q[l].astype(F32), 2 * DIFF_HEADS) * (DIFF_D ** -0.5 * LOG2E),
                                   jnp.tile(g_diff_k[l].astype(F32), 2 * DIFF_HEADS)])[None, :]
        mq_gain = (jnp.tile(g_mem_q[l].astype(F32), MEM_HEADS) * (MEM_D ** -0.5 * LOG2E))[None, :]
        mk, mv = _memkv(mem2d, vec(g_mem[l]), bf(w_mem_kv[l]),
                        jnp.tile(g_mem_k[l].astype(F32), MEM_HEADS)[None, :], batch, mlen)
        dqk, dv, mo, gates = _proj_mix(h, w, qk_gain, mq_gain, mk, mv, seq)

        ret = _retention(rqk, rv, rg, dec, qd, kd, batch, seq)
        go = (g_diff_out[l].astype(F32) * (1.0 - lam_init))[None, :]
        dif = _diff_attention(dqk, dv, vec(lam_q1[l]), vec(lam_k1[l]), vec(lam_q2[l]),
                              vec(lam_k2[l]), go, batch, seq, lam_init)

        x2 = _merge(x1, ret, dif, mo, gates, f32(w_br_ret[l]), f32(w_br_diff[l]),
                    f32(w_br_mem[l]), f32(w_o[l]))
        (xf,) = _ffn(x2, vec(g_ffn2[l]), f32(w_ffn2_in[l]), f32(w_ffn2_out[l]))
    return xf.reshape(batch, seq, D_MODEL)
```

```python
import functools
import math

import jax
import jax.numpy as jnp
import numpy as np
from jax import lax
from jax.experimental import pallas as pl
from jax.experimental.pallas import tpu as pltpu

F32 = jnp.float32
BF16 = jnp.bfloat16

D_MODEL = 1024
D_FF = 2816
FFN_RES = 0.5
EPS = 1e-6
NEG = -1e30
LOG2E = math.log2(math.e)
ROPE_BASE = 10000.0

RET_HEADS = 4
RET_DK = 256
RET_DV = 512
DIFF_HEADS = 8
DIFF_D = 64
MEM_HEADS = 4
MEM_D = 256

V7X_VMEM_LIMIT_BYTES = 56 * 1024 * 1024

TM = 512
TM_FFN = 1024
FFN_CHUNK = 256
RET_C = 256
RET_HEADS_PER_STEP = 2
DIFF_TQ = 256
DIFF_HEADS_PER_STEP = 2


def _params(sem):
    return pltpu.CompilerParams(dimension_semantics=sem,
                                vmem_limit_bytes=V7X_VMEM_LIMIT_BYTES)


def _resident(shape):
    nd = len(shape)
    return pl.BlockSpec(shape, lambda *_: (0,) * nd, pipeline_mode=pl.Buffered(1))


def _rms(x):
    return x * lax.rsqrt(jnp.mean(x * x, axis=-1, keepdims=True) + EPS)


def _after(x, *deps):
    folded = None
    for d in deps:
        t = jnp.sum(d.astype(F32), axis=0, keepdims=True)
        for c in range(t.shape[1] // 128):
            blk = t[:, c * 128:(c + 1) * 128]
            folded = blk if folded is None else folded + blk
    bits = lax.bitcast_convert_type(folded, jnp.uint32)
    zero = lax.shift_right_logical(lax.shift_right_logical(bits, jnp.uint32(16)), jnp.uint32(16))
    zero = zero.astype(F32).astype(x.dtype)
    zero = jnp.concatenate([zero] * (x.shape[1] // 128), axis=1)
    return jnp.concatenate([x[:16] + zero, x[16:]], axis=0)


def _dot(a, b):
    return jnp.dot(a, b, preferred_element_type=F32)


def _dot_nt(a, b):
    return lax.dot_general(a, b, (((1,), (1,)), ((), ())), preferred_element_type=F32)


def _dot_tn(a, b):
    return lax.dot_general(a, b, (((0,), (0,)), ((), ())), preferred_element_type=F32)


STAGE_BYTES = 1024 * 1024
STAGE_SLOTS = 4


def _stage_rows(width):
    return max(16, (STAGE_BYTES // (4 * width)) // 16 * 16)


def _weight_scratch(rows, width):
    return [pltpu.VMEM((rows, width), BF16),
            pltpu.VMEM((STAGE_SLOTS, _stage_rows(width), width), F32),
            pltpu.SemaphoreType.DMA((STAGE_SLOTS,))]


def _load_weight(w_hbm, col0, w_vmem, stage, sem):
    rows, width = w_vmem.shape
    slots, step = stage.shape[0], stage.shape[1]
    starts = list(range(0, rows, step))

    def copy(c):
        n = min(step, rows - starts[c])
        return pltpu.make_async_copy(
            w_hbm.at[pl.ds(starts[c], n), pl.ds(col0, width)],
            stage.at[c % slots, pl.ds(0, n), :], sem.at[c % slots])

    for c in range(min(slots - 1, len(starts))):
        copy(c).start()
    for c in range(len(starts)):
        if c + slots - 1 < len(starts):
            copy(c + slots - 1).start()
        copy(c).wait()
        n = min(step, rows - starts[c])
        w_vmem[starts[c]:starts[c] + n, :] = stage[c % slots, 0:n, :].astype(BF16)


_ANY = pl.BlockSpec(memory_space=pl.ANY)


def _ffn_kernel(*refs, emit_next):
    if emit_next:
        x_ref, g_ref, win_hbm, wout_hbm, gn_ref, o_ref, h_ref = refs[:7]
    else:
        x_ref, g_ref, win_hbm, wout_hbm, o_ref = refs[:5]
    win_ref, win_stage, win_sem, wout_ref, wout_stage, wout_sem = refs[-6:]

    @pl.when(pl.program_id(0) == 0)
    def _():
        _load_weight(win_hbm, 0, win_ref, win_stage, win_sem)
        _load_weight(wout_hbm, 0, wout_ref, wout_stage, wout_sem)

    x = x_ref[...]
    hb = (_rms(x) * g_ref[...]).astype(BF16)
    acc = jnp.zeros(x.shape, F32)
    for j in range(D_FF // FFN_CHUNK):
        lo = j * FFN_CHUNK
        a = _dot(hb, win_ref[:, lo:lo + FFN_CHUNK])
        b = _dot(hb, win_ref[:, D_FF + lo:D_FF + lo + FFN_CHUNK])
        act = (a * jax.nn.sigmoid(a) * b).astype(BF16)
        acc = acc + _dot(act, wout_ref[lo:lo + FFN_CHUNK, :])
    y = x + FFN_RES * acc
    o_ref[...] = y
    if emit_next:
        h_ref[...] = (_rms(y) * gn_ref[...]).astype(BF16)


def _ffn(x, g, w_in, w_out, g_next=None):
    t = x.shape[0]
    emit = g_next is not None
    row = pl.BlockSpec((TM_FFN, D_MODEL), lambda i: (i, 0))
    vec = pl.BlockSpec((1, D_MODEL), lambda i: (0, 0))
    in_specs = [row, vec, _ANY, _ANY]
    args = [x, g, w_in, w_out]
    out_shape = [jax.ShapeDtypeStruct((t, D_MODEL), F32)]
    out_specs = [row]
    if emit:
        in_specs.append(vec)
        args.append(g_next)
        out_shape.append(jax.ShapeDtypeStruct((t, D_MODEL), BF16))
        out_specs.append(row)
    return pl.pallas_call(
        functools.partial(_ffn_kernel, emit_next=emit),
        grid=(t // TM_FFN,),
        in_specs=in_specs,
        out_specs=out_specs,
        out_shape=out_shape,
        scratch_shapes=_weight_scratch(*w_in.shape) + _weight_scratch(*w_out.shape),
        compiler_params=_params(("arbitrary",)),
        name="ffn_next" if emit else "ffn",
    )(*args)


def _group_norm_128(y, gain, group):
    cols = y.shape[1]
    outs = []
    if group == 64:
        lane = lax.broadcasted_iota(jnp.int32, (1, 128), 1)
        first = lane < 64
        for c in range(cols // 128):
            blk = y[:, c * 128:(c + 1) * 128]
            sq = blk * blk
            s_lo = jnp.sum(jnp.where(first, sq, 0.0), axis=-1, keepdims=True)
            s_hi = jnp.sum(jnp.where(first, 0.0, sq), axis=-1, keepdims=True)
            r = jnp.where(first, lax.rsqrt(s_lo * (1.0 / 64) + EPS),
                          lax.rsqrt(s_hi * (1.0 / 64) + EPS))
            outs.append(blk * r)
    else:
        for c in range(cols // group):
            blk = y[:, c * group:(c + 1) * group]
            outs.append(_rms(blk))
    return jnp.concatenate(outs, axis=1) * gain


RET_QK_W = RET_HEADS * RET_DK
RET_V_W = RET_HEADS * RET_DV
PROJ_RET_W = 2 * RET_QK_W + 2 * RET_V_W
DIFF_W = DIFF_HEADS * 2 * DIFF_D
MEM_W = MEM_HEADS * MEM_D
PROJ_MIX_W = 2 * DIFF_W + DIFF_W + MEM_W + 3 * D_MODEL
PROJ_CHUNK = 512

def _proj_ret_kernel(h_ref, w_hbm, pos_ref, inv_ref, qk_ref, v_ref, g_ref, w_ref, w_stage, w_sem):
    @pl.when(pl.program_id(0) == 0)
    def _():
        _load_weight(w_hbm, 0, w_ref, w_stage, w_sem)

    h = h_ref[...]
    tm = h.shape[0]
    n_piece = 8
    rows = tm // n_piece
    cos_p, sin_p = [], []
    for r in range(n_piece):
        ang = pos_ref[r * rows:(r + 1) * rows, :].astype(F32) * inv_ref[...]
        cos_p.append(jnp.cos(ang))
        sin_p.append(jnp.sin(ang))
    cos = jnp.concatenate(cos_p, axis=0)
    sin = jnp.concatenate(sin_p, axis=0)
    half = RET_DK // 2

    def rot(lhs, hd):
        lo = hd * RET_DK
        y = _dot(lhs, w_ref[:, lo:lo + RET_DK])
        x1 = y[:, :half]
        x2 = y[:, half:]
        r1 = x1 * cos - x2 * sin
        r2 = x1 * sin + x2 * cos
        if hd >= RET_HEADS:
            r1 = r1 * (RET_DK ** -0.5)
            r2 = r2 * (RET_DK ** -0.5)
        r1 = r1.astype(BF16)
        r2 = r2.astype(BF16)
        qk_ref[:, lo:lo + half] = r1
        qk_ref[:, lo + half:lo + RET_DK] = r2
        return r1, r2

    def val(lhs, c):
        lo = c * PROJ_CHUNK
        v_ref[:, lo:lo + PROJ_CHUNK] = _dot(
            lhs, w_ref[:, 2 * RET_QK_W + lo:2 * RET_QK_W + lo + PROJ_CHUNK]).astype(BF16)

    def gate(lhs, c):
        lo = c * PROJ_CHUNK
        wlo = 2 * RET_QK_W + RET_V_W + lo
        y = _dot(lhs, w_ref[:, wlo:wlo + PROJ_CHUNK])
        g_ref[:, lo:lo + PROJ_CHUNK] = (y * jax.nn.sigmoid(y)).astype(BF16)

    plain = [(val, 0), (gate, 0), (val, 1), (gate, 1), (val, 2), (gate, 2), (val, 3), (gate, 3)]
    for r, (fn, idx) in enumerate(plain):
        fn(h if r == 0 else _after(h, cos_p[r - 1], sin_p[r - 1]), idx)
    for hd in range(2 * RET_HEADS):
        rot(h, hd)


def _proj_ret(h, w, pos, inv):
    t = h.shape[0]
    row = lambda width: pl.BlockSpec((TM, width), lambda i: (i, 0))
    return pl.pallas_call(
        _proj_ret_kernel,
        grid=(t // TM,),
        in_specs=[row(D_MODEL), _ANY, row(1), pl.BlockSpec((1, 128), lambda i: (0, 0))],
        out_specs=[row(2 * RET_QK_W), row(RET_V_W), row(RET_V_W)],
        out_shape=[jax.ShapeDtypeStruct((t, 2 * RET_QK_W), BF16),
                   jax.ShapeDtypeStruct((t, RET_V_W), BF16),
                   jax.ShapeDtypeStruct((t, RET_V_W), BF16)],
        scratch_shapes=_weight_scratch(D_MODEL, PROJ_RET_W),
        compiler_params=_params(("arbitrary",)),
        name="proj_ret",
    )(h, w, pos, inv)


def _proj_mix_kernel(h_ref, w_hbm, gqk_ref, gmq_ref, mk_ref, mv_ref,
                     qk_ref, v_ref, mo_ref, gt_ref, w_ref, w_stage, w_sem):
    @pl.when(pl.program_id(0) == 0)
    def _():
        _load_weight(w_hbm, PROJ_RET_W, w_ref, w_stage, w_sem)

    h = h_ref[...]
    off_qk = 0
    off_v = off_qk + 2 * DIFF_W
    off_mq = off_v + DIFF_W
    off_gt = off_mq + MEM_W

    mem_q, mem_p = {}, {}

    def qk_norm(lhs, c):
        lo = c * 256
        y = _dot(lhs, w_ref[:, off_qk + lo:off_qk + lo + 256])
        out = _group_norm_128(y, gqk_ref[:, lo:lo + 256], DIFF_D).astype(BF16)
        qk_ref[:, lo:lo + 256] = out
        return out

    def val(lhs, c):
        lo = c * PROJ_CHUNK
        out = _dot(lhs, w_ref[:, off_v + lo:off_v + lo + PROJ_CHUNK]).astype(BF16)
        v_ref[:, lo:lo + PROJ_CHUNK] = out
        return out

    def gate(lhs, c):
        lo = c * PROJ_CHUNK
        y = _dot(lhs, w_ref[:, off_gt + lo:off_gt + lo + PROJ_CHUNK])
        out = jax.nn.sigmoid(y).astype(BF16)
        gt_ref[:, lo:lo + PROJ_CHUNK] = out
        return out

    def mem_query(lhs, hd):
        sl = slice(hd * MEM_D, (hd + 1) * MEM_D)
        y = _dot(lhs, w_ref[:, off_mq + hd * MEM_D:off_mq + (hd + 1) * MEM_D])
        mem_q[hd] = (_rms(y) * gmq_ref[:, sl]).astype(BF16)
        return mem_q[hd]

    def mem_scores(lhs, hd):
        sl = slice(hd * MEM_D, (hd + 1) * MEM_D)
        s = _dot_nt(mem_q[hd], mk_ref[:, sl])
        p = jnp.exp2(s - jnp.max(s, axis=-1, keepdims=True))
        mem_p[hd] = (p.astype(BF16), jnp.sum(p, axis=-1, keepdims=True))
        return mem_p[hd][0]

    def mem_values(lhs, hd):
        sl = slice(hd * MEM_D, (hd + 1) * MEM_D)
        p, l = mem_p[hd]
        out = (_dot(p, mv_ref[:, sl]) / l).astype(BF16)
        mo_ref[:, sl] = out
        return out

    order = [(gate, 0), (mem_query, 0), (qk_norm, 0), (mem_scores, 0), (qk_norm, 1),
             (mem_values, 0), (gate, 1), (mem_query, 1), (qk_norm, 2), (mem_scores, 1),
             (qk_norm, 3), (mem_values, 1), (val, 0), (mem_query, 2), (qk_norm, 4),
             (mem_scores, 2), (qk_norm, 5), (mem_values, 2), (gate, 2), (mem_query, 3),
             (qk_norm, 6), (mem_scores, 3), (qk_norm, 7), (mem_values, 3), (val, 1),
             (gate, 3), (gate, 4), (gate, 5)]
    for fn, idx in order:
        fn(h, idx)


def _proj_mix(h, w, gqk, gmq, mk, mv, seq):
    t = h.shape[0]
    mlen = mk.shape[0] // (t // seq)
    tiles_per_batch = seq // TM
    row = lambda width: pl.BlockSpec((TM, width), lambda i: (i, 0))
    vec = lambda width: pl.BlockSpec((1, width), lambda i: (0, 0))
    kv = pl.BlockSpec((mlen, MEM_W), lambda i: (i // tiles_per_batch, 0))
    return pl.pallas_call(
        _proj_mix_kernel,
        grid=(t // TM,),
        in_specs=[row(D_MODEL), _ANY, vec(2 * DIFF_W), vec(MEM_W), kv, kv],
        out_specs=[row(2 * DIFF_W), row(DIFF_W), row(MEM_W), row(3 * D_MODEL)],
        out_shape=[jax.ShapeDtypeStruct((t, 2 * DIFF_W), BF16),
                   jax.ShapeDtypeStruct((t, DIFF_W), BF16),
                   jax.ShapeDtypeStruct((t, MEM_W), BF16),
                   jax.ShapeDtypeStruct((t, 3 * D_MODEL), BF16)],
        scratch_shapes=_weight_scratch(D_MODEL, PROJ_MIX_W),
        compiler_params=_params(("arbitrary",)),
        name="proj_mix",
    )(h, w, gqk, gmq, mk, mv)


def _ret_kernel(q_ref, k_ref, v_ref, g_ref, dec_ref, qd_ref, kd_ref, o_ref, *, seq):
    nc = seq // RET_C
    heads = RET_HEADS_PER_STEP
    st = [None] * heads
    for c in range(nc):
        lo = c * RET_C
        for hd in range(heads):
            qd = qd_ref[hd]
            q = q_ref[lo:lo + RET_C, hd * RET_DK:(hd + 1) * RET_DK]
            k = k_ref[lo:lo + RET_C, hd * RET_DK:(hd + 1) * RET_DK]
            v = v_ref[lo:lo + RET_C, hd * RET_DV:(hd + 1) * RET_DV]
            s = _dot_nt(q, k) * dec_ref[hd]
            out = _dot(s.astype(BF16), v)
            if st[hd] is not None:
                out = out + _dot(q, st[hd].astype(BF16)) * qd
            if c + 1 < nc:
                upd = _dot_tn((k.astype(F32) * kd_ref[hd]).astype(BF16), v)
                chunk_decay = qd[RET_C - 1:RET_C, :]
                st[hd] = upd if st[hd] is None else chunk_decay * st[hd] + upd
            gate = g_ref[lo:lo + RET_C, hd * RET_DV:(hd + 1) * RET_DV].astype(F32)
            o_ref[lo:lo + RET_C, hd * RET_DV:(hd + 1) * RET_DV] = (_rms(out) * gate).astype(BF16)


def _retention(qk, v, gate, dec, qd, kd, batch, seq):
    t = batch * seq
    hp = RET_HEADS_PER_STEP
    groups = RET_HEADS // hp
    head = lambda b, h: (b, h)
    const = lambda b, h: (h, 0, 0)
    return pl.pallas_call(
        functools.partial(_ret_kernel, seq=seq),
        grid=(batch, groups),
        in_specs=[pl.BlockSpec((seq, hp * RET_DK), head),
                  pl.BlockSpec((seq, hp * RET_DK), lambda b, h: (b, groups + h)),
                  pl.BlockSpec((seq, hp * RET_DV), head),
                  pl.BlockSpec((seq, hp * RET_DV), head),
                  pl.BlockSpec((hp, RET_C, RET_C), const),
                  pl.BlockSpec((hp, RET_C, 1), const),
                  pl.BlockSpec((hp, RET_C, 1), const)],
        out_specs=pl.BlockSpec((seq, hp * RET_DV), head),
        out_shape=jax.ShapeDtypeStruct((t, RET_V_W), BF16),
        compiler_params=_params(("parallel", "parallel")),
        name="retention",
    )(qk, qk, v, gate, dec, qd, kd)


def _diff_kernel(q_ref, k_ref, v_ref, lq1_ref, lk1_ref, lq2_ref, lk2_ref, go_ref, o_ref,
                 *, lam_init, seq):
    tq = DIFF_TQ
    lane = lax.broadcasted_iota(jnp.int32, (1, 2 * DIFF_D), 1)
    first_map = lane < DIFF_D
    r = lax.broadcasted_iota(jnp.int32, (tq, tq), 0)
    cidx = lax.broadcasted_iota(jnp.int32, (tq, tq), 1)
    keep = cidx <= r
    keep = jnp.concatenate([keep, keep], axis=0)
    lam = (jnp.exp(jnp.sum(lq1_ref[...] * lk1_ref[...], axis=-1, keepdims=True))
           - jnp.exp(jnp.sum(lq2_ref[...] * lk2_ref[...], axis=-1, keepdims=True))
           + lam_init)
    go = go_ref[...]
    nq = seq // tq
    hw = 2 * DIFF_D
    heads = DIFF_HEADS_PER_STEP
    ones = jnp.ones((seq, hw), BF16)
    v_ext = [jnp.concatenate([v_ref[:, hd * hw:(hd + 1) * hw], ones], axis=1)
             for hd in range(heads)]

    def finish(hd, i, o_ext):
        r_ = o_ext[:, :hw] / o_ext[:, hw:]
        d = r_[:tq] - lam * r_[tq:]
        o_ref[i * tq:(i + 1) * tq, hd * hw:(hd + 1) * hw] = (_rms(d) * go).astype(BF16)

    prev = [None] * heads
    for i in list(range(nq)) + [None]:
        tiles = [[] for _ in range(heads)]
        run_max = [None] * heads
        acc = [None] * heads
        qs = [None] * heads
        if i is not None:
            for hd in range(heads):
                q = q_ref[i * tq:(i + 1) * tq, hd * hw:(hd + 1) * hw]
                zero = jnp.zeros_like(q)
                qs[hd] = jnp.concatenate([jnp.where(first_map, q, zero),
                                          jnp.where(first_map, zero, q)], axis=0)
        n_cur = 0 if i is None else i + 1
        n_prev = 0 if prev[0] is None else len(prev[0][1])
        for j in range(max(n_cur, n_prev)):
            for hd in range(heads):
                if j < n_cur:
                    s = _dot_nt(qs[hd], k_ref[j * tq:(j + 1) * tq, hd * hw:(hd + 1) * hw])
                    if j == i:
                        s = jnp.where(keep, s, NEG)
                    tiles[hd].append(s)
                    run_max[hd] = s if run_max[hd] is None else jnp.maximum(run_max[hd], s)
                if j < n_prev:
                    p = jnp.exp2(prev[hd][1][j] - prev[hd][2]).astype(BF16)
                    pv = _dot(p, v_ext[hd][j * tq:(j + 1) * tq, :])
                    acc[hd] = pv if acc[hd] is None else acc[hd] + pv
        for hd in range(heads):
            if prev[hd] is not None:
                finish(hd, prev[hd][0], acc[hd])
            prev[hd] = (None if i is None else
                        (i, tiles[hd], jnp.max(run_max[hd], axis=-1, keepdims=True)))


def _diff_attention(qk, v, lq1, lk1, lq2, lk2, g_out, batch, seq, lam_init):
    t = batch * seq
    width = DIFF_HEADS_PER_STEP * 2 * DIFF_D
    groups = DIFF_HEADS // DIFF_HEADS_PER_STEP
    vec64 = pl.BlockSpec((1, DIFF_D), lambda b, h: (0, 0))
    head = pl.BlockSpec((seq, width), lambda b, h: (b, h))
    return pl.pallas_call(
        functools.partial(_diff_kernel, lam_init=lam_init, seq=seq),
        grid=(batch, groups),
        in_specs=[head,
                  pl.BlockSpec((seq, width), lambda b, h: (b, groups + h)),
                  head,
                  vec64, vec64, vec64, vec64,
                  pl.BlockSpec((1, 2 * DIFF_D), lambda b, h: (0, 0))],
        out_specs=head,
        out_shape=jax.ShapeDtypeStruct((t, DIFF_HEADS * 2 * DIFF_D), BF16),
        compiler_params=_params(("parallel", "parallel")),
        name="diff_attention",
    )(qk, qk, v, lq1, lk1, lq2, lk2, g_out)


def _memkv_kernel(mem_ref, g_ref, w_ref, gk_ref, k_ref, v_ref):
    mn = (_rms(mem_ref[...]) * g_ref[...]).astype(BF16)
    kv = _dot(mn, w_ref[...])
    width = MEM_HEADS * MEM_D
    k_ref[...] = _group_norm_128(kv[:, :width], gk_ref[...], MEM_D).astype(BF16)
    v_ref[...] = kv[:, width:].astype(BF16)


def _memkv(mem2d, g_mem, w_kv, gk, batch, mlen):
    width = MEM_HEADS * MEM_D
    row = pl.BlockSpec((mlen, width), lambda b: (b, 0))
    vec = pl.BlockSpec((1, width), lambda b: (0, 0))
    return pl.pallas_call(
        _memkv_kernel,
        grid=(batch,),
        in_specs=[pl.BlockSpec((mlen, D_MODEL), lambda b: (b, 0)), vec,
                  _resident(w_kv.shape), vec],
        out_specs=[row, row],
        out_shape=[jax.ShapeDtypeStruct((batch * mlen, width), BF16)] * 2,
        compiler_params=_params(("parallel",)),
        name="memkv",
    )(mem2d, g_mem, w_kv, gk)


def _merge_kernel(x_ref, ret_ref, dif_ref, mo_ref, gt_ref, wr_hbm, wd_hbm, wm_hbm, wo_hbm, o_ref,
                  wr_ref, wd_ref, wm_ref, wo_ref, stage, sem):
    @pl.when(pl.program_id(0) == 0)
    def _():
        for src, dst in ((wr_hbm, wr_ref), (wd_hbm, wd_ref), (wm_hbm, wm_ref), (wo_hbm, wo_ref)):
            _load_weight(src, 0, dst, stage, sem)

    gt = gt_ref[...].astype(F32)
    y = (gt[:, :D_MODEL] * _dot(ret_ref[...], wr_ref[...])
         + gt[:, D_MODEL:2 * D_MODEL] * _dot(dif_ref[...], wd_ref[...])
         + gt[:, 2 * D_MODEL:] * _dot(mo_ref[...], wm_ref[...]))
    o_ref[...] = x_ref[...] + _dot(y.astype(BF16), wo_ref[...])


def _merge(x, ret, dif, mo, gates, wr, wd, wm, wo):
    t = x.shape[0]
    row = lambda w: pl.BlockSpec((TM, w), lambda i: (i, 0))
    return pl.pallas_call(
        _merge_kernel,
        grid=(t // TM,),
        in_specs=[row(D_MODEL), row(ret.shape[1]), row(D_MODEL), row(D_MODEL), row(3 * D_MODEL),
                  _ANY, _ANY, _ANY, _ANY],
        out_specs=row(D_MODEL),
        out_shape=jax.ShapeDtypeStruct((t, D_MODEL), F32),
        scratch_shapes=[pltpu.VMEM(wr.shape, BF16), pltpu.VMEM(wd.shape, BF16),
                        pltpu.VMEM(wm.shape, BF16), pltpu.VMEM(wo.shape, BF16)]
        + _weight_scratch(D_MODEL, D_MODEL)[1:],
        compiler_params=_params(("arbitrary",)),
        name="merge",
    )(x, ret, dif, mo, gates, wr, wd, wm, wo)


def _retention_constants():
    h = np.arange(RET_HEADS, dtype=np.float64)
    log_g = np.log1p(-(2.0 ** (-5.0 - h)))
    idx = np.arange(RET_C, dtype=np.float64)
    dist = idx[:, None] - idx[None, :]
    dec = np.where(dist >= 0, np.exp(log_g[:, None, None] * np.maximum(dist, 0.0)), 0.0)
    qd = np.exp(log_g[:, None] * (idx + 1.0))[:, :, None]
    kd = np.exp(log_g[:, None] * (RET_C - 1.0 - idx))[:, :, None]
    return (jnp.asarray(dec, F32), jnp.asarray(qd, F32), jnp.asarray(kd, F32))


def kernel(x, mem, positions, g_ffn1, w_ffn1_in, w_ffn1_out, g_mix, w_in, g_diff_q, g_diff_k,
           lam_q1, lam_k1, lam_q2, lam_k2, g_diff_out, g_mem_q, g_mem_k, g_mem, w_mem_kv,
           w_br_ret, w_br_diff, w_br_mem, w_o, g_ffn2, w_ffn2_in, w_ffn2_out):
    batch, seq, _ = x.shape
    mlen = mem.shape[1]
    t = batch * seq
    depth = g_ffn1.shape[0]
    half = RET_DK // 2
    inv = jnp.asarray(ROPE_BASE ** (-np.arange(half, dtype=np.float64) / half), F32)[None, :]
    dec, qd, kd = _retention_constants()
    pos = positions.reshape(t, 1)
    mem2d = mem.reshape(batch * mlen, D_MODEL)
    xf = x.reshape(t, D_MODEL)
    bf = lambda w: w.astype(BF16)
    f32 = lambda w: w.astype(F32)
    vec = lambda g: g.astype(F32)[None, :]

    for l in range(depth):
        lam_init = 0.8 - 0.6 * math.exp(-0.3 * l)
        x1, h = _ffn(xf, vec(g_ffn1[l]), f32(w_ffn1_in[l]), f32(w_ffn1_out[l]), vec(g_mix[l]))

        w = f32(w_in[l])
        rqk, rv, rg = _proj_ret(h, w, pos, inv)
        qk_gain = jnp.concatenate([jnp.tile(g_diff_q[l].astype(F32), 2 * DIFF_HEADS) * (DIFF_D ** -0.5 * LOG2E),
                                   jnp.tile(g_diff_k[l].astype(F32), 2 * DIFF_HEADS)])[None, :]
        mq_gain = (jnp.tile(g_mem_q[l].astype(F32), MEM_HEADS) * (MEM_D ** -0.5 * LOG2E))[None, :]
        mk, mv = _memkv(mem2d, vec(g_mem[l]), bf(w_mem_kv[l]),
                        jnp.tile(g_mem_k[l].astype(F32), MEM_HEADS)[None, :], batch, mlen)
        dqk, dv, mo, gates = _proj_mix(h, w, qk_gain, mq_gain, mk, mv, seq)

        ret = _retention(rqk, rv, rg, dec, qd, kd, batch, seq)
        go = (g_diff_out[l].astype(F32) * (1.0 - lam_init))[None, :]
        dif = _diff_attention(dqk, dv, vec(lam_q1[l]), vec(lam_k1[l]), vec(lam_q2[l]),
                              vec(lam_k2[l]), go, batch, seq, lam_init)

        x2 = _merge(x1, ret, dif, mo, gates, f32(w_br_ret[l]), f32(w_br_diff[l]),
                    f32(w_br_mem[l]), f32(w_o[l]))
        (xf,) = _ffn(x2, vec(g_ffn2[l]), f32(w_ffn2_in[l]), f32(w_ffn2_out[l]))
    return xf.reshape(batch, seq, D_MODEL)
```

```python
import functools
import math

import jax
import jax.numpy as jnp
import numpy as np
from jax import lax
from jax.experimental import pallas as pl
from jax.experimental.pallas import tpu as pltpu

F32 = jnp.float32
BF16 = jnp.bfloat16

D_MODEL = 1024
D_FF = 2816
FFN_RES = 0.5
EPS = 1e-6
NEG = -1e30
LOG2E = math.log2(math.e)
ROPE_BASE = 10000.0

RET_HEADS = 4
RET_DK = 256
RET_DV = 512
DIFF_HEADS = 8
DIFF_D = 64
MEM_HEADS = 4
MEM_D = 256

V7X_VMEM_LIMIT_BYTES = 56 * 1024 * 1024

TM = 512
TM_FFN = 1024
FFN_CHUNK = 256
RET_C = 256
RET_HEADS_PER_STEP = 2
DIFF_TQ = 256
DIFF_HEADS_PER_STEP = 2


def _params(sem):
    return pltpu.CompilerParams(dimension_semantics=sem,
                                vmem_limit_bytes=V7X_VMEM_LIMIT_BYTES)


def _resident(shape):
    nd = len(shape)
    return pl.BlockSpec(shape, lambda *_: (0,) * nd, pipeline_mode=pl.Buffered(1))


def _rms(x):
    return x * lax.rsqrt(jnp.mean(x * x, axis=-1, keepdims=True) + EPS)


def _after(x, *deps):
    folded = None
    for d in deps:
        t = jnp.sum(d.astype(F32), axis=0, keepdims=True)
        for c in range(t.shape[1] // 128):
            blk = t[:, c * 128:(c + 1) * 128]
            folded = blk if folded is None else folded + blk
    bits = lax.bitcast_convert_type(folded, jnp.uint32)
    zero = lax.shift_right_logical(lax.shift_right_logical(bits, jnp.uint32(16)), jnp.uint32(16))
    zero = zero.astype(F32).astype(x.dtype)
    zero = jnp.concatenate([zero] * (x.shape[1] // 128), axis=1)
    return jnp.concatenate([x[:16] + zero, x[16:]], axis=0)


def _dot(a, b):
    return jnp.dot(a, b, preferred_element_type=F32)


def _dot_nt(a, b):
    return lax.dot_general(a, b, (((1,), (1,)), ((), ())), preferred_element_type=F32)


def _dot_tn(a, b):
    return lax.dot_general(a, b, (((0,), (0,)), ((), ())), preferred_element_type=F32)


STAGE_BYTES = 1024 * 1024
STAGE_SLOTS = 4


def _stage_rows(width):
    return max(16, (STAGE_BYTES // (4 * width)) // 16 * 16)


def _weight_scratch(rows, width):
    return [pltpu.VMEM((rows, width), BF16),
            pltpu.VMEM((STAGE_SLOTS, _stage_rows(width), width), F32),
            pltpu.SemaphoreType.DMA((STAGE_SLOTS,))]


def _load_weight(w_hbm, col0, w_vmem, stage, sem):
    rows, width = w_vmem.shape
    slots, step = stage.shape[0], stage.shape[1]
    starts = list(range(0, rows, step))

    def copy(c):
        n = min(step, rows - starts[c])
        return pltpu.make_async_copy(
            w_hbm.at[pl.ds(starts[c], n), pl.ds(col0, width)],
            stage.at[c % slots, pl.ds(0, n), :], sem.at[c % slots])

    for c in range(min(slots - 1, len(starts))):
        copy(c).start()
    for c in range(len(starts)):
        if c + slots - 1 < len(starts):
            copy(c + slots - 1).start()
        copy(c).wait()
        n = min(step, rows - starts[c])
        w_vmem[starts[c]:starts[c] + n, :] = stage[c % slots, 0:n, :].astype(BF16)


_ANY = pl.BlockSpec(memory_space=pl.ANY)


def _ffn_kernel(*refs, emit_next):
    if emit_next:
        x_ref, g_ref, win_hbm, wout_hbm, gn_ref, o_ref, h_ref = refs[:7]
    else:
        x_ref, g_ref, win_hbm, wout_hbm, o_ref = refs[:5]
    win_ref, win_stage, win_sem, wout_ref, wout_stage, wout_sem = refs[-6:]

    @pl.when(pl.program_id(0) == 0)
    def _():
        _load_weight(win_hbm, 0, win_ref, win_stage, win_sem)
        _load_weight(wout_hbm, 0, wout_ref, wout_stage, wout_sem)

    x = x_ref[...]
    hb = (_rms(x) * g_ref[...]).astype(BF16)
    acc = jnp.zeros(x.shape, F32)
    for j in range(D_FF // FFN_CHUNK):
        lo = j * FFN_CHUNK
        a = _dot(hb, win_ref[:, lo:lo + FFN_CHUNK])
        b = _dot(hb, win_ref[:, D_FF + lo:D_FF + lo + FFN_CHUNK])
        act = (a * jax.nn.sigmoid(a) * b).astype(BF16)
        acc = acc + _dot(act, wout_ref[lo:lo + FFN_CHUNK, :])
    y = x + FFN_RES * acc
    o_ref[...] = y
    if emit_next:
        h_ref[...] = (_rms(y) * gn_ref[...]).astype(BF16)


def _ffn(x, g, w_in, w_out, g_next=None):
    t = x.shape[0]
    emit = g_next is not None
    row = pl.BlockSpec((TM_FFN, D_MODEL), lambda i: (i, 0))
    vec = pl.BlockSpec((1, D_MODEL), lambda i: (0, 0))
    in_specs = [row, vec, _ANY, _ANY]
    args = [x, g, w_in, w_out]
    out_shape = [jax.ShapeDtypeStruct((t, D_MODEL), F32)]
    out_specs = [row]
    if emit:
        in_specs.append(vec)
        args.append(g_next)
        out_shape.append(jax.ShapeDtypeStruct((t, D_MODEL), BF16))
        out_specs.append(row)
    return pl.pallas_call(
        functools.partial(_ffn_kernel, emit_next=emit),
        grid=(t // TM_FFN,),
        in_specs=in_specs,
        out_specs=out_specs,
        out_shape=out_shape,
        scratch_shapes=_weight_scratch(*w_in.shape) + _weight_scratch(*w_out.shape),
        compiler_params=_params(("arbitrary",)),
        name="ffn_next" if emit else "ffn",
    )(*args)


def _group_norm_128(y, gain, group):
    cols = y.shape[1]
    outs = []
    if group == 64:
        lane = lax.broadcasted_iota(jnp.int32, (1, 128), 1)
        first = lane < 64
        for c in range(cols // 128):
            blk = y[:, c * 128:(c + 1) * 128]
            sq = blk * blk
            s_lo = jnp.sum(jnp.where(first, sq, 0.0), axis=-1, keepdims=True)
            s_hi = jnp.sum(jnp.where(first, 0.0, sq), axis=-1, keepdims=True)
            r = jnp.where(first, lax.rsqrt(s_lo * (1.0 / 64) + EPS),
                          lax.rsqrt(s_hi * (1.0 / 64) + EPS))
            outs.append(blk * r)
    else:
        for c in range(cols // group):
            blk = y[:, c * group:(c + 1) * group]
            outs.append(_rms(blk))
    return jnp.concatenate(outs, axis=1) * gain


RET_QK_W = RET_HEADS * RET_DK
RET_V_W = RET_HEADS * RET_DV
PROJ_RET_W = 2 * RET_QK_W + 2 * RET_V_W
DIFF_W = DIFF_HEADS * 2 * DIFF_D
MEM_W = MEM_HEADS * MEM_D
PROJ_MIX_W = 2 * DIFF_W + DIFF_W + MEM_W + 3 * D_MODEL
PROJ_CHUNK = 512

def _proj_ret_kernel(h_ref, w_hbm, pos_ref, inv_ref, qk_ref, v_ref, g_ref, w_ref, w_stage, w_sem):
    @pl.when(pl.program_id(0) == 0)
    def _():
        _load_weight(w_hbm, 0, w_ref, w_stage, w_sem)

    h = h_ref[...]
    tm = h.shape[0]
    n_piece = 8
    rows = tm // n_piece
    cos_p, sin_p = [], []
    for r in range(n_piece):
        ang = pos_ref[r * rows:(r + 1) * rows, :].astype(F32) * inv_ref[...]
        cos_p.append(jnp.cos(ang))
        sin_p.append(jnp.sin(ang))
    cos = jnp.concatenate(cos_p, axis=0)
    sin = jnp.concatenate(sin_p, axis=0)
    half = RET_DK // 2

    def rot(lhs, hd):
        lo = hd * RET_DK
        y = _dot(lhs, w_ref[:, lo:lo + RET_DK])
        x1 = y[:, :half]
        x2 = y[:, half:]
        r1 = x1 * cos - x2 * sin
        r2 = x1 * sin + x2 * cos
        if hd >= RET_HEADS:
            r1 = r1 * (RET_DK ** -0.5)
            r2 = r2 * (RET_DK ** -0.5)
        r1 = r1.astype(BF16)
        r2 = r2.astype(BF16)
        qk_ref[:, lo:lo + half] = r1
        qk_ref[:, lo + half:lo + RET_DK] = r2
        return r1, r2

    def val(lhs, c):
        lo = c * PROJ_CHUNK
        v_ref[:, lo:lo + PROJ_CHUNK] = _dot(
            lhs, w_ref[:, 2 * RET_QK_W + lo:2 * RET_QK_W + lo + PROJ_CHUNK]).astype(BF16)

    def gate(lhs, c):
        lo = c * PROJ_CHUNK
        wlo = 2 * RET_QK_W + RET_V_W + lo
        y = _dot(lhs, w_ref[:, wlo:wlo + PROJ_CHUNK])
        g_ref[:, lo:lo + PROJ_CHUNK] = (y * jax.nn.sigmoid(y)).astype(BF16)

    plain = [(val, 0), (gate, 0), (val, 1), (gate, 1), (val, 2), (gate, 2), (val, 3), (gate, 3)]
    for r, (fn, idx) in enumerate(plain):
        fn(h if r == 0 else _after(h, cos_p[r - 1], sin_p[r - 1]), idx)
    for hd in range(2 * RET_HEADS):
        rot(h, hd)


def _proj_ret(h, w, pos, inv):
    t = h.shape[0]
    row = lambda width: pl.BlockSpec((TM, width), lambda i: (i, 0))
    return pl.pallas_call(
        _proj_ret_kernel,
        grid=(t // TM,),
        in_specs=[row(D_MODEL), _ANY, row(1), pl.BlockSpec((1, 128), lambda i: (0, 0))],
        out_specs=[row(2 * RET_QK_W), row(RET_V_W), row(RET_V_W)],
        out_shape=[jax.ShapeDtypeStruct((t, 2 * RET_QK_W), BF16),
                   jax.ShapeDtypeStruct((t, RET_V_W), BF16),
                   jax.ShapeDtypeStruct((t, RET_V_W), BF16)],
        scratch_shapes=_weight_scratch(D_MODEL, PROJ_RET_W),
        compiler_params=_params(("arbitrary",)),
        name="proj_ret",
    )(h, w, pos, inv)


def _proj_mix_kernel(h_ref, w_hbm, gqk_ref, gmq_ref, mk_ref, mv_ref,
                     qk_ref, v_ref, mo_ref, gt_ref, w_ref, w_stage, w_sem):
    @pl.when(pl.program_id(0) == 0)
    def _():
        _load_weight(w_hbm, PROJ_RET_W, w_ref, w_stage, w_sem)

    h = h_ref[...]
    off_qk = 0
    off_v = off_qk + 2 * DIFF_W
    off_mq = off_v + DIFF_W
    off_gt = off_mq + MEM_W

    mem_q, mem_p = {}, {}

    def qk_norm(lhs, c):
        lo = c * 256
        y = _dot(lhs, w_ref[:, off_qk + lo:off_qk + lo + 256])
        out = _group_norm_128(y, gqk_ref[:, lo:lo + 256], DIFF_D).astype(BF16)
        qk_ref[:, lo:lo + 256] = out
        return out

    def val(lhs, c):
        lo = c * PROJ_CHUNK
        out = _dot(lhs, w_ref[:, off_v + lo:off_v + lo + PROJ_CHUNK]).astype(BF16)
        v_ref[:, lo:lo + PROJ_CHUNK] = out
        return out

    def gate(lhs, c):
        lo = c * PROJ_CHUNK
        y = _dot(lhs, w_ref[:, off_gt + lo:off_gt + lo + PROJ_CHUNK])
        out = jax.nn.sigmoid(y).astype(BF16)
        gt_ref[:, lo:lo + PROJ_CHUNK] = out
        return out

    def mem_query(lhs, hd):
        sl = slice(hd * MEM_D, (hd + 1) * MEM_D)
        y = _dot(lhs, w_ref[:, off_mq + hd * MEM_D:off_mq + (hd + 1) * MEM_D])
        mem_q[hd] = (_rms(y) * gmq_ref[:, sl]).astype(BF16)
        return mem_q[hd]

    def mem_scores(lhs, hd):
        sl = slice(hd * MEM_D, (hd + 1) * MEM_D)
        s = _dot_nt(mem_q[hd], mk_ref[:, sl])
        p = jnp.exp2(s - jnp.max(s, axis=-1, keepdims=True))
        mem_p[hd] = (p.astype(BF16), jnp.sum(p, axis=-1, keepdims=True))
        return mem_p[hd][0]

    def mem_values(lhs, hd):
        sl = slice(hd * MEM_D, (hd + 1) * MEM_D)
        p, l = mem_p[hd]
        out = (_dot(p, mv_ref[:, sl]) / l).astype(BF16)
        mo_ref[:, sl] = out
        return out

    order = [(gate, 0), (mem_query, 0), (qk_norm, 0), (mem_scores, 0), (qk_norm, 1),
             (mem_values, 0), (gate, 1), (mem_query, 1), (qk_norm, 2), (mem_scores, 1),
             (qk_norm, 3), (mem_values, 1), (val, 0), (mem_query, 2), (qk_norm, 4),
             (mem_scores, 2), (qk_norm, 5), (mem_values, 2), (gate, 2), (mem_query, 3),
             (qk_norm, 6), (mem_scores, 3), (qk_norm, 7), (mem_values, 3), (val, 1),
             (gate, 3), (gate, 4), (gate, 5)]
    for fn, idx in order:
        fn(h, idx)


def _proj_mix(h, w, gqk, gmq, mk, mv, seq):
    t = h.shape[0]
    mlen = mk.shape[0] // (t // seq)
    tiles_per_batch = seq // TM
    row = lambda width: pl.BlockSpec((TM, width), lambda i: (i, 0))
    vec = lambda width: pl.BlockSpec((1, width), lambda i: (0, 0))
    kv = pl.BlockSpec((mlen, MEM_W), lambda i: (i // tiles_per_batch, 0))
    return pl.pallas_call(
        _proj_mix_kernel,
        grid=(t // TM,),
        in_specs=[row(D_MODEL), _ANY, vec(2 * DIFF_W), vec(MEM_W), kv, kv],
        out_specs=[row(2 * DIFF_W), row(DIFF_W), row(MEM_W), row(3 * D_MODEL)],
        out_shape=[jax.ShapeDtypeStruct((t, 2 * DIFF_W), BF16),
                   jax.ShapeDtypeStruct((t, DIFF_W), BF16),
                   jax.ShapeDtypeStruct((t, MEM_W), BF16),
                   jax.ShapeDtypeStruct((t, 3 * D_MODEL), BF16)],
        scratch_shapes=_weight_scratch(D_MODEL, PROJ_MIX_W),
        compiler_params=_params(("arbitrary",)),
        name="proj_mix",
    )(h, w, gqk, gmq, mk, mv)


def _ret_kernel(q_ref, k_ref, v_ref, g_ref, dec_ref, qd_ref, kd_ref, o_ref, *, seq):
    nc = seq // RET_C
    heads = RET_HEADS_PER_STEP
    st = [None] * heads
    for c in range(nc):
        lo = c * RET_C
        for hd in range(heads):
            qd = qd_ref[hd]
            q = q_ref[lo:lo + RET_C, hd * RET_DK:(hd + 1) * RET_DK]
            k = k_ref[lo:lo + RET_C, hd * RET_DK:(hd + 1) * RET_DK]
            v = v_ref[lo:lo + RET_C, hd * RET_DV:(hd + 1) * RET_DV]
            s = _dot_nt(q, k) * dec_ref[hd]
            out = _dot(s.astype(BF16), v)
            if st[hd] is not None:
                out = out + _dot(q, st[hd].astype(BF16)) * qd
            if c + 1 < nc:
                upd = _dot_tn((k.astype(F32) * kd_ref[hd]).astype(BF16), v)
                chunk_decay = qd[RET_C - 1:RET_C, :]
                st[hd] = upd if st[hd] is None else chunk_decay * st[hd] + upd
            gate = g_ref[lo:lo + RET_C, hd * RET_DV:(hd + 1) * RET_DV].astype(F32)
            o_ref[lo:lo + RET_C, hd * RET_DV:(hd + 1) * RET_DV] = (_rms(out) * gate).astype(BF16)


def _retention(qk, v, gate, dec, qd, kd, batch, seq):
    t = batch * seq
    hp = RET_HEADS_PER_STEP
    groups = RET_HEADS // hp
    head = lambda b, h: (b, h)
    const = lambda b, h: (h, 0, 0)
    return pl.pallas_call(
        functools.partial(_ret_kernel, seq=seq),
        grid=(batch, groups),
        in_specs=[pl.BlockSpec((seq, hp * RET_DK), head),
                  pl.BlockSpec((seq, hp * RET_DK), lambda b, h: (b, groups + h)),
                  pl.BlockSpec((seq, hp * RET_DV), head),
                  pl.BlockSpec((seq, hp * RET_DV), head),
                  pl.BlockSpec((hp, RET_C, RET_C), const),
                  pl.BlockSpec((hp, RET_C, 1), const),
                  pl.BlockSpec((hp, RET_C, 1), const)],
        out_specs=pl.BlockSpec((seq, hp * RET_DV), head),
        out_shape=jax.ShapeDtypeStruct((t, RET_V_W), BF16),
        compiler_params=_params(("parallel", "parallel")),
        name="retention",
    )(qk, qk, v, gate, dec, qd, kd)


def _diff_kernel(q_ref, k_ref, v_ref, lq1_ref, lk1_ref, lq2_ref, lk2_ref, go_ref, o_ref,
                 *, lam_init, seq):
    tq = DIFF_TQ
    lane = lax.broadcasted_iota(jnp.int32, (1, 2 * DIFF_D), 1)
    first_map = lane < DIFF_D
    r = lax.broadcasted_iota(jnp.int32, (tq, tq), 0)
    cidx = lax.broadcasted_iota(jnp.int32, (tq, tq), 1)
    keep = cidx <= r
    keep = jnp.concatenate([keep, keep], axis=0)
    lam = (jnp.exp(jnp.sum(lq1_ref[...] * lk1_ref[...], axis=-1, keepdims=True))
           - jnp.exp(jnp.sum(lq2_ref[...] * lk2_ref[...], axis=-1, keepdims=True))
           + lam_init)
    go = go_ref[...]
    nq = seq // tq
    hw = 2 * DIFF_D
    heads = DIFF_HEADS_PER_STEP
    ones = jnp.ones((seq, hw), BF16)
    v_ext = [jnp.concatenate([v_ref[:, hd * hw:(hd + 1) * hw], ones], axis=1)
             for hd in range(heads)]

    def finish(hd, i, o_ext):
        r_ = o_ext[:, :hw] / o_ext[:, hw:]
        d = r_[:tq] - lam * r_[tq:]
        o_ref[i * tq:(i + 1) * tq, hd * hw:(hd + 1) * hw] = (_rms(d) * go).astype(BF16)

    prev = [None] * heads
    for i in list(range(nq)) + [None]:
        tiles = [[] for _ in range(heads)]
        run_max = [None] * heads
        acc = [None] * heads
        qs = [None] * heads
        if i is not None:
            for hd in range(heads):
                q = q_ref[i * tq:(i + 1) * tq, hd * hw:(hd + 1) * hw]
                zero = jnp.zeros_like(q)
                qs[hd] = jnp.concatenate([jnp.where(first_map, q, zero),
                                          jnp.where(first_map, zero, q)], axis=0)
        n_cur = 0 if i is None else i + 1
        n_prev = 0 if prev[0] is None else len(prev[0][1])
        for j in range(max(n_cur, n_prev)):
            for hd in range(heads):
                if j < n_cur:
                    s = _dot_nt(qs[hd], k_ref[j * tq:(j + 1) * tq, hd * hw:(hd + 1) * hw])
                    if j == i:
                        s = jnp.where(keep, s, NEG)
                    tiles[hd].append(s)
                    run_max[hd] = s if run_max[hd] is None else jnp.maximum(run_max[hd], s)
                if j < n_prev:
                    p = jnp.exp2(prev[hd][1][j] - prev[hd][2]).astype(BF16)
                    pv = _dot(p, v_ext[hd][j * tq:(j + 1) * tq, :])
                    acc[hd] = pv if acc[hd] is None else acc[hd] + pv
        for hd in range(heads):
            if prev[hd] is not None:
                finish(hd, prev[hd][0], acc[hd])
            prev[hd] = (None if i is None else
                        (i, tiles[hd], jnp.max(run_max[hd], axis=-1, keepdims=True)))


def _diff_attention(qk, v, lq1, lk1, lq2, lk2, g_out, batch, seq, lam_init):
    t = batch * seq
    width = DIFF_HEADS_PER_STEP * 2 * DIFF_D
    groups = DIFF_HEADS // DIFF_HEADS_PER_STEP
    vec64 = pl.BlockSpec((1, DIFF_D), lambda b, h: (0, 0))
    head = pl.BlockSpec((seq, width), lambda b, h: (b, h))
    return pl.pallas_call(
        functools.partial(_diff_kernel, lam_init=lam_init, seq=seq),
        grid=(batch, groups),
        in_specs=[head,
                  pl.BlockSpec((seq, width), lambda b, h: (b, groups + h)),
                  head,
                  vec64, vec64, vec64, vec64,
                  pl.BlockSpec((1, 2 * DIFF_D), lambda b, h: (0, 0))],
        out_specs=head,
        out_shape=jax.ShapeDtypeStruct((t, DIFF_HEADS * 2 * DIFF_D), BF16),
        compiler_params=_params(("parallel", "parallel")),
        name="diff_attention",
    )(qk, qk, v, lq1, lk1, lq2, lk2, g_out)


def _memkv_kernel(mem_ref, g_ref, w_ref, gk_ref, k_ref, v_ref):
    mn = (_rms(mem_ref[...]) * g_ref[...]).astype(BF16)
    kv = _dot(mn, w_ref[...])
    width = MEM_HEADS * MEM_D
    k_ref[...] = _group_norm_128(kv[:, :width], gk_ref[...], MEM_D).astype(BF16)
    v_ref[...] = kv[:, width:].astype(BF16)


def _memkv(mem2d, g_mem, w_kv, gk, batch, mlen):
    width = MEM_HEADS * MEM_D
    row = pl.BlockSpec((mlen, width), lambda b: (b, 0))
    vec = pl.BlockSpec((1, width), lambda b: (0, 0))
    return pl.pallas_call(
        _memkv_kernel,
        grid=(batch,),
        in_specs=[pl.BlockSpec((mlen, D_MODEL), lambda b: (b, 0)), vec,
                  _resident(w_kv.shape), vec],
        out_specs=[row, row],
        out_shape=[jax.ShapeDtypeStruct((batch * mlen, width), BF16)] * 2,
        compiler_params=_params(("parallel",)),
        name="memkv",
    )(mem2d, g_mem, w_kv, gk)


def _merge_kernel(x_ref, ret_ref, dif_ref, mo_ref, gt_ref, wr_hbm, wd_hbm, wm_hbm, wo_hbm, o_ref,
                  wr_ref, wd_ref, wm_ref, wo_ref, stage, sem):
    @pl.when(pl.program_id(0) == 0)
    def _():
        for src, dst in ((wr_hbm, wr_ref), (wd_hbm, wd_ref), (wm_hbm, wm_ref), (wo_hbm, wo_ref)):
            _load_weight(src, 0, dst, stage, sem)

    gt = gt_ref[...].astype(F32)
    y = (gt[:, :D_MODEL] * _dot(ret_ref[...], wr_ref[...])
         + gt[:, D_MODEL:2 * D_MODEL] * _dot(dif_ref[...], wd_ref[...])
         + gt[:, 2 * D_MODEL:] * _dot(mo_ref[...], wm_ref[...]))
    o_ref[...] = x_ref[...] + _dot(y.astype(BF16), wo_ref[...])


def _merge(x, ret, dif, mo, gates, wr, wd, wm, wo):
    t = x.shape[0]
    row = lambda w: pl.BlockSpec((TM, w), lambda i: (i, 0))
    return pl.pallas_call(
        _merge_kernel,
        grid=(t // TM,),
        in_specs=[row(D_MODEL), row(ret.shape[1]), row(D_MODEL), row(D_MODEL), row(3 * D_MODEL),
                  _ANY, _ANY, _ANY, _ANY],
        out_specs=row(D_MODEL),
        out_shape=jax.ShapeDtypeStruct((t, D_MODEL), F32),
        scratch_shapes=[pltpu.VMEM(wr.shape, BF16), pltpu.VMEM(wd.shape, BF16),
                        pltpu.VMEM(wm.shape, BF16), pltpu.VMEM(wo.shape, BF16)]
        + _weight_scratch(D_MODEL, D_MODEL)[1:],
        compiler_params=_params(("arbitrary",)),
        name="merge",
    )(x, ret, dif, mo, gates, wr, wd, wm, wo)


def _ret_merge_kernel(x_ref, qk_ref, v_ref, g_ref, dif_ref, mo_ref, gt_ref, dec_ref, qd_ref, kd_ref,
                      wr_hbm, wd_hbm, wm_hbm, wo_hbm, o_ref,
                      st_ref, ret_ref, wr_ref, wd_ref, wm_ref, wo_ref, stage, sem, *, tiles_per_row):
    i = pl.program_id(0)

    @pl.when(i == 0)
    def _():
        for src, dst in ((wr_hbm, wr_ref), (wd_hbm, wd_ref), (wm_hbm, wm_ref), (wo_hbm, wo_ref)):
            _load_weight(src, 0, dst, stage, sem)

    @pl.when(i % tiles_per_row == 0)
    def _():
        st_ref[...] = jnp.zeros_like(st_ref)

    tm = x_ref.shape[0]
    for c in range(tm // RET_C):
        lo = c * RET_C
        for hd in range(RET_HEADS):
            qd = qd_ref[hd]
            q = qk_ref[lo:lo + RET_C, hd * RET_DK:(hd + 1) * RET_DK]
            k = qk_ref[lo:lo + RET_C, RET_QK_W + hd * RET_DK:RET_QK_W + (hd + 1) * RET_DK]
            v = v_ref[lo:lo + RET_C, hd * RET_DV:(hd + 1) * RET_DV]
            st = st_ref[hd]
            s = _dot_nt(q, k) * dec_ref[hd]
            out = _dot(s.astype(BF16), v) + _dot(q, st.astype(BF16)) * qd
            upd = _dot_tn((k.astype(F32) * kd_ref[hd]).astype(BF16), v)
            st_ref[hd] = qd[RET_C - 1:RET_C, :] * st + upd
            gate = g_ref[lo:lo + RET_C, hd * RET_DV:(hd + 1) * RET_DV].astype(F32)
            ret_ref[lo:lo + RET_C, hd * RET_DV:(hd + 1) * RET_DV] = (_rms(out) * gate).astype(BF16)

    gt = gt_ref[...].astype(F32)
    y = (gt[:, :D_MODEL] * _dot(ret_ref[...], wr_ref[...])
         + gt[:, D_MODEL:2 * D_MODEL] * _dot(dif_ref[...], wd_ref[...])
         + gt[:, 2 * D_MODEL:] * _dot(mo_ref[...], wm_ref[...]))
    o_ref[...] = x_ref[...] + _dot(y.astype(BF16), wo_ref[...])


def _ret_merge(x, qk, v, gate, dif, mo, gates, dec, qd, kd, wr, wd, wm, wo, seq):
    t = x.shape[0]
    row = lambda w: pl.BlockSpec((TM, w), lambda i: (i, 0))
    const = lambda shape: pl.BlockSpec(shape, lambda i: (0,) * len(shape))
    return pl.pallas_call(
        functools.partial(_ret_merge_kernel, tiles_per_row=seq // TM),
        grid=(t // TM,),
        in_specs=[row(D_MODEL), row(2 * RET_QK_W), row(RET_V_W), row(RET_V_W), row(D_MODEL),
                  row(D_MODEL), row(3 * D_MODEL),
                  const(dec.shape), const(qd.shape), const(kd.shape),
                  _ANY, _ANY, _ANY, _ANY],
        out_specs=row(D_MODEL),
        out_shape=jax.ShapeDtypeStruct((t, D_MODEL), F32),
        scratch_shapes=[pltpu.VMEM((RET_HEADS, RET_DK, RET_DV), F32),
                        pltpu.VMEM((TM, RET_V_W), BF16),
                        pltpu.VMEM(wr.shape, BF16), pltpu.VMEM(wd.shape, BF16),
                        pltpu.VMEM(wm.shape, BF16), pltpu.VMEM(wo.shape, BF16)]
        + _weight_scratch(D_MODEL, D_MODEL)[1:],
        compiler_params=_params(("arbitrary",)),
        name="ret_merge",
    )(x, qk, v, gate, dif, mo, gates, dec, qd, kd, wr, wd, wm, wo)


def _retention_constants():
    h = np.arange(RET_HEADS, dtype=np.float64)
    log_g = np.log1p(-(2.0 ** (-5.0 - h)))
    idx = np.arange(RET_C, dtype=np.float64)
    dist = idx[:, None] - idx[None, :]
    dec = np.where(dist >= 0, np.exp(log_g[:, None, None] * np.maximum(dist, 0.0)), 0.0)
    qd = np.exp(log_g[:, None] * (idx + 1.0))[:, :, None]
    kd = np.exp(log_g[:, None] * (RET_C - 1.0 - idx))[:, :, None]
    return (jnp.asarray(dec, F32), jnp.asarray(qd, F32), jnp.asarray(kd, F32))


def kernel(x, mem, positions, g_ffn1, w_ffn1_in, w_ffn1_out, g_mix, w_in, g_diff_q, g_diff_k,
           lam_q1, lam_k1, lam_q2, lam_k2, g_diff_out, g_mem_q, g_mem_k, g_mem, w_mem_kv,
           w_br_ret, w_br_diff, w_br_mem, w_o, g_ffn2, w_ffn2_in, w_ffn2_out):
    batch, seq, _ = x.shape
    mlen = mem.shape[1]
    t = batch * seq
    depth = g_ffn1.shape[0]
    half = RET_DK // 2
    inv = jnp.asarray(ROPE_BASE ** (-np.arange(half, dtype=np.float64) / half), F32)[None, :]
    dec, qd, kd = _retention_constants()
    pos = positions.reshape(t, 1)
    mem2d = mem.reshape(batch * mlen, D_MODEL)
    xf = x.reshape(t, D_MODEL)
    bf = lambda w: w.astype(BF16)
    f32 = lambda w: w.astype(F32)
    vec = lambda g: g.astype(F32)[None, :]

    for l in range(depth):
        lam_init = 0.8 - 0.6 * math.exp(-0.3 * l)
        x1, h = _ffn(xf, vec(g_ffn1[l]), f32(w_ffn1_in[l]), f32(w_ffn1_out[l]), vec(g_mix[l]))

        w = f32(w_in[l])
        rqk, rv, rg = _proj_ret(h, w, pos, inv)
        qk_gain = jnp.concatenate([jnp.tile(g_diff_q[l].astype(F32), 2 * DIFF_HEADS) * (DIFF_D ** -0.5 * LOG2E),
                                   jnp.tile(g_diff_k[l].astype(F32), 2 * DIFF_HEADS)])[None, :]
        mq_gain = (jnp.tile(g_mem_q[l].astype(F32), MEM_HEADS) * (MEM_D ** -0.5 * LOG2E))[None, :]
        mk, mv = _memkv(mem2d, vec(g_mem[l]), bf(w_mem_kv[l]),
                        jnp.tile(g_mem_k[l].astype(F32), MEM_HEADS)[None, :], batch, mlen)
        dqk, dv, mo, gates = _proj_mix(h, w, qk_gain, mq_gain, mk, mv, seq)

        go = (g_diff_out[l].astype(F32) * (1.0 - lam_init))[None, :]
        dif = _diff_attention(dqk, dv, vec(lam_q1[l]), vec(lam_k1[l]), vec(lam_q2[l]),
                              vec(lam_k2[l]), go, batch, seq, lam_init)

        x2 = _ret_merge(x1, rqk, rv, rg, dif, mo, gates, dec, qd, kd, f32(w_br_ret[l]),
                        f32(w_br_diff[l]), f32(w_br_mem[l]), f32(w_o[l]), seq)
        (xf,) = _ffn(x2, vec(g_ffn2[l]), f32(w_ffn2_in[l]), f32(w_ffn2_out[l]))
    return xf.reshape(batch, seq, D_MODEL)
```

```python
import functools
import math

import jax
import jax.numpy as jnp
import numpy as np
from jax import lax
from jax.experimental import pallas as pl
from jax.experimental.pallas import tpu as pltpu

F32 = jnp.float32
BF16 = jnp.bfloat16

D_MODEL = 1024
D_FF = 2816
FFN_RES = 0.5
EPS = 1e-6
NEG = -1e30
LOG2E = math.log2(math.e)
ROPE_BASE = 10000.0

RET_HEADS = 4
RET_DK = 256
RET_DV = 512
DIFF_HEADS = 8
DIFF_D = 64
MEM_HEADS = 4
MEM_D = 256

V7X_VMEM_LIMIT_BYTES = 56 * 1024 * 1024

TM = 512
TM_FFN = 1024
TM_RET = 1024
FFN_CHUNK = 256
RET_C = 256
DIFF_TQ = 256
DIFF_HEADS_PER_STEP = 2


def _params(sem):
    return pltpu.CompilerParams(dimension_semantics=sem,
                                vmem_limit_bytes=V7X_VMEM_LIMIT_BYTES)


def _rms(x):
    return x * lax.rsqrt(jnp.mean(x * x, axis=-1, keepdims=True) + EPS)


def _after(x, *deps):
    folded = None
    for d in deps:
        t = jnp.sum(d.astype(F32), axis=0, keepdims=True)
        for c in range(t.shape[1] // 128):
            blk = t[:, c * 128:(c + 1) * 128]
            folded = blk if folded is None else folded + blk
    bits = lax.bitcast_convert_type(folded, jnp.uint32)
    zero = lax.shift_right_logical(lax.shift_right_logical(bits, jnp.uint32(16)), jnp.uint32(16))
    zero = zero.astype(F32).astype(x.dtype)
    zero = jnp.concatenate([zero] * (x.shape[1] // 128), axis=1)
    return jnp.concatenate([x[:16] + zero, x[16:]], axis=0)


def _dot(a, b):
    return jnp.dot(a, b, preferred_element_type=F32)


def _dot_nt(a, b):
    return lax.dot_general(a, b, (((1,), (1,)), ((), ())), preferred_element_type=F32)


def _dot_tn(a, b):
    return lax.dot_general(a, b, (((0,), (0,)), ((), ())), preferred_element_type=F32)


STAGE_BYTES = 1024 * 1024
STAGE_SLOTS = 4


def _stage_rows(width):
    return max(16, (STAGE_BYTES // (4 * width)) // 16 * 16)


def _weight_scratch(rows, width):
    return [pltpu.VMEM((rows, width), BF16),
            pltpu.VMEM((STAGE_SLOTS, _stage_rows(width), width), F32),
            pltpu.SemaphoreType.DMA((STAGE_SLOTS,))]


def _load_weight(w_hbm, col0, w_vmem, stage, sem):
    rows, width = w_vmem.shape
    slots, step = stage.shape[0], stage.shape[1]
    starts = list(range(0, rows, step))

    def copy(c):
        n = min(step, rows - starts[c])
        return pltpu.make_async_copy(
            w_hbm.at[pl.ds(starts[c], n), pl.ds(col0, width)],
            stage.at[c % slots, pl.ds(0, n), :], sem.at[c % slots])

    for c in range(min(slots - 1, len(starts))):
        copy(c).start()
    for c in range(len(starts)):
        if c + slots - 1 < len(starts):
            copy(c + slots - 1).start()
        copy(c).wait()
        n = min(step, rows - starts[c])
        w_vmem[starts[c]:starts[c] + n, :] = stage[c % slots, 0:n, :].astype(BF16)


_ANY = pl.BlockSpec(memory_space=pl.ANY)


def _ffn_kernel(*refs, emit_next):
    if emit_next:
        x_ref, g_ref, win_hbm, wout_hbm, gn_ref, o_ref, h_ref = refs[:7]
    else:
        x_ref, g_ref, win_hbm, wout_hbm, o_ref = refs[:5]
    win_ref, win_stage, win_sem, wout_ref, wout_stage, wout_sem = refs[-6:]

    @pl.when(pl.program_id(0) == 0)
    def _():
        _load_weight(win_hbm, 0, win_ref, win_stage, win_sem)
        _load_weight(wout_hbm, 0, wout_ref, wout_stage, wout_sem)

    x = x_ref[...]
    hb = (_rms(x) * g_ref[...]).astype(BF16)
    acc = jnp.zeros(x.shape, F32)
    for j in range(D_FF // FFN_CHUNK):
        lo = j * FFN_CHUNK
        a = _dot(hb, win_ref[:, lo:lo + FFN_CHUNK])
        b = _dot(hb, win_ref[:, D_FF + lo:D_FF + lo + FFN_CHUNK])
        act = (a * jax.nn.sigmoid(a) * b).astype(BF16)
        acc = acc + _dot(act, wout_ref[lo:lo + FFN_CHUNK, :])
    y = x + FFN_RES * acc
    o_ref[...] = y
    if emit_next:
        h_ref[...] = (_rms(y) * gn_ref[...]).astype(BF16)


def _ffn(x, g, w_in, w_out, g_next=None):
    t = x.shape[0]
    emit = g_next is not None
    row = pl.BlockSpec((TM_FFN, D_MODEL), lambda i: (i, 0))
    vec = pl.BlockSpec((1, D_MODEL), lambda i: (0, 0))
    in_specs = [row, vec, _ANY, _ANY]
    args = [x, g, w_in, w_out]
    out_shape = [jax.ShapeDtypeStruct((t, D_MODEL), F32)]
    out_specs = [row]
    if emit:
        in_specs.append(vec)
        args.append(g_next)
        out_shape.append(jax.ShapeDtypeStruct((t, D_MODEL), BF16))
        out_specs.append(row)
    return pl.pallas_call(
        functools.partial(_ffn_kernel, emit_next=emit),
        grid=(t // TM_FFN,),
        in_specs=in_specs,
        out_specs=out_specs,
        out_shape=out_shape,
        scratch_shapes=_weight_scratch(*w_in.shape) + _weight_scratch(*w_out.shape),
        compiler_params=_params(("arbitrary",)),
        name="ffn_next" if emit else "ffn",
    )(*args)


def _group_norm_128(y, gain, group):
    cols = y.shape[1]
    outs = []
    if group == 64:
        lane = lax.broadcasted_iota(jnp.int32, (1, 128), 1)
        first = lane < 64
        for c in range(cols // 128):
            blk = y[:, c * 128:(c + 1) * 128]
            sq = blk * blk
            s_lo = jnp.sum(jnp.where(first, sq, 0.0), axis=-1, keepdims=True)
            s_hi = jnp.sum(jnp.where(first, 0.0, sq), axis=-1, keepdims=True)
            r = jnp.where(first, lax.rsqrt(s_lo * (1.0 / 64) + EPS),
                          lax.rsqrt(s_hi * (1.0 / 64) + EPS))
            outs.append(blk * r)
    else:
        for c in range(cols // group):
            blk = y[:, c * group:(c + 1) * group]
            outs.append(_rms(blk))
    return jnp.concatenate(outs, axis=1) * gain


RET_QK_W = RET_HEADS * RET_DK
RET_V_W = RET_HEADS * RET_DV
PROJ_RET_W = 2 * RET_QK_W + 2 * RET_V_W
DIFF_W = DIFF_HEADS * 2 * DIFF_D
MEM_W = MEM_HEADS * MEM_D
PROJ_MIX_W = 2 * DIFF_W + DIFF_W + MEM_W + 3 * D_MODEL
PROJ_CHUNK = 512

def _proj_ret_kernel(h_ref, w_hbm, pos_ref, inv_ref, qk_ref, v_ref, g_ref, w_ref, w_stage, w_sem):
    @pl.when(pl.program_id(0) == 0)
    def _():
        _load_weight(w_hbm, 0, w_ref, w_stage, w_sem)

    h = h_ref[...]
    tm = h.shape[0]
    n_piece = 8
    rows = tm // n_piece
    cos_p, sin_p = [], []
    for r in range(n_piece):
        ang = pos_ref[r * rows:(r + 1) * rows, :].astype(F32) * inv_ref[...]
        cos_p.append(jnp.cos(ang))
        sin_p.append(jnp.sin(ang))
    cos = jnp.concatenate(cos_p, axis=0)
    sin = jnp.concatenate(sin_p, axis=0)
    half = RET_DK // 2

    def rot(lhs, hd):
        lo = hd * RET_DK
        y = _dot(lhs, w_ref[:, lo:lo + RET_DK])
        x1 = y[:, :half]
        x2 = y[:, half:]
        r1 = x1 * cos - x2 * sin
        r2 = x1 * sin + x2 * cos
        if hd >= RET_HEADS:
            r1 = r1 * (RET_DK ** -0.5)
            r2 = r2 * (RET_DK ** -0.5)
        r1 = r1.astype(BF16)
        r2 = r2.astype(BF16)
        qk_ref[:, lo:lo + half] = r1
        qk_ref[:, lo + half:lo + RET_DK] = r2
        return r1, r2

    def val(lhs, c):
        lo = c * PROJ_CHUNK
        v_ref[:, lo:lo + PROJ_CHUNK] = _dot(
            lhs, w_ref[:, 2 * RET_QK_W + lo:2 * RET_QK_W + lo + PROJ_CHUNK]).astype(BF16)

    def gate(lhs, c):
        lo = c * PROJ_CHUNK
        wlo = 2 * RET_QK_W + RET_V_W + lo
        y = _dot(lhs, w_ref[:, wlo:wlo + PROJ_CHUNK])
        g_ref[:, lo:lo + PROJ_CHUNK] = (y * jax.nn.sigmoid(y)).astype(BF16)

    plain = [(val, 0), (gate, 0), (val, 1), (gate, 1), (val, 2), (gate, 2), (val, 3), (gate, 3)]
    for r, (fn, idx) in enumerate(plain):
        fn(h if r == 0 else _after(h, cos_p[r - 1], sin_p[r - 1]), idx)
    for hd in range(2 * RET_HEADS):
        rot(h, hd)


def _proj_ret(h, w, pos, inv):
    t = h.shape[0]
    row = lambda width: pl.BlockSpec((TM_RET, width), lambda i: (i, 0))
    return pl.pallas_call(
        _proj_ret_kernel,
        grid=(t // TM_RET,),
        in_specs=[row(D_MODEL), _ANY, row(1), pl.BlockSpec((1, 128), lambda i: (0, 0))],
        out_specs=[row(2 * RET_QK_W), row(RET_V_W), row(RET_V_W)],
        out_shape=[jax.ShapeDtypeStruct((t, 2 * RET_QK_W), BF16),
                   jax.ShapeDtypeStruct((t, RET_V_W), BF16),
                   jax.ShapeDtypeStruct((t, RET_V_W), BF16)],
        scratch_shapes=_weight_scratch(D_MODEL, PROJ_RET_W),
        compiler_params=_params(("arbitrary",)),
        name="proj_ret",
    )(h, w, pos, inv)


def _proj_mix_kernel(h_ref, w_hbm, gqk_ref, gmq_ref, mem_ref, gmem_ref, gmk_ref, wkv_hbm,
                     qk_ref, v_ref, mo_ref, gt_ref,
                     w_ref, w_stage, w_sem, wkv_ref, wkv_stage, wkv_sem, mk_ref, mv_ref,
                     *, tiles_per_row):
    @pl.when(pl.program_id(0) == 0)
    def _():
        _load_weight(w_hbm, PROJ_RET_W, w_ref, w_stage, w_sem)
        _load_weight(wkv_hbm, 0, wkv_ref, wkv_stage, wkv_sem)

    @pl.when(pl.program_id(0) % tiles_per_row == 0)
    def _():
        mn = (_rms(mem_ref[...]) * gmem_ref[...]).astype(BF16)
        kv = _dot(mn, wkv_ref[...])
        mk_ref[...] = _group_norm_128(kv[:, :MEM_W], gmk_ref[...], MEM_D).astype(BF16)
        mv_ref[...] = kv[:, MEM_W:].astype(BF16)

    h = h_ref[...]
    off_qk = 0
    off_v = off_qk + 2 * DIFF_W
    off_mq = off_v + DIFF_W
    off_gt = off_mq + MEM_W

    mem_q, mem_p = {}, {}

    def qk_norm(lhs, c):
        lo = c * 256
        y = _dot(lhs, w_ref[:, off_qk + lo:off_qk + lo + 256])
        out = _group_norm_128(y, gqk_ref[:, lo:lo + 256], DIFF_D).astype(BF16)
        qk_ref[:, lo:lo + 256] = out
        return out

    def val(lhs, c):
        lo = c * PROJ_CHUNK
        out = _dot(lhs, w_ref[:, off_v + lo:off_v + lo + PROJ_CHUNK]).astype(BF16)
        v_ref[:, lo:lo + PROJ_CHUNK] = out
        return out

    def gate(lhs, c):
        lo = c * PROJ_CHUNK
        y = _dot(lhs, w_ref[:, off_gt + lo:off_gt + lo + PROJ_CHUNK])
        out = jax.nn.sigmoid(y).astype(BF16)
        gt_ref[:, lo:lo + PROJ_CHUNK] = out
        return out

    def mem_query(lhs, hd):
        sl = slice(hd * MEM_D, (hd + 1) * MEM_D)
        y = _dot(lhs, w_ref[:, off_mq + hd * MEM_D:off_mq + (hd + 1) * MEM_D])
        mem_q[hd] = (_rms(y) * gmq_ref[:, sl]).astype(BF16)
        return mem_q[hd]

    def mem_scores(lhs, hd):
        sl = slice(hd * MEM_D, (hd + 1) * MEM_D)
        s = _dot_nt(mem_q[hd], mk_ref[:, sl])
        p = jnp.exp2(s - jnp.max(s, axis=-1, keepdims=True))
        mem_p[hd] = (p.astype(BF16), jnp.sum(p, axis=-1, keepdims=True))
        return mem_p[hd][0]

    def mem_values(lhs, hd):
        sl = slice(hd * MEM_D, (hd + 1) * MEM_D)
        p, l = mem_p[hd]
        out = (_dot(p, mv_ref[:, sl]) / l).astype(BF16)
        mo_ref[:, sl] = out
        return out

    order = [(gate, 0), (mem_query, 0), (qk_norm, 0), (mem_scores, 0), (qk_norm, 1),
             (mem_values, 0), (gate, 1), (mem_query, 1), (qk_norm, 2), (mem_scores, 1),
             (qk_norm, 3), (mem_values, 1), (val, 0), (mem_query, 2), (qk_norm, 4),
             (mem_scores, 2), (qk_norm, 5), (mem_values, 2), (gate, 2), (mem_query, 3),
             (qk_norm, 6), (mem_scores, 3), (qk_norm, 7), (mem_values, 3), (val, 1),
             (gate, 3), (gate, 4), (gate, 5)]
    for fn, idx in order:
        fn(h, idx)


def _proj_mix(h, w, gqk, gmq, mem2d, g_mem, gmk, w_kv, seq):
    t = h.shape[0]
    mlen = mem2d.shape[0] // (t // seq)
    tiles_per_batch = seq // TM
    row = lambda width: pl.BlockSpec((TM, width), lambda i: (i, 0))
    vec = lambda width: pl.BlockSpec((1, width), lambda i: (0, 0))
    mem = pl.BlockSpec((mlen, D_MODEL), lambda i: (i // tiles_per_batch, 0))
    return pl.pallas_call(
        functools.partial(_proj_mix_kernel, tiles_per_row=tiles_per_batch),
        grid=(t // TM,),
        in_specs=[row(D_MODEL), _ANY, vec(2 * DIFF_W), vec(MEM_W), mem, vec(D_MODEL), vec(MEM_W),
                  _ANY],
        out_specs=[row(2 * DIFF_W), row(DIFF_W), row(MEM_W), row(3 * D_MODEL)],
        out_shape=[jax.ShapeDtypeStruct((t, 2 * DIFF_W), BF16),
                   jax.ShapeDtypeStruct((t, DIFF_W), BF16),
                   jax.ShapeDtypeStruct((t, MEM_W), BF16),
                   jax.ShapeDtypeStruct((t, 3 * D_MODEL), BF16)],
        scratch_shapes=_weight_scratch(D_MODEL, PROJ_MIX_W) + _weight_scratch(*w_kv.shape)
        + [pltpu.VMEM((mlen, MEM_W), BF16), pltpu.VMEM((mlen, MEM_W), BF16)],
        compiler_params=_params(("arbitrary",)),
        name="proj_mix",
    )(h, w, gqk, gmq, mem2d, g_mem, gmk, w_kv)


def _diff_kernel(q_ref, k_ref, v_ref, lq1_ref, lk1_ref, lq2_ref, lk2_ref, go_ref, o_ref,
                 *, lam_init, seq):
    tq = DIFF_TQ
    lane = lax.broadcasted_iota(jnp.int32, (1, 2 * DIFF_D), 1)
    first_map = lane < DIFF_D
    r = lax.broadcasted_iota(jnp.int32, (tq, tq), 0)
    cidx = lax.broadcasted_iota(jnp.int32, (tq, tq), 1)
    keep = cidx <= r
    keep = jnp.concatenate([keep, keep], axis=0)
    lam = (jnp.exp(jnp.sum(lq1_ref[...] * lk1_ref[...], axis=-1, keepdims=True))
           - jnp.exp(jnp.sum(lq2_ref[...] * lk2_ref[...], axis=-1, keepdims=True))
           + lam_init)
    go = go_ref[...]
    nq = seq // tq
    hw = 2 * DIFF_D
    heads = DIFF_HEADS_PER_STEP
    ones = jnp.ones((seq, hw), BF16)
    v_ext = [jnp.concatenate([v_ref[:, hd * hw:(hd + 1) * hw], ones], axis=1)
             for hd in range(heads)]

    def finish(hd, i, o_ext):
        r_ = o_ext[:, :hw] / o_ext[:, hw:]
        d = r_[:tq] - lam * r_[tq:]
        o_ref[i * tq:(i + 1) * tq, hd * hw:(hd + 1) * hw] = (_rms(d) * go).astype(BF16)

    prev = [None] * heads
    for i in list(range(nq)) + [None]:
        tiles = [[] for _ in range(heads)]
        run_max = [None] * heads
        acc = [None] * heads
        qs = [None] * heads
        if i is not None:
            for hd in range(heads):
                q = q_ref[i * tq:(i + 1) * tq, hd * hw:(hd + 1) * hw]
                zero = jnp.zeros_like(q)
                qs[hd] = jnp.concatenate([jnp.where(first_map, q, zero),
                                          jnp.where(first_map, zero, q)], axis=0)
        n_cur = 0 if i is None else i + 1
        n_prev = 0 if prev[0] is None else len(prev[0][1])
        for j in range(max(n_cur, n_prev)):
            for hd in range(heads):
                if j < n_cur:
                    s = _dot_nt(qs[hd], k_ref[j * tq:(j + 1) * tq, hd * hw:(hd + 1) * hw])
                    if j == i:
                        s = jnp.where(keep, s, NEG)
                    tiles[hd].append(s)
                    run_max[hd] = s if run_max[hd] is None else jnp.maximum(run_max[hd], s)
                if j < n_prev:
                    p = jnp.exp2(prev[hd][1][j] - prev[hd][2]).astype(BF16)
                    pv = _dot(p, v_ext[hd][j * tq:(j + 1) * tq, :])
                    acc[hd] = pv if acc[hd] is None else acc[hd] + pv
        for hd in range(heads):
            if prev[hd] is not None:
                finish(hd, prev[hd][0], acc[hd])
            prev[hd] = (None if i is None else
                        (i, tiles[hd], jnp.max(run_max[hd], axis=-1, keepdims=True)))


def _diff_attention(qk, v, lq1, lk1, lq2, lk2, g_out, batch, seq, lam_init):
    t = batch * seq
    width = DIFF_HEADS_PER_STEP * 2 * DIFF_D
    groups = DIFF_HEADS // DIFF_HEADS_PER_STEP
    vec64 = pl.BlockSpec((1, DIFF_D), lambda b, h: (0, 0))
    head = pl.BlockSpec((seq, width), lambda b, h: (b, h))
    return pl.pallas_call(
        functools.partial(_diff_kernel, lam_init=lam_init, seq=seq),
        grid=(batch, groups),
        in_specs=[head,
                  pl.BlockSpec((seq, width), lambda b, h: (b, groups + h)),
                  head,
                  vec64, vec64, vec64, vec64,
                  pl.BlockSpec((1, 2 * DIFF_D), lambda b, h: (0, 0))],
        out_specs=head,
        out_shape=jax.ShapeDtypeStruct((t, DIFF_HEADS * 2 * DIFF_D), BF16),
        compiler_params=_params(("parallel", "parallel")),
        name="diff_attention",
    )(qk, qk, v, lq1, lk1, lq2, lk2, g_out)


def _ret_merge_kernel(x_ref, qk_ref, v_ref, g_ref, dif_ref, mo_ref, gt_ref, dec_ref, qd_ref, kd_ref,
                      wr_hbm, wd_hbm, wm_hbm, wo_hbm, o_ref,
                      st_ref, ret_ref, wr_ref, wd_ref, wm_ref, wo_ref, stage, sem, *, tiles_per_row):
    i = pl.program_id(0)

    @pl.when(i == 0)
    def _():
        for src, dst in ((wr_hbm, wr_ref), (wd_hbm, wd_ref), (wm_hbm, wm_ref), (wo_hbm, wo_ref)):
            _load_weight(src, 0, dst, stage, sem)

    @pl.when(i % tiles_per_row == 0)
    def _():
        st_ref[...] = jnp.zeros_like(st_ref)

    tm = x_ref.shape[0]
    for c in range(tm // RET_C):
        lo = c * RET_C
        for hd in range(RET_HEADS):
            qd = qd_ref[hd]
            q = qk_ref[lo:lo + RET_C, hd * RET_DK:(hd + 1) * RET_DK]
            k = qk_ref[lo:lo + RET_C, RET_QK_W + hd * RET_DK:RET_QK_W + (hd + 1) * RET_DK]
            v = v_ref[lo:lo + RET_C, hd * RET_DV:(hd + 1) * RET_DV]
            st = st_ref[hd]
            s = _dot_nt(q, k) * dec_ref[hd]
            out = _dot(s.astype(BF16), v) + _dot(q, st.astype(BF16)) * qd
            upd = _dot_tn((k.astype(F32) * kd_ref[hd]).astype(BF16), v)
            st_ref[hd] = qd[RET_C - 1:RET_C, :] * st + upd
            gate = g_ref[lo:lo + RET_C, hd * RET_DV:(hd + 1) * RET_DV].astype(F32)
            ret_ref[lo:lo + RET_C, hd * RET_DV:(hd + 1) * RET_DV] = (_rms(out) * gate).astype(BF16)

    gt = gt_ref[...].astype(F32)
    y = (gt[:, :D_MODEL] * _dot(ret_ref[...], wr_ref[...])
         + gt[:, D_MODEL:2 * D_MODEL] * _dot(dif_ref[...], wd_ref[...])
         + gt[:, 2 * D_MODEL:] * _dot(mo_ref[...], wm_ref[...]))
    o_ref[...] = x_ref[...] + _dot(y.astype(BF16), wo_ref[...])


def _ret_merge(x, qk, v, gate, dif, mo, gates, dec, qd, kd, wr, wd, wm, wo, seq):
    t = x.shape[0]
    row = lambda w: pl.BlockSpec((TM, w), lambda i: (i, 0))
    const = lambda shape: pl.BlockSpec(shape, lambda i: (0,) * len(shape))
    return pl.pallas_call(
        functools.partial(_ret_merge_kernel, tiles_per_row=seq // TM),
        grid=(t // TM,),
        in_specs=[row(D_MODEL), row(2 * RET_QK_W), row(RET_V_W), row(RET_V_W), row(D_MODEL),
                  row(D_MODEL), row(3 * D_MODEL),
                  const(dec.shape), const(qd.shape), const(kd.shape),
                  _ANY, _ANY, _ANY, _ANY],
        out_specs=row(D_MODEL),
        out_shape=jax.ShapeDtypeStruct((t, D_MODEL), F32),
        scratch_shapes=[pltpu.VMEM((RET_HEADS, RET_DK, RET_DV), F32),
                        pltpu.VMEM((TM, RET_V_W), BF16),
                        pltpu.VMEM(wr.shape, BF16), pltpu.VMEM(wd.shape, BF16),
                        pltpu.VMEM(wm.shape, BF16), pltpu.VMEM(wo.shape, BF16)]
        + _weight_scratch(D_MODEL, D_MODEL)[1:],
        compiler_params=_params(("arbitrary",)),
        name="ret_merge",
    )(x, qk, v, gate, dif, mo, gates, dec, qd, kd, wr, wd, wm, wo)


def _retention_constants():
    h = np.arange(RET_HEADS, dtype=np.float64)
    log_g = np.log1p(-(2.0 ** (-5.0 - h)))
    idx = np.arange(RET_C, dtype=np.float64)
    dist = idx[:, None] - idx[None, :]
    dec = np.where(dist >= 0, np.exp(log_g[:, None, None] * np.maximum(dist, 0.0)), 0.0)
    qd = np.exp(log_g[:, None] * (idx + 1.0))[:, :, None]
    kd = np.exp(log_g[:, None] * (RET_C - 1.0 - idx))[:, :, None]
    return (jnp.asarray(dec, F32), jnp.asarray(qd, F32), jnp.asarray(kd, F32))


def kernel(x, mem, positions, g_ffn1, w_ffn1_in, w_ffn1_out, g_mix, w_in, g_diff_q, g_diff_k,
           lam_q1, lam_k1, lam_q2, lam_k2, g_diff_out, g_mem_q, g_mem_k, g_mem, w_mem_kv,
           w_br_ret, w_br_diff, w_br_mem, w_o, g_ffn2, w_ffn2_in, w_ffn2_out):
    batch, seq, _ = x.shape
    mlen = mem.shape[1]
    t = batch * seq
    depth = g_ffn1.shape[0]
    half = RET_DK // 2
    inv = jnp.asarray(ROPE_BASE ** (-np.arange(half, dtype=np.float64) / half), F32)[None, :]
    dec, qd, kd = _retention_constants()
    pos = positions.reshape(t, 1)
    mem2d = mem.reshape(batch * mlen, D_MODEL)
    xf = x.reshape(t, D_MODEL)
    f32 = lambda w: w.astype(F32)
    vec = lambda g: g.astype(F32)[None, :]

    for l in range(depth):
        lam_init = 0.8 - 0.6 * math.exp(-0.3 * l)
        x1, h = _ffn(xf, vec(g_ffn1[l]), f32(w_ffn1_in[l]), f32(w_ffn1_out[l]), vec(g_mix[l]))

        w = f32(w_in[l])
        rqk, rv, rg = _proj_ret(h, w, pos, inv)
        qk_gain = jnp.concatenate([jnp.tile(g_diff_q[l].astype(F32), 2 * DIFF_HEADS) * (DIFF_D ** -0.5 * LOG2E),
                                   jnp.tile(g_diff_k[l].astype(F32), 2 * DIFF_HEADS)])[None, :]
        mq_gain = (jnp.tile(g_mem_q[l].astype(F32), MEM_HEADS) * (MEM_D ** -0.5 * LOG2E))[None, :]
        mk_gain = jnp.tile(g_mem_k[l].astype(F32), MEM_HEADS)[None, :]
        dqk, dv, mo, gates = _proj_mix(h, w, qk_gain, mq_gain, mem2d, vec(g_mem[l]), mk_gain,
                                       f32(w_mem_kv[l]), seq)

        go = (g_diff_out[l].astype(F32) * (1.0 - lam_init))[None, :]
        dif = _diff_attention(dqk, dv, vec(lam_q1[l]), vec(lam_k1[l]), vec(lam_q2[l]),
                              vec(lam_k2[l]), go, batch, seq, lam_init)

        x2 = _ret_merge(x1, rqk, rv, rg, dif, mo, gates, dec, qd, kd, f32(w_br_ret[l]),
                        f32(w_br_diff[l]), f32(w_br_mem[l]), f32(w_o[l]), seq)
        (xf,) = _ffn(x2, vec(g_ffn2[l]), f32(w_ffn2_in[l]), f32(w_ffn2_out[l]))
    return xf.reshape(batch, seq, D_MODEL)
```

```python
import functools
import math

import jax
import jax.numpy as jnp
import numpy as np
from jax import lax
from jax.experimental import pallas as pl
from jax.experimental.pallas import tpu as pltpu

F32 = jnp.float32
BF16 = jnp.bfloat16

D_MODEL = 1024
D_FF = 2816
FFN_RES = 0.5
EPS = 1e-6
NEG = -1e30
LOG2E = math.log2(math.e)
ROPE_BASE = 10000.0

RET_HEADS = 4
RET_DK = 256
RET_DV = 512
DIFF_HEADS = 8
DIFF_D = 64
MEM_HEADS = 4
MEM_D = 256

V7X_VMEM_LIMIT_BYTES = 56 * 1024 * 1024
LANES = 128
BF16_ROW_TILE = 16

TM = 512
TM_FFN = 512
TM_RET = 1024
FFN_CHUNK = 256
RET_C = 256
DIFF_TQ = 256
DIFF_HEADS_PER_STEP = 2


def _params(sem):
    return pltpu.CompilerParams(dimension_semantics=sem,
                                vmem_limit_bytes=V7X_VMEM_LIMIT_BYTES)


def _rms(x):
    return x * lax.rsqrt(jnp.mean(x * x, axis=-1, keepdims=True) + EPS)


def _after(x, *deps):
    folded = None
    for d in deps:
        t = jnp.sum(d.astype(F32), axis=0, keepdims=True)
        for c in range(t.shape[1] // LANES):
            blk = t[:, c * LANES:(c + 1) * LANES]
            folded = blk if folded is None else folded + blk
    bits = lax.bitcast_convert_type(folded, jnp.uint32)
    zero = lax.shift_right_logical(lax.shift_right_logical(bits, jnp.uint32(16)), jnp.uint32(16))
    zero = zero.astype(F32).astype(x.dtype)
    zero = jnp.concatenate([zero] * (x.shape[1] // LANES), axis=1)
    return jnp.concatenate([x[:BF16_ROW_TILE] + zero, x[BF16_ROW_TILE:]], axis=0)


def _dot(a, b):
    return jnp.dot(a, b, preferred_element_type=F32)


def _dot_nt(a, b):
    return lax.dot_general(a, b, (((1,), (1,)), ((), ())), preferred_element_type=F32)


def _dot_tn(a, b):
    return lax.dot_general(a, b, (((0,), (0,)), ((), ())), preferred_element_type=F32)


STAGE_BYTES = 1024 * 1024
STAGE_SLOTS = 4


def _stage_rows(width):
    return max(16, (STAGE_BYTES // (4 * width)) // 16 * 16)


def _weight_scratch(rows, width):
    return [pltpu.VMEM((rows, width), BF16),
            pltpu.VMEM((STAGE_SLOTS, _stage_rows(width), width), F32),
            pltpu.SemaphoreType.DMA((STAGE_SLOTS,))]


def _load_weight(w_hbm, col0, w_vmem, stage, sem):
    rows, width = w_vmem.shape
    slots, step = stage.shape[0], stage.shape[1]
    starts = list(range(0, rows, step))

    def copy(c):
        n = min(step, rows - starts[c])
        return pltpu.make_async_copy(
            w_hbm.at[pl.ds(starts[c], n), pl.ds(col0, width)],
            stage.at[c % slots, pl.ds(0, n), :], sem.at[c % slots])

    for c in range(min(slots - 1, len(starts))):
        copy(c).start()
    for c in range(len(starts)):
        if c + slots - 1 < len(starts):
            copy(c + slots - 1).start()
        copy(c).wait()
        n = min(step, rows - starts[c])
        w_vmem[starts[c]:starts[c] + n, :] = stage[c % slots, 0:n, :].astype(BF16)


_ANY = pl.BlockSpec(memory_space=pl.ANY)


FFN_RING = 4


def _ffn_kernel(*refs, emit_next):
    if emit_next:
        x_ref, g_ref, win_hbm, wout_hbm, gn_ref, o_ref, h_ref = refs[:7]
    else:
        x_ref, g_ref, win_hbm, wout_hbm, o_ref = refs[:5]
    win_ref, wout_ref, sg_ref, su_ref, so_ref, sem = refs[-6:]
    n_chunks = D_FF // FFN_CHUNK

    def copies(j):
        lo, slot = j * FFN_CHUNK, j % FFN_RING
        return (pltpu.make_async_copy(win_hbm.at[:, pl.ds(lo, FFN_CHUNK)],
                                      sg_ref.at[slot], sem.at[slot, 0]),
                pltpu.make_async_copy(win_hbm.at[:, pl.ds(D_FF + lo, FFN_CHUNK)],
                                      su_ref.at[slot], sem.at[slot, 1]),
                pltpu.make_async_copy(wout_hbm.at[pl.ds(lo, FFN_CHUNK), :],
                                      so_ref.at[slot], sem.at[slot, 2]))

    def fetch_chunk(j):
        if j == 0:
            for c in range(min(FFN_RING - 1, n_chunks)):
                for cp in copies(c):
                    cp.start()
        if j + FFN_RING - 1 < n_chunks:
            for cp in copies(j + FFN_RING - 1):
                cp.start()
        for cp in copies(j):
            cp.wait()
        lo, slot = j * FFN_CHUNK, j % FFN_RING
        win_ref[:, lo:lo + FFN_CHUNK] = sg_ref[slot].astype(BF16)
        win_ref[:, D_FF + lo:D_FF + lo + FFN_CHUNK] = su_ref[slot].astype(BF16)
        wout_ref[lo:lo + FFN_CHUNK, :] = so_ref[slot].astype(BF16)

    def body(before_chunk):
        x = x_ref[...]
        hb = (_rms(x) * g_ref[...]).astype(BF16)
        acc = jnp.zeros(x.shape, F32)
        for j in range(n_chunks):
            before_chunk(j)
            lo = j * FFN_CHUNK
            a = _dot(hb, win_ref[:, lo:lo + FFN_CHUNK])
            b = _dot(hb, win_ref[:, D_FF + lo:D_FF + lo + FFN_CHUNK])
            act = (a * jax.nn.sigmoid(a) * b).astype(BF16)
            acc = acc + _dot(act, wout_ref[lo:lo + FFN_CHUNK, :])
        y = x + FFN_RES * acc
        o_ref[...] = y
        if emit_next:
            h_ref[...] = (_rms(y) * gn_ref[...]).astype(BF16)

    @pl.when(pl.program_id(0) == 0)
    def _():
        body(fetch_chunk)

    @pl.when(pl.program_id(0) != 0)
    def _():
        body(lambda j: None)


def _ffn(x, g, w_in, w_out, g_next=None):
    t = x.shape[0]
    emit = g_next is not None
    row = pl.BlockSpec((TM_FFN, D_MODEL), lambda i: (i, 0))
    vec = pl.BlockSpec((1, D_MODEL), lambda i: (0, 0))
    in_specs = [row, vec, _ANY, _ANY]
    args = [x, g, w_in, w_out]
    out_shape = [jax.ShapeDtypeStruct((t, D_MODEL), F32)]
    out_specs = [row]
    if emit:
        in_specs.append(vec)
        args.append(g_next)
        out_shape.append(jax.ShapeDtypeStruct((t, D_MODEL), BF16))
        out_specs.append(row)
    return pl.pallas_call(
        functools.partial(_ffn_kernel, emit_next=emit),
        grid=(t // TM_FFN,),
        in_specs=in_specs,
        out_specs=out_specs,
        out_shape=out_shape,
        scratch_shapes=[pltpu.VMEM(w_in.shape, BF16), pltpu.VMEM(w_out.shape, BF16),
                        pltpu.VMEM((FFN_RING, D_MODEL, FFN_CHUNK), F32),
                        pltpu.VMEM((FFN_RING, D_MODEL, FFN_CHUNK), F32),
                        pltpu.VMEM((FFN_RING, FFN_CHUNK, D_MODEL), F32),
                        pltpu.SemaphoreType.DMA((FFN_RING, 3))],
        compiler_params=_params(("arbitrary",)),
        name="ffn_next" if emit else "ffn",
    )(*args)


def _group_norm_128(y, gain, group):
    cols = y.shape[1]
    outs = []
    if 2 * group == LANES:
        lane = lax.broadcasted_iota(jnp.int32, (1, LANES), 1)
        first = lane < group
        for c in range(cols // LANES):
            blk = y[:, c * LANES:(c + 1) * LANES]
            sq = blk * blk
            s_lo = jnp.sum(jnp.where(first, sq, 0.0), axis=-1, keepdims=True)
            s_hi = jnp.sum(jnp.where(first, 0.0, sq), axis=-1, keepdims=True)
            r = jnp.where(first, lax.rsqrt(s_lo * (1.0 / group) + EPS),
                          lax.rsqrt(s_hi * (1.0 / group) + EPS))
            outs.append(blk * r)
    else:
        for c in range(cols // group):
            blk = y[:, c * group:(c + 1) * group]
            outs.append(_rms(blk))
    return jnp.concatenate(outs, axis=1) * gain


RET_QK_W = RET_HEADS * RET_DK
RET_V_W = RET_HEADS * RET_DV
PROJ_RET_W = 2 * RET_QK_W + 2 * RET_V_W
DIFF_W = DIFF_HEADS * 2 * DIFF_D
MEM_W = MEM_HEADS * MEM_D
PROJ_MIX_W = 2 * DIFF_W + DIFF_W + MEM_W + 3 * D_MODEL
PROJ_CHUNK = 512
QK_CHUNK = 256

def _proj_ret_kernel(h_ref, w_hbm, pos_ref, inv_ref, qk_ref, v_ref, g_ref, w_ref, w_stage, w_sem):
    @pl.when(pl.program_id(0) == 0)
    def _():
        _load_weight(w_hbm, 0, w_ref, w_stage, w_sem)

    h = h_ref[...]
    tm = h.shape[0]
    n_piece = 8
    rows = tm // n_piece
    cos_p, sin_p = [], []
    for r in range(n_piece):
        ang = pos_ref[r * rows:(r + 1) * rows, :].astype(F32) * inv_ref[...]
        cos_p.append(jnp.cos(ang))
        sin_p.append(jnp.sin(ang))
    cos = jnp.concatenate(cos_p, axis=0)
    sin = jnp.concatenate(sin_p, axis=0)
    half = RET_DK // 2

    def rot(lhs, hd):
        lo = hd * RET_DK
        y = _dot(lhs, w_ref[:, lo:lo + RET_DK])
        x1 = y[:, :half]
        x2 = y[:, half:]
        r1 = x1 * cos - x2 * sin
        r2 = x1 * sin + x2 * cos
        if hd >= RET_HEADS:
            r1 = r1 * (RET_DK ** -0.5)
            r2 = r2 * (RET_DK ** -0.5)
        r1 = r1.astype(BF16)
        r2 = r2.astype(BF16)
        qk_ref[:, lo:lo + half] = r1
        qk_ref[:, lo + half:lo + RET_DK] = r2
        return r1, r2

    def val(lhs, c):
        lo = c * PROJ_CHUNK
        v_ref[:, lo:lo + PROJ_CHUNK] = _dot(
            lhs, w_ref[:, 2 * RET_QK_W + lo:2 * RET_QK_W + lo + PROJ_CHUNK]).astype(BF16)

    def gate(lhs, c):
        lo = c * PROJ_CHUNK
        wlo = 2 * RET_QK_W + RET_V_W + lo
        y = _dot(lhs, w_ref[:, wlo:wlo + PROJ_CHUNK])
        g_ref[:, lo:lo + PROJ_CHUNK] = (y * jax.nn.sigmoid(y)).astype(BF16)

    plain = [(val, 0), (gate, 0), (val, 1), (gate, 1), (val, 2), (gate, 2), (val, 3), (gate, 3)]
    for r, (fn, idx) in enumerate(plain):
        fn(h if r == 0 else _after(h, cos_p[r - 1], sin_p[r - 1]), idx)
    for hd in range(2 * RET_HEADS):
        rot(h, hd)


def _proj_ret(h, w, pos, inv):
    t = h.shape[0]
    row = lambda width: pl.BlockSpec((TM_RET, width), lambda i: (i, 0))
    return pl.pallas_call(
        _proj_ret_kernel,
        grid=(t // TM_RET,),
        in_specs=[row(D_MODEL), _ANY, row(1), pl.BlockSpec((1, RET_DK // 2), lambda i: (0, 0))],
        out_specs=[row(2 * RET_QK_W), row(RET_V_W), row(RET_V_W)],
        out_shape=[jax.ShapeDtypeStruct((t, 2 * RET_QK_W), BF16),
                   jax.ShapeDtypeStruct((t, RET_V_W), BF16),
                   jax.ShapeDtypeStruct((t, RET_V_W), BF16)],
        scratch_shapes=_weight_scratch(D_MODEL, PROJ_RET_W),
        compiler_params=_params(("arbitrary",)),
        name="proj_ret",
    )(h, w, pos, inv)


def _proj_mix_kernel(h_ref, w_hbm, gqk_ref, gmq_ref, mem_ref, gmem_ref, gmk_ref, wkv_hbm,
                     qk_ref, v_ref, mo_ref, gt_ref,
                     w_ref, w_stage, w_sem, wkv_ref, wkv_stage, wkv_sem, mk_ref, mv_ref,
                     *, tiles_per_row):
    @pl.when(pl.program_id(0) == 0)
    def _():
        _load_weight(w_hbm, PROJ_RET_W, w_ref, w_stage, w_sem)
        _load_weight(wkv_hbm, 0, wkv_ref, wkv_stage, wkv_sem)

    @pl.when(pl.program_id(0) % tiles_per_row == 0)
    def _():
        mn = (_rms(mem_ref[...]) * gmem_ref[...]).astype(BF16)
        kv = _dot(mn, wkv_ref[...])
        mk_ref[...] = _group_norm_128(kv[:, :MEM_W], gmk_ref[...], MEM_D).astype(BF16)
        mv_ref[...] = kv[:, MEM_W:].astype(BF16)

    h = h_ref[...]
    off_qk = 0
    off_v = off_qk + 2 * DIFF_W
    off_mq = off_v + DIFF_W
    off_gt = off_mq + MEM_W

    mem_q, mem_p = {}, {}

    def qk_norm(lhs, c):
        lo = c * QK_CHUNK
        y = _dot(lhs, w_ref[:, off_qk + lo:off_qk + lo + QK_CHUNK])
        out = _group_norm_128(y, gqk_ref[:, lo:lo + QK_CHUNK], DIFF_D).astype(BF16)
        qk_ref[:, lo:lo + QK_CHUNK] = out
        return out

    def val(lhs, c):
        lo = c * PROJ_CHUNK
        out = _dot(lhs, w_ref[:, off_v + lo:off_v + lo + PROJ_CHUNK]).astype(BF16)
        v_ref[:, lo:lo + PROJ_CHUNK] = out
        return out

    def gate(lhs, c):
        lo = c * PROJ_CHUNK
        y = _dot(lhs, w_ref[:, off_gt + lo:off_gt + lo + PROJ_CHUNK])
        out = jax.nn.sigmoid(y).astype(BF16)
        gt_ref[:, lo:lo + PROJ_CHUNK] = out
        return out

    def mem_query(lhs, hd):
        sl = slice(hd * MEM_D, (hd + 1) * MEM_D)
        y = _dot(lhs, w_ref[:, off_mq + hd * MEM_D:off_mq + (hd + 1) * MEM_D])
        mem_q[hd] = (_rms(y) * gmq_ref[:, sl]).astype(BF16)
        return mem_q[hd]

    def mem_scores(lhs, hd):
        sl = slice(hd * MEM_D, (hd + 1) * MEM_D)
        s = _dot_nt(mem_q[hd], mk_ref[:, sl])
        p = jnp.exp2(s - jnp.max(s, axis=-1, keepdims=True))
        mem_p[hd] = (p.astype(BF16), jnp.sum(p, axis=-1, keepdims=True))
        return mem_p[hd][0]

    def mem_values(lhs, hd):
        sl = slice(hd * MEM_D, (hd + 1) * MEM_D)
        p, l = mem_p[hd]
        out = (_dot(p, mv_ref[:, sl]) / l).astype(BF16)
        mo_ref[:, sl] = out
        return out

    order = [(gate, 0), (mem_query, 0), (qk_norm, 0), (mem_scores, 0), (qk_norm, 1),
             (mem_values, 0), (gate, 1), (mem_query, 1), (qk_norm, 2), (mem_scores, 1),
             (qk_norm, 3), (mem_values, 1), (val, 0), (mem_query, 2), (qk_norm, 4),
             (mem_scores, 2), (qk_norm, 5), (mem_values, 2), (gate, 2), (mem_query, 3),
             (qk_norm, 6), (mem_scores, 3), (qk_norm, 7), (mem_values, 3), (val, 1),
             (gate, 3), (gate, 4), (gate, 5)]
    for fn, idx in order:
        fn(h, idx)


def _proj_mix(h, w, gqk, gmq, mem2d, g_mem, gmk, w_kv, seq):
    t = h.shape[0]
    mlen = mem2d.shape[0] // (t // seq)
    tiles_per_batch = seq // TM
    row = lambda width: pl.BlockSpec((TM, width), lambda i: (i, 0))
    vec = lambda width: pl.BlockSpec((1, width), lambda i: (0, 0))
    mem = pl.BlockSpec((mlen, D_MODEL), lambda i: (i // tiles_per_batch, 0))
    return pl.pallas_call(
        functools.partial(_proj_mix_kernel, tiles_per_row=tiles_per_batch),
        grid=(t // TM,),
        in_specs=[row(D_MODEL), _ANY, vec(2 * DIFF_W), vec(MEM_W), mem, vec(D_MODEL), vec(MEM_W),
                  _ANY],
        out_specs=[row(2 * DIFF_W), row(DIFF_W), row(MEM_W), row(3 * D_MODEL)],
        out_shape=[jax.ShapeDtypeStruct((t, 2 * DIFF_W), BF16),
                   jax.ShapeDtypeStruct((t, DIFF_W), BF16),
                   jax.ShapeDtypeStruct((t, MEM_W), BF16),
                   jax.ShapeDtypeStruct((t, 3 * D_MODEL), BF16)],
        scratch_shapes=_weight_scratch(D_MODEL, PROJ_MIX_W) + _weight_scratch(*w_kv.shape)
        + [pltpu.VMEM((mlen, MEM_W), BF16), pltpu.VMEM((mlen, MEM_W), BF16)],
        compiler_params=_params(("arbitrary",)),
        name="proj_mix",
    )(h, w, gqk, gmq, mem2d, g_mem, gmk, w_kv)


def _diff_kernel(q_ref, k_ref, v_ref, lq1_ref, lk1_ref, lq2_ref, lk2_ref, go_ref, o_ref,
                 *, lam_init, seq):
    tq = DIFF_TQ
    lane = lax.broadcasted_iota(jnp.int32, (1, 2 * DIFF_D), 1)
    first_map = lane < DIFF_D
    r = lax.broadcasted_iota(jnp.int32, (tq, tq), 0)
    cidx = lax.broadcasted_iota(jnp.int32, (tq, tq), 1)
    keep = cidx <= r
    keep = jnp.concatenate([keep, keep], axis=0)
    lam = (jnp.exp(jnp.sum(lq1_ref[...] * lk1_ref[...], axis=-1, keepdims=True))
           - jnp.exp(jnp.sum(lq2_ref[...] * lk2_ref[...], axis=-1, keepdims=True))
           + lam_init)
    go = go_ref[...]
    nq = seq // tq
    hw = 2 * DIFF_D
    heads = DIFF_HEADS_PER_STEP
    ones = jnp.ones((seq, hw), BF16)
    v_ext = [jnp.concatenate([v_ref[:, hd * hw:(hd + 1) * hw], ones], axis=1)
             for hd in range(heads)]

    def finish(hd, i, o_ext):
        r_ = o_ext[:, :hw] / o_ext[:, hw:]
        d = r_[:tq] - lam * r_[tq:]
        o_ref[i * tq:(i + 1) * tq, hd * hw:(hd + 1) * hw] = (_rms(d) * go).astype(BF16)

    prev = [None] * heads
    for i in list(range(nq)) + [None]:
        tiles = [[] for _ in range(heads)]
        run_max = [None] * heads
        acc = [None] * heads
        qs = [None] * heads
        if i is not None:
            for hd in range(heads):
                q = q_ref[i * tq:(i + 1) * tq, hd * hw:(hd + 1) * hw]
                zero = jnp.zeros_like(q)
                qs[hd] = jnp.concatenate([jnp.where(first_map, q, zero),
                                          jnp.where(first_map, zero, q)], axis=0)
        n_cur = 0 if i is None else i + 1
        n_prev = 0 if prev[0] is None else len(prev[0][1])
        for j in range(max(n_cur, n_prev)):
            for hd in range(heads):
                if j < n_cur:
                    s = _dot_nt(qs[hd], k_ref[j * tq:(j + 1) * tq, hd * hw:(hd + 1) * hw])
                    if j == i:
                        s = jnp.where(keep, s, NEG)
                    tiles[hd].append(s)
                    run_max[hd] = s if run_max[hd] is None else jnp.maximum(run_max[hd], s)
                if j < n_prev:
                    p = jnp.exp2(prev[hd][1][j] - prev[hd][2]).astype(BF16)
                    pv = _dot(p, v_ext[hd][j * tq:(j + 1) * tq, :])
                    acc[hd] = pv if acc[hd] is None else acc[hd] + pv
        for hd in range(heads):
            if prev[hd] is not None:
                finish(hd, prev[hd][0], acc[hd])
            prev[hd] = (None if i is None else
                        (i, tiles[hd], jnp.max(run_max[hd], axis=-1, keepdims=True)))


def _diff_attention(qk, v, lq1, lk1, lq2, lk2, g_out, batch, seq, lam_init):
    t = batch * seq
    width = DIFF_HEADS_PER_STEP * 2 * DIFF_D
    groups = DIFF_HEADS // DIFF_HEADS_PER_STEP
    vec64 = pl.BlockSpec((1, DIFF_D), lambda b, h: (0, 0))
    head = pl.BlockSpec((seq, width), lambda b, h: (b, h))
    return pl.pallas_call(
        functools.partial(_diff_kernel, lam_init=lam_init, seq=seq),
        grid=(batch, groups),
        in_specs=[head,
                  pl.BlockSpec((seq, width), lambda b, h: (b, groups + h)),
                  head,
                  vec64, vec64, vec64, vec64,
                  pl.BlockSpec((1, 2 * DIFF_D), lambda b, h: (0, 0))],
        out_specs=head,
        out_shape=jax.ShapeDtypeStruct((t, DIFF_HEADS * 2 * DIFF_D), BF16),
        compiler_params=_params(("parallel", "parallel")),
        name="diff_attention",
    )(qk, qk, v, lq1, lk1, lq2, lk2, g_out)


def _ret_merge_kernel(x_ref, qk_ref, v_ref, g_ref, dif_ref, mo_ref, gt_ref, dec_ref, qd_ref, kd_ref,
                      wr_hbm, wd_hbm, wm_hbm, wo_hbm, o_ref,
                      st_ref, ret_ref, wr_ref, wd_ref, wm_ref, wo_ref, stage, sem, *, tiles_per_row):
    i = pl.program_id(0)

    @pl.when(i == 0)
    def _():
        for src, dst in ((wr_hbm, wr_ref), (wd_hbm, wd_ref), (wm_hbm, wm_ref), (wo_hbm, wo_ref)):
            _load_weight(src, 0, dst, stage, sem)

    @pl.when(i % tiles_per_row == 0)
    def _():
        st_ref[...] = jnp.zeros_like(st_ref)

    tm = x_ref.shape[0]
    for c in range(tm // RET_C):
        lo = c * RET_C
        for hd in range(RET_HEADS):
            qd = qd_ref[hd]
            q = qk_ref[lo:lo + RET_C, hd * RET_DK:(hd + 1) * RET_DK]
            k = qk_ref[lo:lo + RET_C, RET_QK_W + hd * RET_DK:RET_QK_W + (hd + 1) * RET_DK]
            v = v_ref[lo:lo + RET_C, hd * RET_DV:(hd + 1) * RET_DV]
            st = st_ref[hd]
            s = _dot_nt(q, k) * dec_ref[hd]
            out = _dot(s.astype(BF16), v) + _dot(q, st.astype(BF16)) * qd
            upd = _dot_tn((k.astype(F32) * kd_ref[hd]).astype(BF16), v)
            st_ref[hd] = qd[RET_C - 1:RET_C, :] * st + upd
            gate = g_ref[lo:lo + RET_C, hd * RET_DV:(hd + 1) * RET_DV].astype(F32)
            ret_ref[lo:lo + RET_C, hd * RET_DV:(hd + 1) * RET_DV] = (_rms(out) * gate).astype(BF16)

    gt = gt_ref[...].astype(F32)
    y = (gt[:, :D_MODEL] * _dot(ret_ref[...], wr_ref[...])
         + gt[:, D_MODEL:2 * D_MODEL] * _dot(dif_ref[...], wd_ref[...])
         + gt[:, 2 * D_MODEL:] * _dot(mo_ref[...], wm_ref[...]))
    o_ref[...] = x_ref[...] + _dot(y.astype(BF16), wo_ref[...])


def _ret_merge(x, qk, v, gate, dif, mo, gates, dec, qd, kd, wr, wd, wm, wo, seq):
    t = x.shape[0]
    row = lambda w: pl.BlockSpec((TM, w), lambda i: (i, 0))
    const = lambda shape: pl.BlockSpec(shape, lambda i: (0,) * len(shape))
    return pl.pallas_call(
        functools.partial(_ret_merge_kernel, tiles_per_row=seq // TM),
        grid=(t // TM,),
        in_specs=[row(D_MODEL), row(2 * RET_QK_W), row(RET_V_W), row(RET_V_W), row(D_MODEL),
                  row(D_MODEL), row(3 * D_MODEL),
                  const(dec.shape), const(qd.shape), const(kd.shape),
                  _ANY, _ANY, _ANY, _ANY],
        out_specs=row(D_MODEL),
        out_shape=jax.ShapeDtypeStruct((t, D_MODEL), F32),
        scratch_shapes=[pltpu.VMEM((RET_HEADS, RET_DK, RET_DV), F32),
                        pltpu.VMEM((TM, RET_V_W), BF16),
                        pltpu.VMEM(wr.shape, BF16), pltpu.VMEM(wd.shape, BF16),
                        pltpu.VMEM(wm.shape, BF16), pltpu.VMEM(wo.shape, BF16)]
        + _weight_scratch(D_MODEL, D_MODEL)[1:],
        compiler_params=_params(("arbitrary",)),
        name="ret_merge",
    )(x, qk, v, gate, dif, mo, gates, dec, qd, kd, wr, wd, wm, wo)


def _retention_constants():
    h = np.arange(RET_HEADS, dtype=np.float64)
    log_g = np.log1p(-(2.0 ** (-5.0 - h)))
    idx = np.arange(RET_C, dtype=np.float64)
    dist = idx[:, None] - idx[None, :]
    dec = np.where(dist >= 0, np.exp(log_g[:, None, None] * np.maximum(dist, 0.0)), 0.0)
    qd = np.exp(log_g[:, None] * (idx + 1.0))[:, :, None]
    kd = np.exp(log_g[:, None] * (RET_C - 1.0 - idx))[:, :, None]
    return (jnp.asarray(dec, F32), jnp.asarray(qd, F32), jnp.asarray(kd, F32))


def kernel(x, mem, positions, g_ffn1, w_ffn1_in, w_ffn1_out, g_mix, w_in, g_diff_q, g_diff_k,
           lam_q1, lam_k1, lam_q2, lam_k2, g_diff_out, g_mem_q, g_mem_k, g_mem, w_mem_kv,
           w_br_ret, w_br_diff, w_br_mem, w_o, g_ffn2, w_ffn2_in, w_ffn2_out):
    batch, seq, _ = x.shape
    mlen = mem.shape[1]
    t = batch * seq
    depth = g_ffn1.shape[0]
    half = RET_DK // 2
    inv = jnp.asarray(ROPE_BASE ** (-np.arange(half, dtype=np.float64) / half), F32)[None, :]
    dec, qd, kd = _retention_constants()
    pos = positions.reshape(t, 1)
    mem2d = mem.reshape(batch * mlen, D_MODEL)
    xf = x.reshape(t, D_MODEL)
    f32 = lambda w: w.astype(F32)
    vec = lambda g: g.astype(F32)[None, :]

    for l in range(depth):
        lam_init = 0.8 - 0.6 * math.exp(-0.3 * l)
        x1, h = _ffn(xf, vec(g_ffn1[l]), f32(w_ffn1_in[l]), f32(w_ffn1_out[l]), vec(g_mix[l]))

        w = f32(w_in[l])
        rqk, rv, rg = _proj_ret(h, w, pos, inv)
        qk_gain = jnp.concatenate([jnp.tile(g_diff_q[l].astype(F32), 2 * DIFF_HEADS) * (DIFF_D ** -0.5 * LOG2E),
                                   jnp.tile(g_diff_k[l].astype(F32), 2 * DIFF_HEADS)])[None, :]
        mq_gain = (jnp.tile(g_mem_q[l].astype(F32), MEM_HEADS) * (MEM_D ** -0.5 * LOG2E))[None, :]
        mk_gain = jnp.tile(g_mem_k[l].astype(F32), MEM_HEADS)[None, :]
        dqk, dv, mo, gates = _proj_mix(h, w, qk_gain, mq_gain, mem2d, vec(g_mem[l]), mk_gain,
                                       f32(w_mem_kv[l]), seq)

        go = (g_diff_out[l].astype(F32) * (1.0 - lam_init))[None, :]
        dif = _diff_attention(dqk, dv, vec(lam_q1[l]), vec(lam_k1[l]), vec(lam_q2[l]),
                              vec(lam_k2[l]), go, batch, seq, lam_init)

        x2 = _ret_merge(x1, rqk, rv, rg, dif, mo, gates, dec, qd, kd, f32(w_br_ret[l]),
                        f32(w_br_diff[l]), f32(w_br_mem[l]), f32(w_o[l]), seq)
        (xf,) = _ffn(x2, vec(g_ffn2[l]), f32(w_ffn2_in[l]), f32(w_ffn2_out[l]))
    return xf.reshape(batch, seq, D_MODEL)
```

```python
import functools
import math

import jax
import jax.numpy as jnp
import numpy as np
from jax import lax
from jax.experimental import pallas as pl
from jax.experimental.pallas import tpu as pltpu

F32 = jnp.float32
BF16 = jnp.bfloat16

D_MODEL = 1024
D_FF = 2816
FFN_RES = 0.5
EPS = 1e-6
NEG = -1e30
LOG2E = math.log2(math.e)
ROPE_BASE = 10000.0

RET_HEADS = 4
RET_DK = 256
RET_DV = 512
DIFF_HEADS = 8
DIFF_D = 64
MEM_HEADS = 4
MEM_D = 256

V7X_VMEM_LIMIT_BYTES = 56 * 1024 * 1024
LANES = 128
BF16_ROW_TILE = 16

TM = 512
TM_FFN = 512
TM_RET = 1024
FFN_CHUNK = 256
RET_C = 256
DIFF_TQ = 256
DIFF_HEADS_PER_STEP = 2


def _params(sem):
    return pltpu.CompilerParams(dimension_semantics=sem,
                                vmem_limit_bytes=V7X_VMEM_LIMIT_BYTES)


def _rms(x):
    return x * lax.rsqrt(jnp.mean(x * x, axis=-1, keepdims=True) + EPS)


def _after(x, *deps):
    folded = None
    for d in deps:
        t = jnp.sum(d.astype(F32), axis=0, keepdims=True)
        for c in range(t.shape[1] // LANES):
            blk = t[:, c * LANES:(c + 1) * LANES]
            folded = blk if folded is None else folded + blk
    bits = lax.bitcast_convert_type(folded, jnp.uint32)
    zero = lax.shift_right_logical(lax.shift_right_logical(bits, jnp.uint32(16)), jnp.uint32(16))
    zero = zero.astype(F32).astype(x.dtype)
    zero = jnp.concatenate([zero] * (x.shape[1] // LANES), axis=1)
    return jnp.concatenate([x[:BF16_ROW_TILE] + zero, x[BF16_ROW_TILE:]], axis=0)


def _dot(a, b):
    return jnp.dot(a, b, preferred_element_type=F32)


def _dot_nt(a, b):
    return lax.dot_general(a, b, (((1,), (1,)), ((), ())), preferred_element_type=F32)


def _dot_tn(a, b):
    return lax.dot_general(a, b, (((0,), (0,)), ((), ())), preferred_element_type=F32)


STAGE_BYTES = 1024 * 1024
STAGE_SLOTS = 4


def _stage_rows(width):
    return max(16, (STAGE_BYTES // (4 * width)) // 16 * 16)


def _weight_scratch(rows, width):
    return [pltpu.VMEM((rows, width), BF16),
            pltpu.VMEM((STAGE_SLOTS, _stage_rows(width), width), F32),
            pltpu.SemaphoreType.DMA((STAGE_SLOTS,))]


def _load_weight(w_hbm, col0, w_vmem, stage, sem):
    rows, width = w_vmem.shape
    slots, step = stage.shape[0], stage.shape[1]
    starts = list(range(0, rows, step))

    def copy(c):
        n = min(step, rows - starts[c])
        return pltpu.make_async_copy(
            w_hbm.at[pl.ds(starts[c], n), pl.ds(col0, width)],
            stage.at[c % slots, pl.ds(0, n), :], sem.at[c % slots])

    for c in range(min(slots - 1, len(starts))):
        copy(c).start()
    for c in range(len(starts)):
        if c + slots - 1 < len(starts):
            copy(c + slots - 1).start()
        copy(c).wait()
        n = min(step, rows - starts[c])
        w_vmem[starts[c]:starts[c] + n, :] = stage[c % slots, 0:n, :].astype(BF16)


_ANY = pl.BlockSpec(memory_space=pl.ANY)


FFN_RING = 3


def _ffn_kernel(*refs, emit_next):
    if emit_next:
        x_ref, g_ref, win_hbm, wout_hbm, gn_ref, o_ref, h_ref = refs[:7]
    else:
        x_ref, g_ref, win_hbm, wout_hbm, o_ref = refs[:5]
    win_ref, wout_ref, sg_ref, su_ref, so_ref, sem = refs[-6:]
    n_chunks = D_FF // FFN_CHUNK

    def copies(j):
        lo, slot = j * FFN_CHUNK, j % FFN_RING
        return (pltpu.make_async_copy(win_hbm.at[:, pl.ds(lo, FFN_CHUNK)],
                                      sg_ref.at[slot], sem.at[slot, 0]),
                pltpu.make_async_copy(win_hbm.at[:, pl.ds(D_FF + lo, FFN_CHUNK)],
                                      su_ref.at[slot], sem.at[slot, 1]),
                pltpu.make_async_copy(wout_hbm.at[pl.ds(lo, FFN_CHUNK), :],
                                      so_ref.at[slot], sem.at[slot, 2]))

    def fetch_chunk(j):
        if j == 0:
            for c in range(min(FFN_RING - 1, n_chunks)):
                for cp in copies(c):
                    cp.start()
        if j + FFN_RING - 1 < n_chunks:
            for cp in copies(j + FFN_RING - 1):
                cp.start()
        for cp in copies(j):
            cp.wait()
        lo, slot = j * FFN_CHUNK, j % FFN_RING
        win_ref[:, lo:lo + FFN_CHUNK] = sg_ref[slot].astype(BF16)
        win_ref[:, D_FF + lo:D_FF + lo + FFN_CHUNK] = su_ref[slot].astype(BF16)
        wout_ref[lo:lo + FFN_CHUNK, :] = so_ref[slot].astype(BF16)

    def body(before_chunk):
        x = x_ref[...]
        hb = (_rms(x) * g_ref[...]).astype(BF16)
        acc = jnp.zeros(x.shape, F32)
        for j in range(n_chunks):
            before_chunk(j)
            lo = j * FFN_CHUNK
            a = _dot(hb, win_ref[:, lo:lo + FFN_CHUNK])
            b = _dot(hb, win_ref[:, D_FF + lo:D_FF + lo + FFN_CHUNK])
            act = (a * jax.nn.sigmoid(a) * b).astype(BF16)
            acc = acc + _dot(act, wout_ref[lo:lo + FFN_CHUNK, :])
        y = x + FFN_RES * acc
        o_ref[...] = y
        if emit_next:
            h_ref[...] = (_rms(y) * gn_ref[...]).astype(BF16)

    @pl.when(pl.program_id(0) == 0)
    def _():
        body(fetch_chunk)

    @pl.when(pl.program_id(0) != 0)
    def _():
        body(lambda j: None)


def _ffn(x, g, w_in, w_out, g_next=None):
    t = x.shape[0]
    emit = g_next is not None
    row = pl.BlockSpec((TM_FFN, D_MODEL), lambda i: (i, 0))
    vec = pl.BlockSpec((1, D_MODEL), lambda i: (0, 0))
    in_specs = [row, vec, _ANY, _ANY]
    args = [x, g, w_in, w_out]
    out_shape = [jax.ShapeDtypeStruct((t, D_MODEL), F32)]
    out_specs = [row]
    if emit:
        in_specs.append(vec)
        args.append(g_next)
        out_shape.append(jax.ShapeDtypeStruct((t, D_MODEL), BF16))
        out_specs.append(row)
    return pl.pallas_call(
        functools.partial(_ffn_kernel, emit_next=emit),
        grid=(t // TM_FFN,),
        in_specs=in_specs,
        out_specs=out_specs,
        out_shape=out_shape,
        scratch_shapes=[pltpu.VMEM(w_in.shape, BF16), pltpu.VMEM(w_out.shape, BF16),
                        pltpu.VMEM((FFN_RING, D_MODEL, FFN_CHUNK), F32),
                        pltpu.VMEM((FFN_RING, D_MODEL, FFN_CHUNK), F32),
                        pltpu.VMEM((FFN_RING, FFN_CHUNK, D_MODEL), F32),
                        pltpu.SemaphoreType.DMA((FFN_RING, 3))],
        compiler_params=_params(("arbitrary",)),
        name="ffn_next" if emit else "ffn",
    )(*args)


def _group_norm_128(y, gain, group):
    cols = y.shape[1]
    outs = []
    if 2 * group == LANES:
        lane = lax.broadcasted_iota(jnp.int32, (1, LANES), 1)
        first = lane < group
        for c in range(cols // LANES):
            blk = y[:, c * LANES:(c + 1) * LANES]
            sq = blk * blk
            s_lo = jnp.sum(jnp.where(first, sq, 0.0), axis=-1, keepdims=True)
            s_hi = jnp.sum(jnp.where(first, 0.0, sq), axis=-1, keepdims=True)
            r = jnp.where(first, lax.rsqrt(s_lo * (1.0 / group) + EPS),
                          lax.rsqrt(s_hi * (1.0 / group) + EPS))
            outs.append(blk * r)
    else:
        for c in range(cols // group):
            blk = y[:, c * group:(c + 1) * group]
            outs.append(_rms(blk))
    return jnp.concatenate(outs, axis=1) * gain


RET_QK_W = RET_HEADS * RET_DK
RET_V_W = RET_HEADS * RET_DV
PROJ_RET_W = 2 * RET_QK_W + 2 * RET_V_W
DIFF_W = DIFF_HEADS * 2 * DIFF_D
MEM_W = MEM_HEADS * MEM_D
PROJ_MIX_W = 2 * DIFF_W + DIFF_W + MEM_W + 3 * D_MODEL
PROJ_CHUNK = 512
QK_CHUNK = 256

def _proj_ret_kernel(h_ref, w_hbm, pos_ref, inv_ref, qk_ref, v_ref, g_ref, w_ref, w_stage, w_sem):
    @pl.when(pl.program_id(0) == 0)
    def _():
        _load_weight(w_hbm, 0, w_ref, w_stage, w_sem)

    h = h_ref[...]
    tm = h.shape[0]
    n_piece = 8
    rows = tm // n_piece
    cos_p, sin_p = [], []
    for r in range(n_piece):
        ang = pos_ref[r * rows:(r + 1) * rows, :].astype(F32) * inv_ref[...]
        cos_p.append(jnp.cos(ang))
        sin_p.append(jnp.sin(ang))
    cos = jnp.concatenate(cos_p, axis=0)
    sin = jnp.concatenate(sin_p, axis=0)
    half = RET_DK // 2

    def rot(lhs, hd):
        lo = hd * RET_DK
        y = _dot(lhs, w_ref[:, lo:lo + RET_DK])
        x1 = y[:, :half]
        x2 = y[:, half:]
        r1 = x1 * cos - x2 * sin
        r2 = x1 * sin + x2 * cos
        if hd >= RET_HEADS:
            r1 = r1 * (RET_DK ** -0.5)
            r2 = r2 * (RET_DK ** -0.5)
        r1 = r1.astype(BF16)
        r2 = r2.astype(BF16)
        qk_ref[:, lo:lo + half] = r1
        qk_ref[:, lo + half:lo + RET_DK] = r2
        return r1, r2

    def val(lhs, c):
        lo = c * PROJ_CHUNK
        v_ref[:, lo:lo + PROJ_CHUNK] = _dot(
            lhs, w_ref[:, 2 * RET_QK_W + lo:2 * RET_QK_W + lo + PROJ_CHUNK]).astype(BF16)

    def gate(lhs, c):
        lo = c * PROJ_CHUNK
        wlo = 2 * RET_QK_W + RET_V_W + lo
        y = _dot(lhs, w_ref[:, wlo:wlo + PROJ_CHUNK])
        g_ref[:, lo:lo + PROJ_CHUNK] = (y * jax.nn.sigmoid(y)).astype(BF16)

    plain = [(val, 0), (gate, 0), (val, 1), (gate, 1), (val, 2), (gate, 2), (val, 3), (gate, 3)]
    for r, (fn, idx) in enumerate(plain):
        fn(h if r == 0 else _after(h, cos_p[r - 1], sin_p[r - 1]), idx)
    for hd in range(2 * RET_HEADS):
        rot(h, hd)


def _proj_ret(h, w, pos, inv):
    t = h.shape[0]
    row = lambda width: pl.BlockSpec((TM_RET, width), lambda i: (i, 0))
    return pl.pallas_call(
        _proj_ret_kernel,
        grid=(t // TM_RET,),
        in_specs=[row(D_MODEL), _ANY, row(1), pl.BlockSpec((1, RET_DK // 2), lambda i: (0, 0))],
        out_specs=[row(2 * RET_QK_W), row(RET_V_W), row(RET_V_W)],
        out_shape=[jax.ShapeDtypeStruct((t, 2 * RET_QK_W), BF16),
                   jax.ShapeDtypeStruct((t, RET_V_W), BF16),
                   jax.ShapeDtypeStruct((t, RET_V_W), BF16)],
        scratch_shapes=_weight_scratch(D_MODEL, PROJ_RET_W),
        compiler_params=_params(("arbitrary",)),
        name="proj_ret",
    )(h, w, pos, inv)


def _proj_mix_kernel(h_ref, w_hbm, gqk_ref, gmq_ref, mem_ref, gmem_ref, gmk_ref, wkv_hbm,
                     qk_ref, v_ref, mo_ref, gt_ref,
                     w_ref, w_stage, w_sem, wkv_ref, wkv_stage, wkv_sem, mk_ref, mv_ref,
                     *, tiles_per_row):
    @pl.when(pl.program_id(0) == 0)
    def _():
        _load_weight(w_hbm, PROJ_RET_W, w_ref, w_stage, w_sem)
        _load_weight(wkv_hbm, 0, wkv_ref, wkv_stage, wkv_sem)

    @pl.when(pl.program_id(0) % tiles_per_row == 0)
    def _():
        mn = (_rms(mem_ref[...]) * gmem_ref[...]).astype(BF16)
        kv = _dot(mn, wkv_ref[...])
        mk_ref[...] = _group_norm_128(kv[:, :MEM_W], gmk_ref[...], MEM_D).astype(BF16)
        mv_ref[...] = kv[:, MEM_W:].astype(BF16)

    h = h_ref[...]
    off_qk = 0
    off_v = off_qk + 2 * DIFF_W
    off_mq = off_v + DIFF_W
    off_gt = off_mq + MEM_W

    mem_q, mem_p = {}, {}

    def qk_norm(lhs, c):
        lo = c * QK_CHUNK
        y = _dot(lhs, w_ref[:, off_qk + lo:off_qk + lo + QK_CHUNK])
        out = _group_norm_128(y, gqk_ref[:, lo:lo + QK_CHUNK], DIFF_D).astype(BF16)
        qk_ref[:, lo:lo + QK_CHUNK] = out
        return out

    def val(lhs, c):
        lo = c * PROJ_CHUNK
        out = _dot(lhs, w_ref[:, off_v + lo:off_v + lo + PROJ_CHUNK]).astype(BF16)
        v_ref[:, lo:lo + PROJ_CHUNK] = out
        return out

    def gate(lhs, c):
        lo = c * PROJ_CHUNK
        y = _dot(lhs, w_ref[:, off_gt + lo:off_gt + lo + PROJ_CHUNK])
        out = jax.nn.sigmoid(y).astype(BF16)
        gt_ref[:, lo:lo + PROJ_CHUNK] = out
        return out

    def mem_query(lhs, hd):
        sl = slice(hd * MEM_D, (hd + 1) * MEM_D)
        y = _dot(lhs, w_ref[:, off_mq + hd * MEM_D:off_mq + (hd + 1) * MEM_D])
        mem_q[hd] = (_rms(y) * gmq_ref[:, sl]).astype(BF16)
        return mem_q[hd]

    def mem_scores(lhs, hd):
        sl = slice(hd * MEM_D, (hd + 1) * MEM_D)
        s = _dot_nt(mem_q[hd], mk_ref[:, sl])
        p = jnp.exp2(s - jnp.max(s, axis=-1, keepdims=True))
        mem_p[hd] = (p.astype(BF16), jnp.sum(p, axis=-1, keepdims=True))
        return mem_p[hd][0]

    def mem_values(lhs, hd):
        sl = slice(hd * MEM_D, (hd + 1) * MEM_D)
        p, l = mem_p[hd]
        out = (_dot(p, mv_ref[:, sl]) / l).astype(BF16)
        mo_ref[:, sl] = out
        return out

    order = [(gate, 0), (mem_query, 0), (qk_norm, 0), (mem_scores, 0), (qk_norm, 1),
             (mem_values, 0), (gate, 1), (mem_query, 1), (qk_norm, 2), (mem_scores, 1),
             (qk_norm, 3), (mem_values, 1), (gate, 2), (mem_query, 2), (qk_norm, 4),
             (mem_scores, 2), (qk_norm, 5), (mem_values, 2), (gate, 3), (mem_query, 3),
             (qk_norm, 6), (mem_scores, 3), (qk_norm, 7), (mem_values, 3), (gate, 4),
             (gate, 5), (val, 0), (val, 1)]
    for fn, idx in order:
        fn(h, idx)


def _proj_mix(h, w, gqk, gmq, mem2d, g_mem, gmk, w_kv, seq):
    t = h.shape[0]
    mlen = mem2d.shape[0] // (t // seq)
    tiles_per_batch = seq // TM
    row = lambda width: pl.BlockSpec((TM, width), lambda i: (i, 0))
    vec = lambda width: pl.BlockSpec((1, width), lambda i: (0, 0))
    mem = pl.BlockSpec((mlen, D_MODEL), lambda i: (i // tiles_per_batch, 0))
    return pl.pallas_call(
        functools.partial(_proj_mix_kernel, tiles_per_row=tiles_per_batch),
        grid=(t // TM,),
        in_specs=[row(D_MODEL), _ANY, vec(2 * DIFF_W), vec(MEM_W), mem, vec(D_MODEL), vec(MEM_W),
                  _ANY],
        out_specs=[row(2 * DIFF_W), row(DIFF_W), row(MEM_W), row(3 * D_MODEL)],
        out_shape=[jax.ShapeDtypeStruct((t, 2 * DIFF_W), BF16),
                   jax.ShapeDtypeStruct((t, DIFF_W), BF16),
                   jax.ShapeDtypeStruct((t, MEM_W), BF16),
                   jax.ShapeDtypeStruct((t, 3 * D_MODEL), BF16)],
        scratch_shapes=_weight_scratch(D_MODEL, PROJ_MIX_W) + _weight_scratch(*w_kv.shape)
        + [pltpu.VMEM((mlen, MEM_W), BF16), pltpu.VMEM((mlen, MEM_W), BF16)],
        compiler_params=_params(("arbitrary",)),
        name="proj_mix",
    )(h, w, gqk, gmq, mem2d, g_mem, gmk, w_kv)


def _diff_kernel(q_ref, k_ref, v_ref, lq1_ref, lk1_ref, lq2_ref, lk2_ref, go_ref, o_ref,
                 *, lam_init, seq):
    tq = DIFF_TQ
    lane = lax.broadcasted_iota(jnp.int32, (1, 2 * DIFF_D), 1)
    first_map = lane < DIFF_D
    r = lax.broadcasted_iota(jnp.int32, (tq, tq), 0)
    cidx = lax.broadcasted_iota(jnp.int32, (tq, tq), 1)
    keep = cidx <= r
    keep = jnp.concatenate([keep, keep], axis=0)
    lam = (jnp.exp(jnp.sum(lq1_ref[...] * lk1_ref[...], axis=-1, keepdims=True))
           - jnp.exp(jnp.sum(lq2_ref[...] * lk2_ref[...], axis=-1, keepdims=True))
           + lam_init)
    go = go_ref[...]
    nq = seq // tq
    hw = 2 * DIFF_D
    heads = DIFF_HEADS_PER_STEP
    ones = jnp.ones((seq, hw), BF16)
    v_ext = [jnp.concatenate([v_ref[:, hd * hw:(hd + 1) * hw], ones], axis=1)
             for hd in range(heads)]

    def finish(hd, i, o_ext):
        r_ = o_ext[:, :hw] / o_ext[:, hw:]
        d = r_[:tq] - lam * r_[tq:]
        o_ref[i * tq:(i + 1) * tq, hd * hw:(hd + 1) * hw] = (_rms(d) * go).astype(BF16)

    prev = [None] * heads
    for i in list(range(nq)) + [None]:
        tiles = [[] for _ in range(heads)]
        run_max = [None] * heads
        acc = [None] * heads
        qs = [None] * heads
        if i is not None:
            for hd in range(heads):
                q = q_ref[i * tq:(i + 1) * tq, hd * hw:(hd + 1) * hw]
                zero = jnp.zeros_like(q)
                qs[hd] = jnp.concatenate([jnp.where(first_map, q, zero),
                                          jnp.where(first_map, zero, q)], axis=0)
        n_cur = 0 if i is None else i + 1
        n_prev = 0 if prev[0] is None else len(prev[0][1])
        for j in range(max(n_cur, n_prev)):
            for hd in range(heads):
                if j < n_cur:
                    s = _dot_nt(qs[hd], k_ref[j * tq:(j + 1) * tq, hd * hw:(hd + 1) * hw])
                    if j == i:
                        s = jnp.where(keep, s, NEG)
                    tiles[hd].append(s)
                    run_max[hd] = s if run_max[hd] is None else jnp.maximum(run_max[hd], s)
                if j < n_prev:
                    p = jnp.exp2(prev[hd][1][j] - prev[hd][2]).astype(BF16)
                    pv = _dot(p, v_ext[hd][j * tq:(j + 1) * tq, :])
                    acc[hd] = pv if acc[hd] is None else acc[hd] + pv
        for hd in range(heads):
            if prev[hd] is not None:
                finish(hd, prev[hd][0], acc[hd])
            prev[hd] = (None if i is None else
                        (i, tiles[hd], jnp.max(run_max[hd], axis=-1, keepdims=True)))


def _diff_attention(qk, v, lq1, lk1, lq2, lk2, g_out, batch, seq, lam_init):
    t = batch * seq
    width = DIFF_HEADS_PER_STEP * 2 * DIFF_D
    groups = DIFF_HEADS // DIFF_HEADS_PER_STEP
    vec64 = pl.BlockSpec((1, DIFF_D), lambda b, h: (0, 0))
    head = pl.BlockSpec((seq, width), lambda b, h: (b, h))
    return pl.pallas_call(
        functools.partial(_diff_kernel, lam_init=lam_init, seq=seq),
        grid=(batch, groups),
        in_specs=[head,
                  pl.BlockSpec((seq, width), lambda b, h: (b, groups + h)),
                  head,
                  vec64, vec64, vec64, vec64,
                  pl.BlockSpec((1, 2 * DIFF_D), lambda b, h: (0, 0))],
        out_specs=head,
        out_shape=jax.ShapeDtypeStruct((t, DIFF_HEADS * 2 * DIFF_D), BF16),
        compiler_params=_params(("parallel", "parallel")),
        name="diff_attention",
    )(qk, qk, v, lq1, lk1, lq2, lk2, g_out)


def _ret_merge_kernel(x_ref, qk_ref, v_ref, g_ref, dif_ref, mo_ref, gt_ref, dec_ref, qd_ref, kd_ref,
                      wr_hbm, wd_hbm, wm_hbm, wo_hbm, o_ref,
                      st_ref, ret_ref, wr_ref, wd_ref, wm_ref, wo_ref, stage, sem, *, tiles_per_row):
    i = pl.program_id(0)

    @pl.when(i == 0)
    def _():
        for src, dst in ((wr_hbm, wr_ref), (wd_hbm, wd_ref), (wm_hbm, wm_ref), (wo_hbm, wo_ref)):
            _load_weight(src, 0, dst, stage, sem)

    @pl.when(i % tiles_per_row == 0)
    def _():
        st_ref[...] = jnp.zeros_like(st_ref)

    tm = x_ref.shape[0]
    for c in range(tm // RET_C):
        lo = c * RET_C
        for hd in range(RET_HEADS):
            qd = qd_ref[hd]
            q = qk_ref[lo:lo + RET_C, hd * RET_DK:(hd + 1) * RET_DK]
            k = qk_ref[lo:lo + RET_C, RET_QK_W + hd * RET_DK:RET_QK_W + (hd + 1) * RET_DK]
            v = v_ref[lo:lo + RET_C, hd * RET_DV:(hd + 1) * RET_DV]
            st = st_ref[hd]
            s = _dot_nt(q, k) * dec_ref[hd]
            out = _dot(s.astype(BF16), v) + _dot(q, st.astype(BF16)) * qd
            upd = _dot_tn((k.astype(F32) * kd_ref[hd]).astype(BF16), v)
            st_ref[hd] = qd[RET_C - 1:RET_C, :] * st + upd
            gate = g_ref[lo:lo + RET_C, hd * RET_DV:(hd + 1) * RET_DV].astype(F32)
            ret_ref[lo:lo + RET_C, hd * RET_DV:(hd + 1) * RET_DV] = (_rms(out) * gate).astype(BF16)

    gt = gt_ref[...].astype(F32)
    y = (gt[:, :D_MODEL] * _dot(ret_ref[...], wr_ref[...])
         + gt[:, D_MODEL:2 * D_MODEL] * _dot(dif_ref[...], wd_ref[...])
         + gt[:, 2 * D_MODEL:] * _dot(mo_ref[...], wm_ref[...]))
    o_ref[...] = x_ref[...] + _dot(y.astype(BF16), wo_ref[...])


def _ret_merge(x, qk, v, gate, dif, mo, gates, dec, qd, kd, wr, wd, wm, wo, seq):
    t = x.shape[0]
    row = lambda w: pl.BlockSpec((TM, w), lambda i: (i, 0))
    const = lambda shape: pl.BlockSpec(shape, lambda i: (0,) * len(shape))
    return pl.pallas_call(
        functools.partial(_ret_merge_kernel, tiles_per_row=seq // TM),
        grid=(t // TM,),
        in_specs=[row(D_MODEL), row(2 * RET_QK_W), row(RET_V_W), row(RET_V_W), row(D_MODEL),
                  row(D_MODEL), row(3 * D_MODEL),
                  const(dec.shape), const(qd.shape), const(kd.shape),
                  _ANY, _ANY, _ANY, _ANY],
        out_specs=row(D_MODEL),
        out_shape=jax.ShapeDtypeStruct((t, D_MODEL), F32),
        scratch_shapes=[pltpu.VMEM((RET_HEADS, RET_DK, RET_DV), F32),
                        pltpu.VMEM((TM, RET_V_W), BF16),
                        pltpu.VMEM(wr.shape, BF16), pltpu.VMEM(wd.shape, BF16),
                        pltpu.VMEM(wm.shape, BF16), pltpu.VMEM(wo.shape, BF16)]
        + _weight_scratch(D_MODEL, D_MODEL)[1:],
        compiler_params=_params(("arbitrary",)),
        name="ret_merge",
    )(x, qk, v, gate, dif, mo, gates, dec, qd, kd, wr, wd, wm, wo)


def _retention_constants():
    h = np.arange(RET_HEADS, dtype=np.float64)
    log_g = np.log1p(-(2.0 ** (-5.0 - h)))
    idx = np.arange(RET_C, dtype=np.float64)
    dist = idx[:, None] - idx[None, :]
    dec = np.where(dist >= 0, np.exp(log_g[:, None, None] * np.maximum(dist, 0.0)), 0.0)
    qd = np.exp(log_g[:, None] * (idx + 1.0))[:, :, None]
    kd = np.exp(log_g[:, None] * (RET_C - 1.0 - idx))[:, :, None]
    return (jnp.asarray(dec, F32), jnp.asarray(qd, F32), jnp.asarray(kd, F32))


def kernel(x, mem, positions, g_ffn1, w_ffn1_in, w_ffn1_out, g_mix, w_in, g_diff_q, g_diff_k,
           lam_q1, lam_k1, lam_q2, lam_k2, g_diff_out, g_mem_q, g_mem_k, g_mem, w_mem_kv,
           w_br_ret, w_br_diff, w_br_mem, w_o, g_ffn2, w_ffn2_in, w_ffn2_out):
    batch, seq, _ = x.shape
    mlen = mem.shape[1]
    t = batch * seq
    depth = g_ffn1.shape[0]
    half = RET_DK // 2
    inv = jnp.asarray(ROPE_BASE ** (-np.arange(half, dtype=np.float64) / half), F32)[None, :]
    dec, qd, kd = _retention_constants()
    pos = positions.reshape(t, 1)
    mem2d = mem.reshape(batch * mlen, D_MODEL)
    xf = x.reshape(t, D_MODEL)
    f32 = lambda w: w.astype(F32)
    vec = lambda g: g.astype(F32)[None, :]

    for l in range(depth):
        lam_init = 0.8 - 0.6 * math.exp(-0.3 * l)
        x1, h = _ffn(xf, vec(g_ffn1[l]), f32(w_ffn1_in[l]), f32(w_ffn1_out[l]), vec(g_mix[l]))

        w = f32(w_in[l])
        rqk, rv, rg = _proj_ret(h, w, pos, inv)
        qk_gain = jnp.concatenate([jnp.tile(g_diff_q[l].astype(F32), 2 * DIFF_HEADS) * (DIFF_D ** -0.5 * LOG2E),
                                   jnp.tile(g_diff_k[l].astype(F32), 2 * DIFF_HEADS)])[None, :]
        mq_gain = (jnp.tile(g_mem_q[l].astype(F32), MEM_HEADS) * (MEM_D ** -0.5 * LOG2E))[None, :]
        mk_gain = jnp.tile(g_mem_k[l].astype(F32), MEM_HEADS)[None, :]
        dqk, dv, mo, gates = _proj_mix(h, w, qk_gain, mq_gain, mem2d, vec(g_mem[l]), mk_gain,
                                       f32(w_mem_kv[l]), seq)

        go = (g_diff_out[l].astype(F32) * (1.0 - lam_init))[None, :]
        dif = _diff_attention(dqk, dv, vec(lam_q1[l]), vec(lam_k1[l]), vec(lam_q2[l]),
                              vec(lam_k2[l]), go, batch, seq, lam_init)

        x2 = _ret_merge(x1, rqk, rv, rg, dif, mo, gates, dec, qd, kd, f32(w_br_ret[l]),
                        f32(w_br_diff[l]), f32(w_br_mem[l]), f32(w_o[l]), seq)
        (xf,) = _ffn(x2, vec(g_ffn2[l]), f32(w_ffn2_in[l]), f32(w_ffn2_out[l]))
    return xf.reshape(batch, seq, D_MODEL)
```

```python
import functools
import math

import jax
import jax.numpy as jnp
import numpy as np
from jax import lax
from jax.experimental import pallas as pl
from jax.experimental.pallas import tpu as pltpu

F32 = jnp.float32
BF16 = jnp.bfloat16

D_MODEL = 1024
D_FF = 2816
FFN_RES = 0.5
EPS = 1e-6
NEG = -1e30
LOG2E = math.log2(math.e)
ROPE_BASE = 10000.0

RET_HEADS = 4
RET_DK = 256
RET_DV = 512
DIFF_HEADS = 8
DIFF_D = 64
MEM_HEADS = 4
MEM_D = 256

V7X_VMEM_LIMIT_BYTES = 56 * 1024 * 1024
LANES = 128
BF16_ROW_TILE = 16

TM = 512
TM_FFN = 512
TM_RET = 1024
FFN_CHUNK = 256
RET_C = 256
DIFF_TQ = 256
DIFF_HEADS_PER_STEP = 2


def _params(sem):
    return pltpu.CompilerParams(dimension_semantics=sem,
                                vmem_limit_bytes=V7X_VMEM_LIMIT_BYTES)


def _rms(x):
    return x * lax.rsqrt(jnp.mean(x * x, axis=-1, keepdims=True) + EPS)


def _after(x, *deps):
    folded = None
    for d in deps:
        t = jnp.sum(d.astype(F32), axis=0, keepdims=True)
        for c in range(t.shape[1] // LANES):
            blk = t[:, c * LANES:(c + 1) * LANES]
            folded = blk if folded is None else folded + blk
    bits = lax.bitcast_convert_type(folded, jnp.uint32)
    zero = lax.shift_right_logical(lax.shift_right_logical(bits, jnp.uint32(16)), jnp.uint32(16))
    zero = zero.astype(F32).astype(x.dtype)
    zero = jnp.concatenate([zero] * (x.shape[1] // LANES), axis=1)
    return jnp.concatenate([x[:BF16_ROW_TILE] + zero, x[BF16_ROW_TILE:]], axis=0)


def _dot(a, b):
    return jnp.dot(a, b, preferred_element_type=F32)


def _dot_nt(a, b):
    return lax.dot_general(a, b, (((1,), (1,)), ((), ())), preferred_element_type=F32)


def _dot_tn(a, b):
    return lax.dot_general(a, b, (((0,), (0,)), ((), ())), preferred_element_type=F32)


STAGE_BYTES = 1024 * 1024
STAGE_SLOTS = 4


def _stage_rows(width):
    return max(16, (STAGE_BYTES // (4 * width)) // 16 * 16)


def _weight_scratch(rows, width, slots=STAGE_SLOTS):
    return [pltpu.VMEM((rows, width), BF16),
            pltpu.VMEM((slots, _stage_rows(width), width), F32),
            pltpu.SemaphoreType.DMA((slots,))]


def _load_weight(w_hbm, col0, w_vmem, stage, sem):
    rows, width = w_vmem.shape
    slots, step = stage.shape[0], stage.shape[1]
    starts = list(range(0, rows, step))

    def copy(c):
        n = min(step, rows - starts[c])
        return pltpu.make_async_copy(
            w_hbm.at[pl.ds(starts[c], n), pl.ds(col0, width)],
            stage.at[c % slots, pl.ds(0, n), :], sem.at[c % slots])

    for c in range(min(slots - 1, len(starts))):
        copy(c).start()
    for c in range(len(starts)):
        if c + slots - 1 < len(starts):
            copy(c + slots - 1).start()
        copy(c).wait()
        n = min(step, rows - starts[c])
        w_vmem[starts[c]:starts[c] + n, :] = stage[c % slots, 0:n, :].astype(BF16)


_ANY = pl.BlockSpec(memory_space=pl.ANY)


FFN_RING = 3


def _ffn_kernel(*refs, emit_next):
    if emit_next:
        x_ref, g_ref, win_hbm, wout_hbm, gn_ref, o_ref, h_ref = refs[:7]
    else:
        x_ref, g_ref, win_hbm, wout_hbm, o_ref = refs[:5]
    win_ref, wout_ref, sg_ref, su_ref, so_ref, sem = refs[-6:]
    n_chunks = D_FF // FFN_CHUNK

    def copies(j):
        lo, slot = j * FFN_CHUNK, j % FFN_RING
        return (pltpu.make_async_copy(win_hbm.at[:, pl.ds(lo, FFN_CHUNK)],
                                      sg_ref.at[slot], sem.at[slot, 0]),
                pltpu.make_async_copy(win_hbm.at[:, pl.ds(D_FF + lo, FFN_CHUNK)],
                                      su_ref.at[slot], sem.at[slot, 1]),
                pltpu.make_async_copy(wout_hbm.at[pl.ds(lo, FFN_CHUNK), :],
                                      so_ref.at[slot], sem.at[slot, 2]))

    def fetch_chunk(j):
        if j == 0:
            for c in range(min(FFN_RING - 1, n_chunks)):
                for cp in copies(c):
                    cp.start()
        if j + FFN_RING - 1 < n_chunks:
            for cp in copies(j + FFN_RING - 1):
                cp.start()
        for cp in copies(j):
            cp.wait()
        lo, slot = j * FFN_CHUNK, j % FFN_RING
        win_ref[:, lo:lo + FFN_CHUNK] = sg_ref[slot].astype(BF16)
        win_ref[:, D_FF + lo:D_FF + lo + FFN_CHUNK] = su_ref[slot].astype(BF16)
        wout_ref[lo:lo + FFN_CHUNK, :] = so_ref[slot].astype(BF16)

    def body(before_chunk):
        x = x_ref[...]
        hb = (_rms(x) * g_ref[...]).astype(BF16)
        acc = jnp.zeros(x.shape, F32)
        for j in range(n_chunks):
            before_chunk(j)
            lo = j * FFN_CHUNK
            a = _dot(hb, win_ref[:, lo:lo + FFN_CHUNK])
            b = _dot(hb, win_ref[:, D_FF + lo:D_FF + lo + FFN_CHUNK])
            act = (a * jax.nn.sigmoid(a) * b).astype(BF16)
            acc = acc + _dot(act, wout_ref[lo:lo + FFN_CHUNK, :])
        y = x + FFN_RES * acc
        o_ref[...] = y
        if emit_next:
            h_ref[...] = (_rms(y) * gn_ref[...]).astype(BF16)

    @pl.when(pl.program_id(0) == 0)
    def _():
        body(fetch_chunk)

    @pl.when(pl.program_id(0) != 0)
    def _():
        body(lambda j: None)


def _ffn(x, g, w_in, w_out, g_next=None):
    t = x.shape[0]
    emit = g_next is not None
    row = pl.BlockSpec((TM_FFN, D_MODEL), lambda i: (i, 0))
    vec = pl.BlockSpec((1, D_MODEL), lambda i: (0, 0))
    in_specs = [row, vec, _ANY, _ANY]
    args = [x, g, w_in, w_out]
    out_shape = [jax.ShapeDtypeStruct((t, D_MODEL), F32)]
    out_specs = [row]
    if emit:
        in_specs.append(vec)
        args.append(g_next)
        out_shape.append(jax.ShapeDtypeStruct((t, D_MODEL), BF16))
        out_specs.append(row)
    return pl.pallas_call(
        functools.partial(_ffn_kernel, emit_next=emit),
        grid=(t // TM_FFN,),
        in_specs=in_specs,
        out_specs=out_specs,
        out_shape=out_shape,
        scratch_shapes=[pltpu.VMEM(w_in.shape, BF16), pltpu.VMEM(w_out.shape, BF16),
                        pltpu.VMEM((FFN_RING, D_MODEL, FFN_CHUNK), F32),
                        pltpu.VMEM((FFN_RING, D_MODEL, FFN_CHUNK), F32),
                        pltpu.VMEM((FFN_RING, FFN_CHUNK, D_MODEL), F32),
                        pltpu.SemaphoreType.DMA((FFN_RING, 3))],
        compiler_params=_params(("arbitrary",)),
        name="ffn_next" if emit else "ffn",
    )(*args)


def _group_norm_128(y, gain, group):
    cols = y.shape[1]
    outs = []
    if 2 * group == LANES:
        lane = lax.broadcasted_iota(jnp.int32, (1, LANES), 1)
        first = lane < group
        for c in range(cols // LANES):
            blk = y[:, c * LANES:(c + 1) * LANES]
            sq = blk * blk
            s_lo = jnp.sum(jnp.where(first, sq, 0.0), axis=-1, keepdims=True)
            s_hi = jnp.sum(jnp.where(first, 0.0, sq), axis=-1, keepdims=True)
            r = jnp.where(first, lax.rsqrt(s_lo * (1.0 / group) + EPS),
                          lax.rsqrt(s_hi * (1.0 / group) + EPS))
            outs.append(blk * r)
    else:
        for c in range(cols // group):
            blk = y[:, c * group:(c + 1) * group]
            outs.append(_rms(blk))
    return jnp.concatenate(outs, axis=1) * gain


RET_QK_W = RET_HEADS * RET_DK
RET_V_W = RET_HEADS * RET_DV
PROJ_RET_W = 2 * RET_QK_W + 2 * RET_V_W
DIFF_W = DIFF_HEADS * 2 * DIFF_D
MEM_W = MEM_HEADS * MEM_D
PROJ_MIX_W = 2 * DIFF_W + DIFF_W + MEM_W + 3 * D_MODEL
PROJ_CHUNK = 512
QK_CHUNK = 256

def _proj_ret_kernel(h_ref, w_hbm, pos_ref, inv_ref, qk_ref, v_ref, g_ref, w_ref, w_stage, w_sem):
    @pl.when(pl.program_id(0) == 0)
    def _():
        _load_weight(w_hbm, 0, w_ref, w_stage, w_sem)

    h = h_ref[...]
    tm = h.shape[0]
    n_piece = 8
    rows = tm // n_piece
    cos_p, sin_p = [], []
    for r in range(n_piece):
        ang = pos_ref[r * rows:(r + 1) * rows, :].astype(F32) * inv_ref[...]
        cos_p.append(jnp.cos(ang))
        sin_p.append(jnp.sin(ang))
    cos = jnp.concatenate(cos_p, axis=0)
    sin = jnp.concatenate(sin_p, axis=0)
    half = RET_DK // 2

    def rot(lhs, hd):
        lo = hd * RET_DK
        y = _dot(lhs, w_ref[:, lo:lo + RET_DK])
        x1 = y[:, :half]
        x2 = y[:, half:]
        r1 = x1 * cos - x2 * sin
        r2 = x1 * sin + x2 * cos
        if hd >= RET_HEADS:
            r1 = r1 * (RET_DK ** -0.5)
            r2 = r2 * (RET_DK ** -0.5)
        r1 = r1.astype(BF16)
        r2 = r2.astype(BF16)
        qk_ref[:, lo:lo + half] = r1
        qk_ref[:, lo + half:lo + RET_DK] = r2
        return r1, r2

    def val(lhs, c):
        lo = c * PROJ_CHUNK
        v_ref[:, lo:lo + PROJ_CHUNK] = _dot(
            lhs, w_ref[:, 2 * RET_QK_W + lo:2 * RET_QK_W + lo + PROJ_CHUNK]).astype(BF16)

    def gate(lhs, c):
        lo = c * PROJ_CHUNK
        wlo = 2 * RET_QK_W + RET_V_W + lo
        y = _dot(lhs, w_ref[:, wlo:wlo + PROJ_CHUNK])
        g_ref[:, lo:lo + PROJ_CHUNK] = (y * jax.nn.sigmoid(y)).astype(BF16)

    plain = [(val, 0), (gate, 0), (val, 1), (gate, 1), (val, 2), (gate, 2), (val, 3), (gate, 3)]
    for r, (fn, idx) in enumerate(plain):
        fn(h if r == 0 else _after(h, cos_p[r - 1], sin_p[r - 1]), idx)
    for hd in range(2 * RET_HEADS):
        rot(h, hd)


def _proj_ret(h, w, pos, inv):
    t = h.shape[0]
    row = lambda width: pl.BlockSpec((TM_RET, width), lambda i: (i, 0))
    return pl.pallas_call(
        _proj_ret_kernel,
        grid=(t // TM_RET,),
        in_specs=[row(D_MODEL), _ANY, row(1), pl.BlockSpec((1, RET_DK // 2), lambda i: (0, 0))],
        out_specs=[row(2 * RET_QK_W), row(RET_V_W), row(RET_V_W)],
        out_shape=[jax.ShapeDtypeStruct((t, 2 * RET_QK_W), BF16),
                   jax.ShapeDtypeStruct((t, RET_V_W), BF16),
                   jax.ShapeDtypeStruct((t, RET_V_W), BF16)],
        scratch_shapes=_weight_scratch(D_MODEL, PROJ_RET_W, slots=2 * STAGE_SLOTS),
        compiler_params=_params(("arbitrary",)),
        name="proj_ret",
    )(h, w, pos, inv)


def _proj_mix_kernel(h_ref, w_hbm, gqk_ref, gmq_ref, mem_ref, gmem_ref, gmk_ref, wkv_hbm,
                     qk_ref, v_ref, mo_ref, gt_ref,
                     w_ref, w_stage, w_sem, wkv_ref, wkv_stage, wkv_sem, mk_ref, mv_ref,
                     *, tiles_per_row):
    @pl.when(pl.program_id(0) == 0)
    def _():
        _load_weight(w_hbm, PROJ_RET_W, w_ref, w_stage, w_sem)
        _load_weight(wkv_hbm, 0, wkv_ref, wkv_stage, wkv_sem)

    @pl.when(pl.program_id(0) % tiles_per_row == 0)
    def _():
        mn = (_rms(mem_ref[...]) * gmem_ref[...]).astype(BF16)
        kv = _dot(mn, wkv_ref[...])
        mk_ref[...] = _group_norm_128(kv[:, :MEM_W], gmk_ref[...], MEM_D).astype(BF16)
        mv_ref[...] = kv[:, MEM_W:].astype(BF16)

    h = h_ref[...]
    off_qk = 0
    off_v = off_qk + 2 * DIFF_W
    off_mq = off_v + DIFF_W
    off_gt = off_mq + MEM_W

    mem_q, mem_p = {}, {}

    def qk_norm(lhs, c):
        lo = c * QK_CHUNK
        y = _dot(lhs, w_ref[:, off_qk + lo:off_qk + lo + QK_CHUNK])
        out = _group_norm_128(y, gqk_ref[:, lo:lo + QK_CHUNK], DIFF_D).astype(BF16)
        qk_ref[:, lo:lo + QK_CHUNK] = out
        return out

    def val(lhs, c):
        lo = c * PROJ_CHUNK
        out = _dot(lhs, w_ref[:, off_v + lo:off_v + lo + PROJ_CHUNK]).astype(BF16)
        v_ref[:, lo:lo + PROJ_CHUNK] = out
        return out

    def gate(lhs, c):
        lo = c * PROJ_CHUNK
        y = _dot(lhs, w_ref[:, off_gt + lo:off_gt + lo + PROJ_CHUNK])
        out = jax.nn.sigmoid(y).astype(BF16)
        gt_ref[:, lo:lo + PROJ_CHUNK] = out
        return out

    def mem_query(lhs, hd):
        sl = slice(hd * MEM_D, (hd + 1) * MEM_D)
        y = _dot(lhs, w_ref[:, off_mq + hd * MEM_D:off_mq + (hd + 1) * MEM_D])
        mem_q[hd] = (_rms(y) * gmq_ref[:, sl]).astype(BF16)
        return mem_q[hd]

    def mem_scores(lhs, hd):
        sl = slice(hd * MEM_D, (hd + 1) * MEM_D)
        s = _dot_nt(mem_q[hd], mk_ref[:, sl])
        p = jnp.exp2(s - jnp.max(s, axis=-1, keepdims=True))
        mem_p[hd] = (p.astype(BF16), jnp.sum(p, axis=-1, keepdims=True))
        return mem_p[hd][0]

    def mem_values(lhs, hd):
        sl = slice(hd * MEM_D, (hd + 1) * MEM_D)
        p, l = mem_p[hd]
        out = (_dot(p, mv_ref[:, sl]) / l).astype(BF16)
        mo_ref[:, sl] = out
        return out

    order = [(gate, 0), (mem_query, 0), (qk_norm, 0), (mem_scores, 0), (qk_norm, 1),
             (mem_values, 0), (gate, 1), (mem_query, 1), (qk_norm, 2), (mem_scores, 1),
             (qk_norm, 3), (mem_values, 1), (gate, 2), (mem_query, 2), (qk_norm, 4),
             (mem_scores, 2), (qk_norm, 5), (mem_values, 2), (gate, 3), (mem_query, 3),
             (qk_norm, 6), (mem_scores, 3), (qk_norm, 7), (mem_values, 3), (gate, 4),
             (gate, 5), (val, 0), (val, 1)]
    for fn, idx in order:
        fn(h, idx)


def _proj_mix(h, w, gqk, gmq, mem2d, g_mem, gmk, w_kv, seq):
    t = h.shape[0]
    mlen = mem2d.shape[0] // (t // seq)
    tiles_per_batch = seq // TM
    row = lambda width: pl.BlockSpec((TM, width), lambda i: (i, 0))
    vec = lambda width: pl.BlockSpec((1, width), lambda i: (0, 0))
    mem = pl.BlockSpec((mlen, D_MODEL), lambda i: (i // tiles_per_batch, 0))
    return pl.pallas_call(
        functools.partial(_proj_mix_kernel, tiles_per_row=tiles_per_batch),
        grid=(t // TM,),
        in_specs=[row(D_MODEL), _ANY, vec(2 * DIFF_W), vec(MEM_W), mem, vec(D_MODEL), vec(MEM_W),
                  _ANY],
        out_specs=[row(2 * DIFF_W), row(DIFF_W), row(MEM_W), row(3 * D_MODEL)],
        out_shape=[jax.ShapeDtypeStruct((t, 2 * DIFF_W), BF16),
                   jax.ShapeDtypeStruct((t, DIFF_W), BF16),
                   jax.ShapeDtypeStruct((t, MEM_W), BF16),
                   jax.ShapeDtypeStruct((t, 3 * D_MODEL), BF16)],
        scratch_shapes=_weight_scratch(D_MODEL, PROJ_MIX_W, slots=STAGE_SLOTS + 2)
        + _weight_scratch(*w_kv.shape, slots=STAGE_SLOTS - 1)
        + [pltpu.VMEM((mlen, MEM_W), BF16), pltpu.VMEM((mlen, MEM_W), BF16)],
        compiler_params=_params(("arbitrary",)),
        name="proj_mix",
    )(h, w, gqk, gmq, mem2d, g_mem, gmk, w_kv)


def _diff_kernel(q_ref, k_ref, v_ref, lq1_ref, lk1_ref, lq2_ref, lk2_ref, go_ref, o_ref,
                 *, lam_init, seq):
    tq = DIFF_TQ
    lane = lax.broadcasted_iota(jnp.int32, (1, 2 * DIFF_D), 1)
    first_map = lane < DIFF_D
    r = lax.broadcasted_iota(jnp.int32, (tq, tq), 0)
    cidx = lax.broadcasted_iota(jnp.int32, (tq, tq), 1)
    keep = cidx <= r
    keep = jnp.concatenate([keep, keep], axis=0)
    lam = (jnp.exp(jnp.sum(lq1_ref[...] * lk1_ref[...], axis=-1, keepdims=True))
           - jnp.exp(jnp.sum(lq2_ref[...] * lk2_ref[...], axis=-1, keepdims=True))
           + lam_init)
    go = go_ref[...]
    nq = seq // tq
    hw = 2 * DIFF_D
    heads = DIFF_HEADS_PER_STEP
    ones = jnp.ones((seq, hw), BF16)
    v_ext = [jnp.concatenate([v_ref[:, hd * hw:(hd + 1) * hw], ones], axis=1)
             for hd in range(heads)]

    def finish(hd, i, o_ext):
        r_ = o_ext[:, :hw] / o_ext[:, hw:]
        d = r_[:tq] - lam * r_[tq:]
        o_ref[i * tq:(i + 1) * tq, hd * hw:(hd + 1) * hw] = (_rms(d) * go).astype(BF16)

    prev = [None] * heads
    for i in list(range(nq)) + [None]:
        tiles = [[] for _ in range(heads)]
        run_max = [None] * heads
        acc = [None] * heads
        qs = [None] * heads
        if i is not None:
            for hd in range(heads):
                q = q_ref[i * tq:(i + 1) * tq, hd * hw:(hd + 1) * hw]
                zero = jnp.zeros_like(q)
                qs[hd] = jnp.concatenate([jnp.where(first_map, q, zero),
                                          jnp.where(first_map, zero, q)], axis=0)
        n_cur = 0 if i is None else i + 1
        n_prev = 0 if prev[0] is None else len(prev[0][1])
        for j in range(max(n_cur, n_prev)):
            for hd in range(heads):
                if j < n_cur:
                    s = _dot_nt(qs[hd], k_ref[j * tq:(j + 1) * tq, hd * hw:(hd + 1) * hw])
                    if j == i:
                        s = jnp.where(keep, s, NEG)
                    tiles[hd].append(s)
                    run_max[hd] = s if run_max[hd] is None else jnp.maximum(run_max[hd], s)
                if j < n_prev:
                    p = jnp.exp2(prev[hd][1][j] - prev[hd][2]).astype(BF16)
                    pv = _dot(p, v_ext[hd][j * tq:(j + 1) * tq, :])
                    acc[hd] = pv if acc[hd] is None else acc[hd] + pv
        for hd in range(heads):
            if prev[hd] is not None:
                finish(hd, prev[hd][0], acc[hd])
            prev[hd] = (None if i is None else
                        (i, tiles[hd], jnp.max(run_max[hd], axis=-1, keepdims=True)))


def _diff_attention(qk, v, lq1, lk1, lq2, lk2, g_out, batch, seq, lam_init):
    t = batch * seq
    width = DIFF_HEADS_PER_STEP * 2 * DIFF_D
    groups = DIFF_HEADS // DIFF_HEADS_PER_STEP
    vec64 = pl.BlockSpec((1, DIFF_D), lambda b, h: (0, 0))
    head = pl.BlockSpec((seq, width), lambda b, h: (b, h))
    return pl.pallas_call(
        functools.partial(_diff_kernel, lam_init=lam_init, seq=seq),
        grid=(batch, groups),
        in_specs=[head,
                  pl.BlockSpec((seq, width), lambda b, h: (b, groups + h)),
                  head,
                  vec64, vec64, vec64, vec64,
                  pl.BlockSpec((1, 2 * DIFF_D), lambda b, h: (0, 0))],
        out_specs=head,
        out_shape=jax.ShapeDtypeStruct((t, DIFF_HEADS * 2 * DIFF_D), BF16),
        compiler_params=_params(("parallel", "parallel")),
        name="diff_attention",
    )(qk, qk, v, lq1, lk1, lq2, lk2, g_out)


def _ret_merge_kernel(x_ref, qk_ref, v_ref, g_ref, dif_ref, mo_ref, gt_ref, dec_ref, qd_ref, kd_ref,
                      wr_hbm, wd_hbm, wm_hbm, wo_hbm, o_ref,
                      st_ref, ret_ref, wr_ref, wd_ref, wm_ref, wo_ref, stage, sem, *, tiles_per_row):
    i = pl.program_id(0)

    @pl.when(i == 0)
    def _():
        for src, dst in ((wr_hbm, wr_ref), (wd_hbm, wd_ref), (wm_hbm, wm_ref), (wo_hbm, wo_ref)):
            _load_weight(src, 0, dst, stage, sem)

    @pl.when(i % tiles_per_row == 0)
    def _():
        st_ref[...] = jnp.zeros_like(st_ref)

    tm = x_ref.shape[0]
    for c in range(tm // RET_C):
        lo = c * RET_C
        for hd in range(RET_HEADS):
            qd = qd_ref[hd]
            q = qk_ref[lo:lo + RET_C, hd * RET_DK:(hd + 1) * RET_DK]
            k = qk_ref[lo:lo + RET_C, RET_QK_W + hd * RET_DK:RET_QK_W + (hd + 1) * RET_DK]
            v = v_ref[lo:lo + RET_C, hd * RET_DV:(hd + 1) * RET_DV]
            st = st_ref[hd]
            s = _dot_nt(q, k) * dec_ref[hd]
            out = _dot(s.astype(BF16), v) + _dot(q, st.astype(BF16)) * qd
            upd = _dot_tn((k.astype(F32) * kd_ref[hd]).astype(BF16), v)
            st_ref[hd] = qd[RET_C - 1:RET_C, :] * st + upd
            gate = g_ref[lo:lo + RET_C, hd * RET_DV:(hd + 1) * RET_DV].astype(F32)
            ret_ref[lo:lo + RET_C, hd * RET_DV:(hd + 1) * RET_DV] = (_rms(out) * gate).astype(BF16)

    gt = gt_ref[...].astype(F32)
    y = (gt[:, :D_MODEL] * _dot(ret_ref[...], wr_ref[...])
         + gt[:, D_MODEL:2 * D_MODEL] * _dot(dif_ref[...], wd_ref[...])
         + gt[:, 2 * D_MODEL:] * _dot(mo_ref[...], wm_ref[...]))
    o_ref[...] = x_ref[...] + _dot(y.astype(BF16), wo_ref[...])


def _ret_merge(x, qk, v, gate, dif, mo, gates, dec, qd, kd, wr, wd, wm, wo, seq):
    t = x.shape[0]
    row = lambda w: pl.BlockSpec((TM, w), lambda i: (i, 0))
    const = lambda shape: pl.BlockSpec(shape, lambda i: (0,) * len(shape))
    return pl.pallas_call(
        functools.partial(_ret_merge_kernel, tiles_per_row=seq // TM),
        grid=(t // TM,),
        in_specs=[row(D_MODEL), row(2 * RET_QK_W), row(RET_V_W), row(RET_V_W), row(D_MODEL),
                  row(D_MODEL), row(3 * D_MODEL),
                  const(dec.shape), const(qd.shape), const(kd.shape),
                  _ANY, _ANY, _ANY, _ANY],
        out_specs=row(D_MODEL),
        out_shape=jax.ShapeDtypeStruct((t, D_MODEL), F32),
        scratch_shapes=[pltpu.VMEM((RET_HEADS, RET_DK, RET_DV), F32),
                        pltpu.VMEM((TM, RET_V_W), BF16),
                        pltpu.VMEM(wr.shape, BF16), pltpu.VMEM(wd.shape, BF16),
                        pltpu.VMEM(wm.shape, BF16), pltpu.VMEM(wo.shape, BF16)]
        + _weight_scratch(D_MODEL, D_MODEL, slots=STAGE_SLOTS + 1)[1:],
        compiler_params=_params(("arbitrary",)),
        name="ret_merge",
    )(x, qk, v, gate, dif, mo, gates, dec, qd, kd, wr, wd, wm, wo)


def _retention_constants():
    h = np.arange(RET_HEADS, dtype=np.float64)
    log_g = np.log1p(-(2.0 ** (-5.0 - h)))
    idx = np.arange(RET_C, dtype=np.float64)
    dist = idx[:, None] - idx[None, :]
    dec = np.where(dist >= 0, np.exp(log_g[:, None, None] * np.maximum(dist, 0.0)), 0.0)
    qd = np.exp(log_g[:, None] * (idx + 1.0))[:, :, None]
    kd = np.exp(log_g[:, None] * (RET_C - 1.0 - idx))[:, :, None]
    return (jnp.asarray(dec, F32), jnp.asarray(qd, F32), jnp.asarray(kd, F32))


def kernel(x, mem, positions, g_ffn1, w_ffn1_in, w_ffn1_out, g_mix, w_in, g_diff_q, g_diff_k,
           lam_q1, lam_k1, lam_q2, lam_k2, g_diff_out, g_mem_q, g_mem_k, g_mem, w_mem_kv,
           w_br_ret, w_br_diff, w_br_mem, w_o, g_ffn2, w_ffn2_in, w_ffn2_out):
    batch, seq, _ = x.shape
    mlen = mem.shape[1]
    t = batch * seq
    depth = g_ffn1.shape[0]
    half = RET_DK // 2
    inv = jnp.asarray(ROPE_BASE ** (-np.arange(half, dtype=np.float64) / half), F32)[None, :]
    dec, qd, kd = _retention_constants()
    pos = positions.reshape(t, 1)
    mem2d = mem.reshape(batch * mlen, D_MODEL)
    xf = x.reshape(t, D_MODEL)
    f32 = lambda w: w.astype(F32)
    vec = lambda g: g.astype(F32)[None, :]

    for l in range(depth):
        lam_init = 0.8 - 0.6 * math.exp(-0.3 * l)
        x1, h = _ffn(xf, vec(g_ffn1[l]), f32(w_ffn1_in[l]), f32(w_ffn1_out[l]), vec(g_mix[l]))

        w = f32(w_in[l])
        rqk, rv, rg = _proj_ret(h, w, pos, inv)
        qk_gain = jnp.concatenate([jnp.tile(g_diff_q[l].astype(F32), 2 * DIFF_HEADS) * (DIFF_D ** -0.5 * LOG2E),
                                   jnp.tile(g_diff_k[l].astype(F32), 2 * DIFF_HEADS)])[None, :]
        mq_gain = (jnp.tile(g_mem_q[l].astype(F32), MEM_HEADS) * (MEM_D ** -0.5 * LOG2E))[None, :]
        mk_gain = jnp.tile(g_mem_k[l].astype(F32), MEM_HEADS)[None, :]
        dqk, dv, mo, gates = _proj_mix(h, w, qk_gain, mq_gain, mem2d, vec(g_mem[l]), mk_gain,
                                       f32(w_mem_kv[l]), seq)

        go = (g_diff_out[l].astype(F32) * (1.0 - lam_init))[None, :]
        dif = _diff_attention(dqk, dv, vec(lam_q1[l]), vec(lam_k1[l]), vec(lam_q2[l]),
                              vec(lam_k2[l]), go, batch, seq, lam_init)

        x2 = _ret_merge(x1, rqk, rv, rg, dif, mo, gates, dec, qd, kd, f32(w_br_ret[l]),
                        f32(w_br_diff[l]), f32(w_br_mem[l]), f32(w_o[l]), seq)
        (xf,) = _ffn(x2, vec(g_ffn2[l]), f32(w_ffn2_in[l]), f32(w_ffn2_out[l]))
    return xf.reshape(batch, seq, D_MODEL)
```

```python
import functools
import math

import jax
import jax.numpy as jnp
import numpy as np
from jax import lax
from jax.experimental import pallas as pl
from jax.experimental.pallas import tpu as pltpu

F32 = jnp.float32
BF16 = jnp.bfloat16

D_MODEL = 1024
D_FF = 2816
FFN_RES = 0.5
EPS = 1e-6
NEG = -1e30
LOG2E = math.log2(math.e)
ROPE_BASE = 10000.0

RET_HEADS = 4
RET_DK = 256
RET_DV = 512
DIFF_HEADS = 8
DIFF_D = 64
MEM_HEADS = 4
MEM_D = 256

V7X_VMEM_LIMIT_BYTES = 56 * 1024 * 1024
LANES = 128

TM = 512
TM_FFN = 512
TM_RET = 1024
FFN_CHUNK = 256
RET_C = 256
DIFF_TQ = 256
DIFF_HEADS_PER_STEP = 2


def _params(sem):
    return pltpu.CompilerParams(dimension_semantics=sem,
                                vmem_limit_bytes=V7X_VMEM_LIMIT_BYTES)


def _rms(x):
    return x * lax.rsqrt(jnp.mean(x * x, axis=-1, keepdims=True) + EPS)


def _dot(a, b):
    return jnp.dot(a, b, preferred_element_type=F32)


def _dot_nt(a, b):
    return lax.dot_general(a, b, (((1,), (1,)), ((), ())), preferred_element_type=F32)


def _dot_tn(a, b):
    return lax.dot_general(a, b, (((0,), (0,)), ((), ())), preferred_element_type=F32)


STAGE_BYTES = 1024 * 1024
STAGE_SLOTS = 4


def _stage_rows(width):
    return max(16, (STAGE_BYTES // (4 * width)) // 16 * 16)


def _weight_scratch(rows, width, slots=STAGE_SLOTS):
    return [pltpu.VMEM((rows, width), BF16),
            pltpu.VMEM((slots, _stage_rows(width), width), F32),
            pltpu.SemaphoreType.DMA((slots,))]


def _load_weight(w_hbm, col0, w_vmem, stage, sem):
    rows, width = w_vmem.shape
    slots, step = stage.shape[0], stage.shape[1]
    starts = list(range(0, rows, step))

    def copy(c):
        n = min(step, rows - starts[c])
        return pltpu.make_async_copy(
            w_hbm.at[pl.ds(starts[c], n), pl.ds(col0, width)],
            stage.at[c % slots, pl.ds(0, n), :], sem.at[c % slots])

    for c in range(min(slots - 1, len(starts))):
        copy(c).start()
    for c in range(len(starts)):
        if c + slots - 1 < len(starts):
            copy(c + slots - 1).start()
        copy(c).wait()
        n = min(step, rows - starts[c])
        w_vmem[starts[c]:starts[c] + n, :] = stage[c % slots, 0:n, :].astype(BF16)


_ANY = pl.BlockSpec(memory_space=pl.ANY)


FFN_RING = 3


def _ffn_kernel(*refs, emit_next):
    if emit_next:
        x_ref, g_ref, win_hbm, wout_hbm, gn_ref, o_ref, h_ref = refs[:7]
    else:
        x_ref, g_ref, win_hbm, wout_hbm, o_ref = refs[:5]
    win_ref, wout_ref, sg_ref, su_ref, so_ref, sem = refs[-6:]
    n_chunks = D_FF // FFN_CHUNK

    def copies(j):
        lo, slot = j * FFN_CHUNK, j % FFN_RING
        return (pltpu.make_async_copy(win_hbm.at[:, pl.ds(lo, FFN_CHUNK)],
                                      sg_ref.at[slot], sem.at[slot, 0]),
                pltpu.make_async_copy(win_hbm.at[:, pl.ds(D_FF + lo, FFN_CHUNK)],
                                      su_ref.at[slot], sem.at[slot, 1]),
                pltpu.make_async_copy(wout_hbm.at[pl.ds(lo, FFN_CHUNK), :],
                                      so_ref.at[slot], sem.at[slot, 2]))

    def fetch_chunk(j):
        if j == 0:
            for c in range(min(FFN_RING - 1, n_chunks)):
                for cp in copies(c):
                    cp.start()
        if j + FFN_RING - 1 < n_chunks:
            for cp in copies(j + FFN_RING - 1):
                cp.start()
        for cp in copies(j):
            cp.wait()
        lo, slot = j * FFN_CHUNK, j % FFN_RING
        win_ref[:, lo:lo + FFN_CHUNK] = sg_ref[slot].astype(BF16)
        win_ref[:, D_FF + lo:D_FF + lo + FFN_CHUNK] = su_ref[slot].astype(BF16)
        wout_ref[lo:lo + FFN_CHUNK, :] = so_ref[slot].astype(BF16)

    def body(before_chunk):
        x = x_ref[...]
        hb = (_rms(x) * g_ref[...]).astype(BF16)
        acc = jnp.zeros(x.shape, F32)
        for j in range(n_chunks):
            before_chunk(j)
            lo = j * FFN_CHUNK
            a = _dot(hb, win_ref[:, lo:lo + FFN_CHUNK])
            b = _dot(hb, win_ref[:, D_FF + lo:D_FF + lo + FFN_CHUNK])
            act = (a * jax.nn.sigmoid(a) * b).astype(BF16)
            acc = acc + _dot(act, wout_ref[lo:lo + FFN_CHUNK, :])
        y = x + FFN_RES * acc
        o_ref[...] = y
        if emit_next:
            h_ref[...] = (_rms(y) * gn_ref[...]).astype(BF16)

    @pl.when(pl.program_id(0) == 0)
    def _():
        body(fetch_chunk)

    @pl.when(pl.program_id(0) != 0)
    def _():
        body(lambda j: None)


def _ffn(x, g, w_in, w_out, g_next=None):
    t = x.shape[0]
    emit = g_next is not None
    row = pl.BlockSpec((TM_FFN, D_MODEL), lambda i: (i, 0))
    vec = pl.BlockSpec((1, D_MODEL), lambda i: (0, 0))
    in_specs = [row, vec, _ANY, _ANY]
    args = [x, g, w_in, w_out]
    out_shape = [jax.ShapeDtypeStruct((t, D_MODEL), F32)]
    out_specs = [row]
    if emit:
        in_specs.append(vec)
        args.append(g_next)
        out_shape.append(jax.ShapeDtypeStruct((t, D_MODEL), BF16))
        out_specs.append(row)
    return pl.pallas_call(
        functools.partial(_ffn_kernel, emit_next=emit),
        grid=(t // TM_FFN,),
        in_specs=in_specs,
        out_specs=out_specs,
        out_shape=out_shape,
        scratch_shapes=[pltpu.VMEM(w_in.shape, BF16), pltpu.VMEM(w_out.shape, BF16),
                        pltpu.VMEM((FFN_RING, D_MODEL, FFN_CHUNK), F32),
                        pltpu.VMEM((FFN_RING, D_MODEL, FFN_CHUNK), F32),
                        pltpu.VMEM((FFN_RING, FFN_CHUNK, D_MODEL), F32),
                        pltpu.SemaphoreType.DMA((FFN_RING, 3))],
        compiler_params=_params(("arbitrary",)),
        name="ffn_next" if emit else "ffn",
    )(*args)


def _group_norm_128(y, gain, group):
    cols = y.shape[1]
    outs = []
    if 2 * group == LANES:
        lane = lax.broadcasted_iota(jnp.int32, (1, LANES), 1)
        first = lane < group
        for c in range(cols // LANES):
            blk = y[:, c * LANES:(c + 1) * LANES]
            sq = blk * blk
            s_lo = jnp.sum(jnp.where(first, sq, 0.0), axis=-1, keepdims=True)
            s_hi = jnp.sum(jnp.where(first, 0.0, sq), axis=-1, keepdims=True)
            r = jnp.where(first, lax.rsqrt(s_lo * (1.0 / group) + EPS),
                          lax.rsqrt(s_hi * (1.0 / group) + EPS))
            outs.append(blk * r)
    else:
        for c in range(cols // group):
            blk = y[:, c * group:(c + 1) * group]
            outs.append(_rms(blk))
    return jnp.concatenate(outs, axis=1) * gain


RET_QK_W = RET_HEADS * RET_DK
RET_V_W = RET_HEADS * RET_DV
PROJ_RET_W = 2 * RET_QK_W + 2 * RET_V_W
DIFF_W = DIFF_HEADS * 2 * DIFF_D
MEM_W = MEM_HEADS * MEM_D
PROJ_MIX_W = 2 * DIFF_W + DIFF_W + MEM_W + 3 * D_MODEL
PROJ_CHUNK = 512
QK_CHUNK = 256

def _proj_ret_kernel(irregular_ref, h_ref, w_hbm, pos_ref, inv_ref, qk_ref, v_ref, g_ref,
                     w_ref, w_stage, w_sem, base_cos, base_sin, cos_ref, sin_ref):
    tm = h_ref.shape[0]
    inv = inv_ref[...]
    offs = lax.broadcasted_iota(jnp.int32, (tm, 1), 0)

    @pl.when(pl.program_id(0) == 0)
    def _():
        _load_weight(w_hbm, 0, w_ref, w_stage, w_sem)
        ang = offs.astype(F32) * inv
        base_cos[...] = jnp.cos(ang)
        base_sin[...] = jnp.sin(ang)
        cos_ref[...] = jnp.zeros_like(cos_ref)
        sin_ref[...] = jnp.zeros_like(sin_ref)

    irregular = irregular_ref[pl.program_id(0)] != 0

    @pl.when(irregular)
    def _():
        ang = pos_ref[...].astype(F32) * inv
        cos_ref[...] = jnp.cos(ang)
        sin_ref[...] = jnp.sin(ang)

    h = h_ref[...]
    a0 = pos_ref[0:1, :].astype(F32) * inv
    c0, s0 = jnp.cos(a0), jnp.sin(a0)
    bc, bs = base_cos[...], base_sin[...]
    consecutive = jnp.logical_not(irregular)
    cos = jnp.where(consecutive, c0 * bc - s0 * bs, cos_ref[...])
    sin = jnp.where(consecutive, s0 * bc + c0 * bs, sin_ref[...])
    half = RET_DK // 2

    def rot(lhs, hd):
        lo = hd * RET_DK
        y = _dot(lhs, w_ref[:, lo:lo + RET_DK])
        x1 = y[:, :half]
        x2 = y[:, half:]
        r1 = x1 * cos - x2 * sin
        r2 = x1 * sin + x2 * cos
        if hd >= RET_HEADS:
            r1 = r1 * (RET_DK ** -0.5)
            r2 = r2 * (RET_DK ** -0.5)
        r1 = r1.astype(BF16)
        r2 = r2.astype(BF16)
        qk_ref[:, lo:lo + half] = r1
        qk_ref[:, lo + half:lo + RET_DK] = r2
        return r1, r2

    def val(lhs, c):
        lo = c * PROJ_CHUNK
        v_ref[:, lo:lo + PROJ_CHUNK] = _dot(
            lhs, w_ref[:, 2 * RET_QK_W + lo:2 * RET_QK_W + lo + PROJ_CHUNK]).astype(BF16)

    def gate(lhs, c):
        lo = c * PROJ_CHUNK
        wlo = 2 * RET_QK_W + RET_V_W + lo
        y = _dot(lhs, w_ref[:, wlo:wlo + PROJ_CHUNK])
        g_ref[:, lo:lo + PROJ_CHUNK] = (y * jax.nn.sigmoid(y)).astype(BF16)

    for hd in range(2 * RET_HEADS):
        rot(h, hd)
        val(h, hd // 2) if hd % 2 == 0 else gate(h, hd // 2)


def _proj_ret(h, w, pos, inv):
    t = h.shape[0]
    tiles = pos.reshape(t // TM_RET, TM_RET)
    irregular = jnp.any(tiles != tiles[:, :1] + jnp.arange(TM_RET, dtype=pos.dtype)[None, :],
                        axis=1).astype(jnp.int32)
    row = lambda width: pl.BlockSpec((TM_RET, width), lambda i, flags: (i, 0))
    return pl.pallas_call(
        _proj_ret_kernel,
        grid_spec=pltpu.PrefetchScalarGridSpec(
            num_scalar_prefetch=1,
            grid=(t // TM_RET,),
            in_specs=[row(D_MODEL), _ANY, row(1),
                      pl.BlockSpec((1, RET_DK // 2), lambda i, flags: (0, 0))],
            out_specs=[row(2 * RET_QK_W), row(RET_V_W), row(RET_V_W)],
            scratch_shapes=_weight_scratch(D_MODEL, PROJ_RET_W, slots=2 * STAGE_SLOTS)
            + [pltpu.VMEM((TM_RET, RET_DK // 2), F32)] * 4),
        out_shape=[jax.ShapeDtypeStruct((t, 2 * RET_QK_W), BF16),
                   jax.ShapeDtypeStruct((t, RET_V_W), BF16),
                   jax.ShapeDtypeStruct((t, RET_V_W), BF16)],
        compiler_params=_params(("arbitrary",)),
        name="proj_ret",
    )(irregular, h, w, pos, inv)


def _proj_mix_kernel(h_ref, w_hbm, gqk_ref, gmq_ref, mem_ref, gmem_ref, gmk_ref, wkv_hbm,
                     qk_ref, v_ref, mo_ref, gt_ref,
                     w_ref, w_stage, w_sem, wkv_ref, wkv_stage, wkv_sem, mk_ref, mv_ref,
                     *, tiles_per_row):
    @pl.when(pl.program_id(0) == 0)
    def _():
        _load_weight(w_hbm, PROJ_RET_W, w_ref, w_stage, w_sem)
        _load_weight(wkv_hbm, 0, wkv_ref, wkv_stage, wkv_sem)

    @pl.when(pl.program_id(0) % tiles_per_row == 0)
    def _():
        mn = (_rms(mem_ref[...]) * gmem_ref[...]).astype(BF16)
        kv = _dot(mn, wkv_ref[...])
        mk_ref[...] = _group_norm_128(kv[:, :MEM_W], gmk_ref[...], MEM_D).astype(BF16)
        mv_ref[...] = kv[:, MEM_W:].astype(BF16)

    h = h_ref[...]
    off_qk = 0
    off_v = off_qk + 2 * DIFF_W
    off_mq = off_v + DIFF_W
    off_gt = off_mq + MEM_W

    mem_q, mem_p = {}, {}

    def qk_norm(lhs, c):
        lo = c * QK_CHUNK
        y = _dot(lhs, w_ref[:, off_qk + lo:off_qk + lo + QK_CHUNK])
        out = _group_norm_128(y, gqk_ref[:, lo:lo + QK_CHUNK], DIFF_D).astype(BF16)
        qk_ref[:, lo:lo + QK_CHUNK] = out
        return out

    def val(lhs, c):
        lo = c * PROJ_CHUNK
        out = _dot(lhs, w_ref[:, off_v + lo:off_v + lo + PROJ_CHUNK]).astype(BF16)
        v_ref[:, lo:lo + PROJ_CHUNK] = out
        return out

    def gate(lhs, c):
        lo = c * PROJ_CHUNK
        y = _dot(lhs, w_ref[:, off_gt + lo:off_gt + lo + PROJ_CHUNK])
        out = jax.nn.sigmoid(y).astype(BF16)
        gt_ref[:, lo:lo + PROJ_CHUNK] = out
        return out

    def mem_query(lhs, hd):
        sl = slice(hd * MEM_D, (hd + 1) * MEM_D)
        y = _dot(lhs, w_ref[:, off_mq + hd * MEM_D:off_mq + (hd + 1) * MEM_D])
        mem_q[hd] = (_rms(y) * gmq_ref[:, sl]).astype(BF16)
        return mem_q[hd]

    def mem_scores(lhs, hd):
        sl = slice(hd * MEM_D, (hd + 1) * MEM_D)
        s = _dot_nt(mem_q[hd], mk_ref[:, sl])
        p = jnp.exp2(s - jnp.max(s, axis=-1, keepdims=True))
        mem_p[hd] = (p.astype(BF16), jnp.sum(p, axis=-1, keepdims=True))
        return mem_p[hd][0]

    def mem_values(lhs, hd):
        sl = slice(hd * MEM_D, (hd + 1) * MEM_D)
        p, l = mem_p[hd]
        out = (_dot(p, mv_ref[:, sl]) / l).astype(BF16)
        mo_ref[:, sl] = out
        return out

    order = [(gate, 0), (mem_query, 0), (qk_norm, 0), (mem_scores, 0), (qk_norm, 1),
             (mem_values, 0), (gate, 1), (mem_query, 1), (qk_norm, 2), (mem_scores, 1),
             (qk_norm, 3), (mem_values, 1), (gate, 2), (mem_query, 2), (qk_norm, 4),
             (mem_scores, 2), (qk_norm, 5), (mem_values, 2), (gate, 3), (mem_query, 3),
             (qk_norm, 6), (mem_scores, 3), (qk_norm, 7), (mem_values, 3), (gate, 4),
             (gate, 5), (val, 0), (val, 1)]
    for fn, idx in order:
        fn(h, idx)


def _proj_mix(h, w, gqk, gmq, mem2d, g_mem, gmk, w_kv, seq):
    t = h.shape[0]
    mlen = mem2d.shape[0] // (t // seq)
    tiles_per_batch = seq // TM
    row = lambda width: pl.BlockSpec((TM, width), lambda i: (i, 0))
    vec = lambda width: pl.BlockSpec((1, width), lambda i: (0, 0))
    mem = pl.BlockSpec((mlen, D_MODEL), lambda i: (i // tiles_per_batch, 0))
    return pl.pallas_call(
        functools.partial(_proj_mix_kernel, tiles_per_row=tiles_per_batch),
        grid=(t // TM,),
        in_specs=[row(D_MODEL), _ANY, vec(2 * DIFF_W), vec(MEM_W), mem, vec(D_MODEL), vec(MEM_W),
                  _ANY],
        out_specs=[row(2 * DIFF_W), row(DIFF_W), row(MEM_W), row(3 * D_MODEL)],
        out_shape=[jax.ShapeDtypeStruct((t, 2 * DIFF_W), BF16),
                   jax.ShapeDtypeStruct((t, DIFF_W), BF16),
                   jax.ShapeDtypeStruct((t, MEM_W), BF16),
                   jax.ShapeDtypeStruct((t, 3 * D_MODEL), BF16)],
        scratch_shapes=_weight_scratch(D_MODEL, PROJ_MIX_W, slots=STAGE_SLOTS + 2)
        + _weight_scratch(*w_kv.shape, slots=STAGE_SLOTS - 1)
        + [pltpu.VMEM((mlen, MEM_W), BF16), pltpu.VMEM((mlen, MEM_W), BF16)],
        compiler_params=_params(("arbitrary",)),
        name="proj_mix",
    )(h, w, gqk, gmq, mem2d, g_mem, gmk, w_kv)


def _diff_kernel(q_ref, k_ref, v_ref, lq1_ref, lk1_ref, lq2_ref, lk2_ref, go_ref, o_ref,
                 *, lam_init, seq):
    tq = DIFF_TQ
    lane = lax.broadcasted_iota(jnp.int32, (1, 2 * DIFF_D), 1)
    first_map = lane < DIFF_D
    r = lax.broadcasted_iota(jnp.int32, (tq, tq), 0)
    cidx = lax.broadcasted_iota(jnp.int32, (tq, tq), 1)
    keep = cidx <= r
    keep = jnp.concatenate([keep, keep], axis=0)
    lam = (jnp.exp(jnp.sum(lq1_ref[...] * lk1_ref[...], axis=-1, keepdims=True))
           - jnp.exp(jnp.sum(lq2_ref[...] * lk2_ref[...], axis=-1, keepdims=True))
           + lam_init)
    go = go_ref[...]
    nq = seq // tq
    hw = 2 * DIFF_D
    heads = DIFF_HEADS_PER_STEP
    ones = jnp.ones((seq, hw), BF16)
    v_ext = [jnp.concatenate([v_ref[:, hd * hw:(hd + 1) * hw], ones], axis=1)
             for hd in range(heads)]

    def finish(hd, i, o_ext):
        r_ = o_ext[:, :hw] / o_ext[:, hw:]
        d = r_[:tq] - lam * r_[tq:]
        o_ref[i * tq:(i + 1) * tq, hd * hw:(hd + 1) * hw] = (_rms(d) * go).astype(BF16)

    prev = [None] * heads
    for i in list(range(nq)) + [None]:
        tiles = [[] for _ in range(heads)]
        run_max = [None] * heads
        acc = [None] * heads
        qs = [None] * heads
        if i is not None:
            for hd in range(heads):
                q = q_ref[i * tq:(i + 1) * tq, hd * hw:(hd + 1) * hw]
                zero = jnp.zeros_like(q)
                qs[hd] = jnp.concatenate([jnp.where(first_map, q, zero),
                                          jnp.where(first_map, zero, q)], axis=0)
        n_cur = 0 if i is None else i + 1
        n_prev = 0 if prev[0] is None else len(prev[0][1])
        for j in range(max(n_cur, n_prev)):
            for hd in range(heads):
                if j < n_cur:
                    s = _dot_nt(qs[hd], k_ref[j * tq:(j + 1) * tq, hd * hw:(hd + 1) * hw])
                    if j == i:
                        s = jnp.where(keep, s, NEG)
                    tiles[hd].append(s)
                    run_max[hd] = s if run_max[hd] is None else jnp.maximum(run_max[hd], s)
                if j < n_prev:
                    p = jnp.exp2(prev[hd][1][j] - prev[hd][2]).astype(BF16)
                    pv = _dot(p, v_ext[hd][j * tq:(j + 1) * tq, :])
                    acc[hd] = pv if acc[hd] is None else acc[hd] + pv
        for hd in range(heads):
            if prev[hd] is not None:
                finish(hd, prev[hd][0], acc[hd])
            prev[hd] = (None if i is None else
                        (i, tiles[hd], jnp.max(run_max[hd], axis=-1, keepdims=True)))


def _diff_attention(qk, v, lq1, lk1, lq2, lk2, g_out, batch, seq, lam_init):
    t = batch * seq
    width = DIFF_HEADS_PER_STEP * 2 * DIFF_D
    groups = DIFF_HEADS // DIFF_HEADS_PER_STEP
    vec64 = pl.BlockSpec((1, DIFF_D), lambda b, h: (0, 0))
    head = pl.BlockSpec((seq, width), lambda b, h: (b, h))
    return pl.pallas_call(
        functools.partial(_diff_kernel, lam_init=lam_init, seq=seq),
        grid=(batch, groups),
        in_specs=[head,
                  pl.BlockSpec((seq, width), lambda b, h: (b, groups + h)),
                  head,
                  vec64, vec64, vec64, vec64,
                  pl.BlockSpec((1, 2 * DIFF_D), lambda b, h: (0, 0))],
        out_specs=head,
        out_shape=jax.ShapeDtypeStruct((t, DIFF_HEADS * 2 * DIFF_D), BF16),
        compiler_params=_params(("parallel", "parallel")),
        name="diff_attention",
    )(qk, qk, v, lq1, lk1, lq2, lk2, g_out)


def _ret_merge_kernel(x_ref, qk_ref, v_ref, g_ref, dif_ref, mo_ref, gt_ref, dec_ref, qd_ref, kd_ref,
                      wr_hbm, wd_hbm, wm_hbm, wo_hbm, o_ref,
                      st_ref, ret_ref, wr_ref, wd_ref, wm_ref, wo_ref, stage, sem, *, tiles_per_row):
    i = pl.program_id(0)

    @pl.when(i == 0)
    def _():
        for src, dst in ((wr_hbm, wr_ref), (wd_hbm, wd_ref), (wm_hbm, wm_ref), (wo_hbm, wo_ref)):
            _load_weight(src, 0, dst, stage, sem)

    @pl.when(i % tiles_per_row == 0)
    def _():
        st_ref[...] = jnp.zeros_like(st_ref)

    tm = x_ref.shape[0]
    for c in range(tm // RET_C):
        lo = c * RET_C
        for hd in range(RET_HEADS):
            qd = qd_ref[hd]
            q = qk_ref[lo:lo + RET_C, hd * RET_DK:(hd + 1) * RET_DK]
            k = qk_ref[lo:lo + RET_C, RET_QK_W + hd * RET_DK:RET_QK_W + (hd + 1) * RET_DK]
            v = v_ref[lo:lo + RET_C, hd * RET_DV:(hd + 1) * RET_DV]
            st = st_ref[hd]
            s = _dot_nt(q, k) * dec_ref[hd]
            out = _dot(s.astype(BF16), v) + _dot(q, st.astype(BF16)) * qd
            upd = _dot_tn((k.astype(F32) * kd_ref[hd]).astype(BF16), v)
            st_ref[hd] = qd[RET_C - 1:RET_C, :] * st + upd
            gate = g_ref[lo:lo + RET_C, hd * RET_DV:(hd + 1) * RET_DV].astype(F32)
            ret_ref[lo:lo + RET_C, hd * RET_DV:(hd + 1) * RET_DV] = (_rms(out) * gate).astype(BF16)

    gt = gt_ref[...].astype(F32)
    y = (gt[:, :D_MODEL] * _dot(ret_ref[...], wr_ref[...])
         + gt[:, D_MODEL:2 * D_MODEL] * _dot(dif_ref[...], wd_ref[...])
         + gt[:, 2 * D_MODEL:] * _dot(mo_ref[...], wm_ref[...]))
    o_ref[...] = x_ref[...] + _dot(y.astype(BF16), wo_ref[...])


def _ret_merge(x, qk, v, gate, dif, mo, gates, dec, qd, kd, wr, wd, wm, wo, seq):
    t = x.shape[0]
    row = lambda w: pl.BlockSpec((TM, w), lambda i: (i, 0))
    const = lambda shape: pl.BlockSpec(shape, lambda i: (0,) * len(shape))
    return pl.pallas_call(
        functools.partial(_ret_merge_kernel, tiles_per_row=seq // TM),
        grid=(t // TM,),
        in_specs=[row(D_MODEL), row(2 * RET_QK_W), row(RET_V_W), row(RET_V_W), row(D_MODEL),
                  row(D_MODEL), row(3 * D_MODEL),
                  const(dec.shape), const(qd.shape), const(kd.shape),
                  _ANY, _ANY, _ANY, _ANY],
        out_specs=row(D_MODEL),
        out_shape=jax.ShapeDtypeStruct((t, D_MODEL), F32),
        scratch_shapes=[pltpu.VMEM((RET_HEADS, RET_DK, RET_DV), F32),
                        pltpu.VMEM((TM, RET_V_W), BF16),
                        pltpu.VMEM(wr.shape, BF16), pltpu.VMEM(wd.shape, BF16),
                        pltpu.VMEM(wm.shape, BF16), pltpu.VMEM(wo.shape, BF16)]
        + _weight_scratch(D_MODEL, D_MODEL, slots=STAGE_SLOTS + 1)[1:],
        compiler_params=_params(("arbitrary",)),
        name="ret_merge",
    )(x, qk, v, gate, dif, mo, gates, dec, qd, kd, wr, wd, wm, wo)


def _retention_constants():
    h = np.arange(RET_HEADS, dtype=np.float64)
    log_g = np.log1p(-(2.0 ** (-5.0 - h)))
    idx = np.arange(RET_C, dtype=np.float64)
    dist = idx[:, None] - idx[None, :]
    dec = np.where(dist >= 0, np.exp(log_g[:, None, None] * np.maximum(dist, 0.0)), 0.0)
    qd = np.exp(log_g[:, None] * (idx + 1.0))[:, :, None]
    kd = np.exp(log_g[:, None] * (RET_C - 1.0 - idx))[:, :, None]
    return (jnp.asarray(dec, F32), jnp.asarray(qd, F32), jnp.asarray(kd, F32))


def kernel(x, mem, positions, g_ffn1, w_ffn1_in, w_ffn1_out, g_mix, w_in, g_diff_q, g_diff_k,
           lam_q1, lam_k1, lam_q2, lam_k2, g_diff_out, g_mem_q, g_mem_k, g_mem, w_mem_kv,
           w_br_ret, w_br_diff, w_br_mem, w_o, g_ffn2, w_ffn2_in, w_ffn2_out):
    batch, seq, _ = x.shape
    mlen = mem.shape[1]
    t = batch * seq
    depth = g_ffn1.shape[0]
    half = RET_DK // 2
    inv = jnp.asarray(ROPE_BASE ** (-np.arange(half, dtype=np.float64) / half), F32)[None, :]
    dec, qd, kd = _retention_constants()
    pos = positions.reshape(t, 1)
    mem2d = mem.reshape(batch * mlen, D_MODEL)
    xf = x.reshape(t, D_MODEL)
    f32 = lambda w: w.astype(F32)
    vec = lambda g: g.astype(F32)[None, :]

    for l in range(depth):
        lam_init = 0.8 - 0.6 * math.exp(-0.3 * l)
        x1, h = _ffn(xf, vec(g_ffn1[l]), f32(w_ffn1_in[l]), f32(w_ffn1_out[l]), vec(g_mix[l]))

        w = f32(w_in[l])
        rqk, rv, rg = _proj_ret(h, w, pos, inv)
        qk_gain = jnp.concatenate([jnp.tile(g_diff_q[l].astype(F32), 2 * DIFF_HEADS) * (DIFF_D ** -0.5 * LOG2E),
                                   jnp.tile(g_diff_k[l].astype(F32), 2 * DIFF_HEADS)])[None, :]
        mq_gain = (jnp.tile(g_mem_q[l].astype(F32), MEM_HEADS) * (MEM_D ** -0.5 * LOG2E))[None, :]
        mk_gain = jnp.tile(g_mem_k[l].astype(F32), MEM_HEADS)[None, :]
        dqk, dv, mo, gates = _proj_mix(h, w, qk_gain, mq_gain, mem2d, vec(g_mem[l]), mk_gain,
                                       f32(w_mem_kv[l]), seq)

        go = (g_diff_out[l].astype(F32) * (1.0 - lam_init))[None, :]
        dif = _diff_attention(dqk, dv, vec(lam_q1[l]), vec(lam_k1[l]), vec(lam_q2[l]),
                              vec(lam_k2[l]), go, batch, seq, lam_init)

        x2 = _ret_merge(x1, rqk, rv, rg, dif, mo, gates, dec, qd, kd, f32(w_br_ret[l]),
                        f32(w_br_diff[l]), f32(w_br_mem[l]), f32(w_o[l]), seq)
        (xf,) = _ffn(x2, vec(g_ffn2[l]), f32(w_ffn2_in[l]), f32(w_ffn2_out[l]))
    return xf.reshape(batch, seq, D_MODEL)
```

```python
import functools
import math

import jax
import jax.numpy as jnp
import numpy as np
from jax import lax
from jax.experimental import pallas as pl
from jax.experimental.pallas import tpu as pltpu

F32 = jnp.float32
BF16 = jnp.bfloat16

D_MODEL = 1024
D_FF = 2816
FFN_RES = 0.5
EPS = 1e-6
NEG = -1e30
LOG2E = math.log2(math.e)
ROPE_BASE = 10000.0

RET_HEADS = 4
RET_DK = 256
RET_DV = 512
DIFF_HEADS = 8
DIFF_D = 64
MEM_HEADS = 4
MEM_D = 256

V7X_VMEM_LIMIT_BYTES = 56 * 1024 * 1024
LANES = 128

TM = 512
TM_FFN = 512
TM_RET = 1024
FFN_CHUNK = 256
RET_C = 256
DIFF_TQ = 256
DIFF_HEADS_PER_STEP = 2


def _params(sem):
    return pltpu.CompilerParams(dimension_semantics=sem,
                                vmem_limit_bytes=V7X_VMEM_LIMIT_BYTES)


def _rms(x):
    return x * lax.rsqrt(jnp.mean(x * x, axis=-1, keepdims=True) + EPS)


def _dot(a, b):
    return jnp.dot(a, b, preferred_element_type=F32)


def _dot_nt(a, b):
    return lax.dot_general(a, b, (((1,), (1,)), ((), ())), preferred_element_type=F32)


def _dot_tn(a, b):
    return lax.dot_general(a, b, (((0,), (0,)), ((), ())), preferred_element_type=F32)


STAGE_BYTES = 1024 * 1024
STAGE_SLOTS = 4


def _stage_rows(width):
    return max(16, (STAGE_BYTES // (4 * width)) // 16 * 16)


def _weight_scratch(rows, width, slots=STAGE_SLOTS):
    return [pltpu.VMEM((rows, width), BF16),
            pltpu.VMEM((slots, _stage_rows(width), width), F32),
            pltpu.SemaphoreType.DMA((slots,))]


def _load_weight(w_hbm, col0, w_vmem, stage, sem):
    rows, width = w_vmem.shape
    slots, step = stage.shape[0], stage.shape[1]
    starts = list(range(0, rows, step))

    def copy(c):
        n = min(step, rows - starts[c])
        return pltpu.make_async_copy(
            w_hbm.at[pl.ds(starts[c], n), pl.ds(col0, width)],
            stage.at[c % slots, pl.ds(0, n), :], sem.at[c % slots])

    for c in range(min(slots - 1, len(starts))):
        copy(c).start()
    for c in range(len(starts)):
        if c + slots - 1 < len(starts):
            copy(c + slots - 1).start()
        copy(c).wait()
        n = min(step, rows - starts[c])
        w_vmem[starts[c]:starts[c] + n, :] = stage[c % slots, 0:n, :].astype(BF16)


_ANY = pl.BlockSpec(memory_space=pl.ANY)


FFN_RING = 3


def _ffn_kernel(*refs, emit_next):
    if emit_next:
        x_ref, g_ref, win_hbm, wout_hbm, gn_ref, o_ref, h_ref = refs[:7]
    else:
        x_ref, g_ref, win_hbm, wout_hbm, o_ref = refs[:5]
    win_ref, wout_ref, sg_ref, su_ref, so_ref, sem = refs[-6:]
    n_chunks = D_FF // FFN_CHUNK

    def copies(j):
        lo, slot = j * FFN_CHUNK, j % FFN_RING
        return (pltpu.make_async_copy(win_hbm.at[:, pl.ds(lo, FFN_CHUNK)],
                                      sg_ref.at[slot], sem.at[slot, 0]),
                pltpu.make_async_copy(win_hbm.at[:, pl.ds(D_FF + lo, FFN_CHUNK)],
                                      su_ref.at[slot], sem.at[slot, 1]),
                pltpu.make_async_copy(wout_hbm.at[pl.ds(lo, FFN_CHUNK), :],
                                      so_ref.at[slot], sem.at[slot, 2]))

    def fetch_chunk(j):
        if j == 0:
            for c in range(min(FFN_RING - 1, n_chunks)):
                for cp in copies(c):
                    cp.start()
        if j + FFN_RING - 1 < n_chunks:
            for cp in copies(j + FFN_RING - 1):
                cp.start()
        for cp in copies(j):
            cp.wait()
        lo, slot = j * FFN_CHUNK, j % FFN_RING
        win_ref[:, lo:lo + FFN_CHUNK] = sg_ref[slot].astype(BF16)
        win_ref[:, D_FF + lo:D_FF + lo + FFN_CHUNK] = su_ref[slot].astype(BF16)
        wout_ref[lo:lo + FFN_CHUNK, :] = so_ref[slot].astype(BF16)

    def body(before_chunk):
        x = x_ref[...]
        hb = (_rms(x) * g_ref[...]).astype(BF16)
        acc = jnp.zeros(x.shape, F32)
        for j in range(n_chunks):
            before_chunk(j)
            lo = j * FFN_CHUNK
            a = _dot(hb, win_ref[:, lo:lo + FFN_CHUNK])
            b = _dot(hb, win_ref[:, D_FF + lo:D_FF + lo + FFN_CHUNK])
            act = (a * jax.nn.sigmoid(a) * b).astype(BF16)
            acc = acc + _dot(act, wout_ref[lo:lo + FFN_CHUNK, :])
        y = x + FFN_RES * acc
        o_ref[...] = y
        if emit_next:
            h_ref[...] = (_rms(y) * gn_ref[...]).astype(BF16)

    @pl.when(pl.program_id(0) == 0)
    def _():
        body(fetch_chunk)

    @pl.when(pl.program_id(0) != 0)
    def _():
        body(lambda j: None)


def _ffn(x, g, w_in, w_out, g_next=None):
    t = x.shape[0]
    emit = g_next is not None
    row = pl.BlockSpec((TM_FFN, D_MODEL), lambda i: (i, 0))
    vec = pl.BlockSpec((1, D_MODEL), lambda i: (0, 0))
    in_specs = [row, vec, _ANY, _ANY]
    args = [x, g, w_in, w_out]
    out_shape = [jax.ShapeDtypeStruct((t, D_MODEL), F32)]
    out_specs = [row]
    if emit:
        in_specs.append(vec)
        args.append(g_next)
        out_shape.append(jax.ShapeDtypeStruct((t, D_MODEL), BF16))
        out_specs.append(row)
    return pl.pallas_call(
        functools.partial(_ffn_kernel, emit_next=emit),
        grid=(t // TM_FFN,),
        in_specs=in_specs,
        out_specs=out_specs,
        out_shape=out_shape,
        scratch_shapes=[pltpu.VMEM(w_in.shape, BF16), pltpu.VMEM(w_out.shape, BF16),
                        pltpu.VMEM((FFN_RING, D_MODEL, FFN_CHUNK), F32),
                        pltpu.VMEM((FFN_RING, D_MODEL, FFN_CHUNK), F32),
                        pltpu.VMEM((FFN_RING, FFN_CHUNK, D_MODEL), F32),
                        pltpu.SemaphoreType.DMA((FFN_RING, 3))],
        compiler_params=_params(("arbitrary",)),
        name="ffn_next" if emit else "ffn",
    )(*args)


def _group_norm_128(y, gain, group):
    cols = y.shape[1]
    outs = []
    if 2 * group == LANES:
        lane = lax.broadcasted_iota(jnp.int32, (1, LANES), 1)
        first = lane < group
        for c in range(cols // LANES):
            blk = y[:, c * LANES:(c + 1) * LANES]
            sq = blk * blk
            s_lo = jnp.sum(jnp.where(first, sq, 0.0), axis=-1, keepdims=True)
            s_hi = jnp.sum(jnp.where(first, 0.0, sq), axis=-1, keepdims=True)
            r = jnp.where(first, lax.rsqrt(s_lo * (1.0 / group) + EPS),
                          lax.rsqrt(s_hi * (1.0 / group) + EPS))
            outs.append(blk * r)
    else:
        for c in range(cols // group):
            blk = y[:, c * group:(c + 1) * group]
            outs.append(_rms(blk))
    return jnp.concatenate(outs, axis=1) * gain


RET_QK_W = RET_HEADS * RET_DK
RET_V_W = RET_HEADS * RET_DV
PROJ_RET_W = 2 * RET_QK_W + 2 * RET_V_W
DIFF_W = DIFF_HEADS * 2 * DIFF_D
MEM_W = MEM_HEADS * MEM_D
PROJ_MIX_W = 2 * DIFF_W + DIFF_W + MEM_W + 3 * D_MODEL
PROJ_CHUNK = 512
QK_CHUNK = 256

def _proj_ret_kernel(irregular_ref, first_ref, h_ref, w_hbm, pos_ref, inv_ref, qk_ref, v_ref, g_ref,
                     w_ref, w_stage, w_sem, base_cos, base_sin, cos_ref, sin_ref):
    tm = h_ref.shape[0]
    inv = inv_ref[...]
    offs = lax.broadcasted_iota(jnp.int32, (tm, 1), 0)

    @pl.when(pl.program_id(0) == 0)
    def _():
        _load_weight(w_hbm, 0, w_ref, w_stage, w_sem)
        ang = offs.astype(F32) * inv
        base_cos[...] = jnp.cos(ang)
        base_sin[...] = jnp.sin(ang)
        cos_ref[...] = jnp.zeros_like(cos_ref)
        sin_ref[...] = jnp.zeros_like(sin_ref)

    irregular = irregular_ref[pl.program_id(0)] != 0

    @pl.when(irregular)
    def _():
        diag = (lax.broadcasted_iota(jnp.int32, (LANES, LANES), 0)
                == lax.broadcasted_iota(jnp.int32, (LANES, LANES), 1))
        for r in range(tm // LANES):
            p = pos_ref[r:r + 1, :].astype(F32)
            col = jnp.sum(jnp.where(diag, p, 0.0), axis=1, keepdims=True)
            ang = col * inv
            cos_ref[r * LANES:(r + 1) * LANES, :] = jnp.cos(ang)
            sin_ref[r * LANES:(r + 1) * LANES, :] = jnp.sin(ang)

    h = h_ref[...]
    p0 = jnp.full(inv.shape, first_ref[pl.program_id(0)], jnp.int32)
    a0 = p0.astype(F32) * inv
    c0, s0 = jnp.cos(a0), jnp.sin(a0)
    bc, bs = base_cos[...], base_sin[...]
    consecutive = jnp.logical_not(irregular)
    cos = jnp.where(consecutive, c0 * bc - s0 * bs, cos_ref[...])
    sin = jnp.where(consecutive, s0 * bc + c0 * bs, sin_ref[...])
    half = RET_DK // 2

    def rot(lhs, hd):
        lo = hd * RET_DK
        y = _dot(lhs, w_ref[:, lo:lo + RET_DK])
        x1 = y[:, :half]
        x2 = y[:, half:]
        r1 = x1 * cos - x2 * sin
        r2 = x1 * sin + x2 * cos
        if hd >= RET_HEADS:
            r1 = r1 * (RET_DK ** -0.5)
            r2 = r2 * (RET_DK ** -0.5)
        r1 = r1.astype(BF16)
        r2 = r2.astype(BF16)
        qk_ref[:, lo:lo + half] = r1
        qk_ref[:, lo + half:lo + RET_DK] = r2
        return r1, r2

    def val(lhs, c):
        lo = c * PROJ_CHUNK
        v_ref[:, lo:lo + PROJ_CHUNK] = _dot(
            lhs, w_ref[:, 2 * RET_QK_W + lo:2 * RET_QK_W + lo + PROJ_CHUNK]).astype(BF16)

    def gate(lhs, c):
        lo = c * PROJ_CHUNK
        wlo = 2 * RET_QK_W + RET_V_W + lo
        y = _dot(lhs, w_ref[:, wlo:wlo + PROJ_CHUNK])
        g_ref[:, lo:lo + PROJ_CHUNK] = (y * jax.nn.sigmoid(y)).astype(BF16)

    for hd in range(2 * RET_HEADS):
        rot(h, hd)
        val(h, hd // 2) if hd % 2 == 0 else gate(h, hd // 2)


def _proj_ret(h, w, pos, inv):
    t = h.shape[0]
    tiles = pos.reshape(t // TM_RET, TM_RET)
    first = tiles[:, 0].astype(jnp.int32)
    irregular = jnp.any(tiles != tiles[:, :1] + jnp.arange(TM_RET, dtype=pos.dtype)[None, :],
                        axis=1).astype(jnp.int32)
    row = lambda width: pl.BlockSpec((TM_RET, width), lambda i, flags, first: (i, 0))
    return pl.pallas_call(
        _proj_ret_kernel,
        grid_spec=pltpu.PrefetchScalarGridSpec(
            num_scalar_prefetch=2,
            grid=(t // TM_RET,),
            in_specs=[row(D_MODEL), _ANY,
                      pl.BlockSpec((TM_RET // LANES, LANES), lambda i, flags, first: (i, 0)),
                      pl.BlockSpec((1, RET_DK // 2), lambda i, flags, first: (0, 0))],
            out_specs=[row(2 * RET_QK_W), row(RET_V_W), row(RET_V_W)],
            scratch_shapes=_weight_scratch(D_MODEL, PROJ_RET_W, slots=2 * STAGE_SLOTS)
            + [pltpu.VMEM((TM_RET, RET_DK // 2), F32)] * 4),
        out_shape=[jax.ShapeDtypeStruct((t, 2 * RET_QK_W), BF16),
                   jax.ShapeDtypeStruct((t, RET_V_W), BF16),
                   jax.ShapeDtypeStruct((t, RET_V_W), BF16)],
        compiler_params=_params(("arbitrary",)),
        name="proj_ret",
    )(irregular, first, h, w, pos.reshape(t // LANES, LANES), inv)


def _proj_mix_kernel(h_ref, w_hbm, gqk_ref, gmq_ref, mem_ref, gmem_ref, gmk_ref, wkv_hbm,
                     qk_ref, v_ref, mo_ref, gt_ref,
                     w_ref, w_stage, w_sem, wkv_ref, wkv_stage, wkv_sem, mk_ref, mv_ref,
                     *, tiles_per_row):
    @pl.when(pl.program_id(0) == 0)
    def _():
        _load_weight(w_hbm, PROJ_RET_W, w_ref, w_stage, w_sem)
        _load_weight(wkv_hbm, 0, wkv_ref, wkv_stage, wkv_sem)

    @pl.when(pl.program_id(0) % tiles_per_row == 0)
    def _():
        mn = (_rms(mem_ref[...]) * gmem_ref[...]).astype(BF16)
        kv = _dot(mn, wkv_ref[...])
        mk_ref[...] = _group_norm_128(kv[:, :MEM_W], gmk_ref[...], MEM_D).astype(BF16)
        mv_ref[...] = kv[:, MEM_W:].astype(BF16)

    h = h_ref[...]
    off_qk = 0
    off_v = off_qk + 2 * DIFF_W
    off_mq = off_v + DIFF_W
    off_gt = off_mq + MEM_W

    mem_q, mem_p = {}, {}

    def qk_norm(lhs, c):
        lo = c * QK_CHUNK
        y = _dot(lhs, w_ref[:, off_qk + lo:off_qk + lo + QK_CHUNK])
        out = _group_norm_128(y, gqk_ref[:, lo:lo + QK_CHUNK], DIFF_D).astype(BF16)
        qk_ref[:, lo:lo + QK_CHUNK] = out
        return out

    def val(lhs, c):
        lo = c * PROJ_CHUNK
        out = _dot(lhs, w_ref[:, off_v + lo:off_v + lo + PROJ_CHUNK]).astype(BF16)
        v_ref[:, lo:lo + PROJ_CHUNK] = out
        return out

    def gate(lhs, c):
        lo = c * PROJ_CHUNK
        y = _dot(lhs, w_ref[:, off_gt + lo:off_gt + lo + PROJ_CHUNK])
        out = jax.nn.sigmoid(y).astype(BF16)
        gt_ref[:, lo:lo + PROJ_CHUNK] = out
        return out

    def mem_query(lhs, hd):
        sl = slice(hd * MEM_D, (hd + 1) * MEM_D)
        y = _dot(lhs, w_ref[:, off_mq + hd * MEM_D:off_mq + (hd + 1) * MEM_D])
        mem_q[hd] = (_rms(y) * gmq_ref[:, sl]).astype(BF16)
        return mem_q[hd]

    def mem_scores(lhs, hd):
        sl = slice(hd * MEM_D, (hd + 1) * MEM_D)
        s = _dot_nt(mem_q[hd], mk_ref[:, sl])
        p = jnp.exp2(s - jnp.max(s, axis=-1, keepdims=True))
        mem_p[hd] = (p.astype(BF16), jnp.sum(p, axis=-1, keepdims=True))
        return mem_p[hd][0]

    def mem_values(lhs, hd):
        sl = slice(hd * MEM_D, (hd + 1) * MEM_D)
        p, l = mem_p[hd]
        out = (_dot(p, mv_ref[:, sl]) / l).astype(BF16)
        mo_ref[:, sl] = out
        return out

    order = [(gate, 0), (mem_query, 0), (qk_norm, 0), (mem_scores, 0), (qk_norm, 1),
             (mem_values, 0), (gate, 1), (mem_query, 1), (qk_norm, 2), (mem_scores, 1),
             (qk_norm, 3), (mem_values, 1), (gate, 2), (mem_query, 2), (qk_norm, 4),
             (mem_scores, 2), (qk_norm, 5), (mem_values, 2), (gate, 3), (mem_query, 3),
             (qk_norm, 6), (mem_scores, 3), (qk_norm, 7), (mem_values, 3), (gate, 4),
             (gate, 5), (val, 0), (val, 1)]
    for fn, idx in order:
        fn(h, idx)


def _proj_mix(h, w, gqk, gmq, mem2d, g_mem, gmk, w_kv, seq):
    t = h.shape[0]
    mlen = mem2d.shape[0] // (t // seq)
    tiles_per_batch = seq // TM
    row = lambda width: pl.BlockSpec((TM, width), lambda i: (i, 0))
    vec = lambda width: pl.BlockSpec((1, width), lambda i: (0, 0))
    mem = pl.BlockSpec((mlen, D_MODEL), lambda i: (i // tiles_per_batch, 0))
    return pl.pallas_call(
        functools.partial(_proj_mix_kernel, tiles_per_row=tiles_per_batch),
        grid=(t // TM,),
        in_specs=[row(D_MODEL), _ANY, vec(2 * DIFF_W), vec(MEM_W), mem, vec(D_MODEL), vec(MEM_W),
                  _ANY],
        out_specs=[row(2 * DIFF_W), row(DIFF_W), row(MEM_W), row(3 * D_MODEL)],
        out_shape=[jax.ShapeDtypeStruct((t, 2 * DIFF_W), BF16),
                   jax.ShapeDtypeStruct((t, DIFF_W), BF16),
                   jax.ShapeDtypeStruct((t, MEM_W), BF16),
                   jax.ShapeDtypeStruct((t, 3 * D_MODEL), BF16)],
        scratch_shapes=_weight_scratch(D_MODEL, PROJ_MIX_W, slots=STAGE_SLOTS + 2)
        + _weight_scratch(*w_kv.shape, slots=STAGE_SLOTS - 1)
        + [pltpu.VMEM((mlen, MEM_W), BF16), pltpu.VMEM((mlen, MEM_W), BF16)],
        compiler_params=_params(("arbitrary",)),
        name="proj_mix",
    )(h, w, gqk, gmq, mem2d, g_mem, gmk, w_kv)


def _diff_kernel(q_ref, k_ref, v_ref, lq1_ref, lk1_ref, lq2_ref, lk2_ref, go_ref, o_ref,
                 *, lam_init, seq):
    tq = DIFF_TQ
    lane = lax.broadcasted_iota(jnp.int32, (1, 2 * DIFF_D), 1)
    first_map = lane < DIFF_D
    r = lax.broadcasted_iota(jnp.int32, (tq, tq), 0)
    cidx = lax.broadcasted_iota(jnp.int32, (tq, tq), 1)
    keep = cidx <= r
    keep = jnp.concatenate([keep, keep], axis=0)
    lam = (jnp.exp(jnp.sum(lq1_ref[...] * lk1_ref[...], axis=-1, keepdims=True))
           - jnp.exp(jnp.sum(lq2_ref[...] * lk2_ref[...], axis=-1, keepdims=True))
           + lam_init)
    go = go_ref[...]
    nq = seq // tq
    hw = 2 * DIFF_D
    heads = DIFF_HEADS_PER_STEP
    ones = jnp.ones((seq, hw), BF16)
    v_ext = [jnp.concatenate([v_ref[:, hd * hw:(hd + 1) * hw], ones], axis=1)
             for hd in range(heads)]

    def finish(hd, i, o_ext):
        r_ = o_ext[:, :hw] / o_ext[:, hw:]
        d = r_[:tq] - lam * r_[tq:]
        o_ref[i * tq:(i + 1) * tq, hd * hw:(hd + 1) * hw] = (_rms(d) * go).astype(BF16)

    prev = [None] * heads
    for i in list(range(nq)) + [None]:
        tiles = [[] for _ in range(heads)]
        run_max = [None] * heads
        acc = [None] * heads
        qs = [None] * heads
        if i is not None:
            for hd in range(heads):
                q = q_ref[i * tq:(i + 1) * tq, hd * hw:(hd + 1) * hw]
                zero = jnp.zeros_like(q)
                qs[hd] = jnp.concatenate([jnp.where(first_map, q, zero),
                                          jnp.where(first_map, zero, q)], axis=0)
        n_cur = 0 if i is None else i + 1
        n_prev = 0 if prev[0] is None else len(prev[0][1])
        for j in range(max(n_cur, n_prev)):
            for hd in range(heads):
                if j < n_cur:
                    s = _dot_nt(qs[hd], k_ref[j * tq:(j + 1) * tq, hd * hw:(hd + 1) * hw])
                    if j == i:
                        s = jnp.where(keep, s, NEG)
                    tiles[hd].append(s)
                    run_max[hd] = s if run_max[hd] is None else jnp.maximum(run_max[hd], s)
                if j < n_prev:
                    p = jnp.exp2(prev[hd][1][j] - prev[hd][2]).astype(BF16)
                    pv = _dot(p, v_ext[hd][j * tq:(j + 1) * tq, :])
                    acc[hd] = pv if acc[hd] is None else acc[hd] + pv
        for hd in range(heads):
            if prev[hd] is not None:
                finish(hd, prev[hd][0], acc[hd])
            prev[hd] = (None if i is None else
                        (i, tiles[hd], jnp.max(run_max[hd], axis=-1, keepdims=True)))


def _diff_attention(qk, v, lq1, lk1, lq2, lk2, g_out, batch, seq, lam_init):
    t = batch * seq
    width = DIFF_HEADS_PER_STEP * 2 * DIFF_D
    groups = DIFF_HEADS // DIFF_HEADS_PER_STEP
    vec64 = pl.BlockSpec((1, DIFF_D), lambda b, h: (0, 0))
    head = pl.BlockSpec((seq, width), lambda b, h: (b, h))
    return pl.pallas_call(
        functools.partial(_diff_kernel, lam_init=lam_init, seq=seq),
        grid=(batch, groups),
        in_specs=[head,
                  pl.BlockSpec((seq, width), lambda b, h: (b, groups + h)),
                  head,
                  vec64, vec64, vec64, vec64,
                  pl.BlockSpec((1, 2 * DIFF_D), lambda b, h: (0, 0))],
        out_specs=head,
        out_shape=jax.ShapeDtypeStruct((t, DIFF_HEADS * 2 * DIFF_D), BF16),
        compiler_params=_params(("parallel", "parallel")),
        name="diff_attention",
    )(qk, qk, v, lq1, lk1, lq2, lk2, g_out)


def _ret_merge_kernel(x_ref, qk_ref, v_ref, g_ref, dif_ref, mo_ref, gt_ref, dec_ref, qd_ref, kd_ref,
                      wr_hbm, wd_hbm, wm_hbm, wo_hbm, o_ref,
                      st_ref, ret_ref, wr_ref, wd_ref, wm_ref, wo_ref, stage, sem, *, tiles_per_row):
    i = pl.program_id(0)

    @pl.when(i == 0)
    def _():
        for src, dst in ((wr_hbm, wr_ref), (wd_hbm, wd_ref), (wm_hbm, wm_ref), (wo_hbm, wo_ref)):
            _load_weight(src, 0, dst, stage, sem)

    @pl.when(i % tiles_per_row == 0)
    def _():
        st_ref[...] = jnp.zeros_like(st_ref)

    tm = x_ref.shape[0]
    for c in range(tm // RET_C):
        lo = c * RET_C
        for hd in range(RET_HEADS):
            qd = qd_ref[hd]
            q = qk_ref[lo:lo + RET_C, hd * RET_DK:(hd + 1) * RET_DK]
            k = qk_ref[lo:lo + RET_C, RET_QK_W + hd * RET_DK:RET_QK_W + (hd + 1) * RET_DK]
            v = v_ref[lo:lo + RET_C, hd * RET_DV:(hd + 1) * RET_DV]
            st = st_ref[hd]
            s = _dot_nt(q, k) * dec_ref[hd]
            out = _dot(s.astype(BF16), v) + _dot(q, st.astype(BF16)) * qd
            upd = _dot_tn((k.astype(F32) * kd_ref[hd]).astype(BF16), v)
            st_ref[hd] = qd[RET_C - 1:RET_C, :] * st + upd
            gate = g_ref[lo:lo + RET_C, hd * RET_DV:(hd + 1) * RET_DV].astype(F32)
            ret_ref[lo:lo + RET_C, hd * RET_DV:(hd + 1) * RET_DV] = (_rms(out) * gate).astype(BF16)

    gt = gt_ref[...].astype(F32)
    y = (gt[:, :D_MODEL] * _dot(ret_ref[...], wr_ref[...])
         + gt[:, D_MODEL:2 * D_MODEL] * _dot(dif_ref[...], wd_ref[...])
         + gt[:, 2 * D_MODEL:] * _dot(mo_ref[...], wm_ref[...]))
    o_ref[...] = x_ref[...] + _dot(y.astype(BF16), wo_ref[...])


def _ret_merge(x, qk, v, gate, dif, mo, gates, dec, qd, kd, wr, wd, wm, wo, seq):
    t = x.shape[0]
    row = lambda w: pl.BlockSpec((TM, w), lambda i: (i, 0))
    const = lambda shape: pl.BlockSpec(shape, lambda i: (0,) * len(shape))
    return pl.pallas_call(
        functools.partial(_ret_merge_kernel, tiles_per_row=seq // TM),
        grid=(t // TM,),
        in_specs=[row(D_MODEL), row(2 * RET_QK_W), row(RET_V_W), row(RET_V_W), row(D_MODEL),
                  row(D_MODEL), row(3 * D_MODEL),
                  const(dec.shape), const(qd.shape), const(kd.shape),
                  _ANY, _ANY, _ANY, _ANY],
        out_specs=row(D_MODEL),
        out_shape=jax.ShapeDtypeStruct((t, D_MODEL), F32),
        scratch_shapes=[pltpu.VMEM((RET_HEADS, RET_DK, RET_DV), F32),
                        pltpu.VMEM((TM, RET_V_W), BF16),
                        pltpu.VMEM(wr.shape, BF16), pltpu.VMEM(wd.shape, BF16),
                        pltpu.VMEM(wm.shape, BF16), pltpu.VMEM(wo.shape, BF16)]
        + _weight_scratch(D_MODEL, D_MODEL, slots=STAGE_SLOTS + 1)[1:],
        compiler_params=_params(("arbitrary",)),
        name="ret_merge",
    )(x, qk, v, gate, dif, mo, gates, dec, qd, kd, wr, wd, wm, wo)


def _retention_constants():
    h = np.arange(RET_HEADS, dtype=np.float64)
    log_g = np.log1p(-(2.0 ** (-5.0 - h)))
    idx = np.arange(RET_C, dtype=np.float64)
    dist = idx[:, None] - idx[None, :]
    dec = np.where(dist >= 0, np.exp(log_g[:, None, None] * np.maximum(dist, 0.0)), 0.0)
    qd = np.exp(log_g[:, None] * (idx + 1.0))[:, :, None]
    kd = np.exp(log_g[:, None] * (RET_C - 1.0 - idx))[:, :, None]
    return (jnp.asarray(dec, F32), jnp.asarray(qd, F32), jnp.asarray(kd, F32))


def kernel(x, mem, positions, g_ffn1, w_ffn1_in, w_ffn1_out, g_mix, w_in, g_diff_q, g_diff_k,
           lam_q1, lam_k1, lam_q2, lam_k2, g_diff_out, g_mem_q, g_mem_k, g_mem, w_mem_kv,
           w_br_ret, w_br_diff, w_br_mem, w_o, g_ffn2, w_ffn2_in, w_ffn2_out):
    batch, seq, _ = x.shape
    mlen = mem.shape[1]
    t = batch * seq
    depth = g_ffn1.shape[0]
    half = RET_DK // 2
    inv = jnp.asarray(ROPE_BASE ** (-np.arange(half, dtype=np.float64) / half), F32)[None, :]
    dec, qd, kd = _retention_constants()
    pos = positions.reshape(t)
    mem2d = mem.reshape(batch * mlen, D_MODEL)
    xf = x.reshape(t, D_MODEL)
    f32 = lambda w: w.astype(F32)
    vec = lambda g: g.astype(F32)[None, :]

    for l in range(depth):
        lam_init = 0.8 - 0.6 * math.exp(-0.3 * l)
        x1, h = _ffn(xf, vec(g_ffn1[l]), f32(w_ffn1_in[l]), f32(w_ffn1_out[l]), vec(g_mix[l]))

        w = f32(w_in[l])
        rqk, rv, rg = _proj_ret(h, w, pos, inv)
        qk_gain = jnp.concatenate([jnp.tile(g_diff_q[l].astype(F32), 2 * DIFF_HEADS) * (DIFF_D ** -0.5 * LOG2E),
                                   jnp.tile(g_diff_k[l].astype(F32), 2 * DIFF_HEADS)])[None, :]
        mq_gain = (jnp.tile(g_mem_q[l].astype(F32), MEM_HEADS) * (MEM_D ** -0.5 * LOG2E))[None, :]
        mk_gain = jnp.tile(g_mem_k[l].astype(F32), MEM_HEADS)[None, :]
        dqk, dv, mo, gates = _proj_mix(h, w, qk_gain, mq_gain, mem2d, vec(g_mem[l]), mk_gain,
                                       f32(w_mem_kv[l]), seq)

        go = (g_diff_out[l].astype(F32) * (1.0 - lam_init))[None, :]
        dif = _diff_attention(dqk, dv, vec(lam_q1[l]), vec(lam_k1[l]), vec(lam_q2[l]),
                              vec(lam_k2[l]), go, batch, seq, lam_init)

        x2 = _ret_merge(x1, rqk, rv, rg, dif, mo, gates, dec, qd, kd, f32(w_br_ret[l]),
                        f32(w_br_diff[l]), f32(w_br_mem[l]), f32(w_o[l]), seq)
        (xf,) = _ffn(x2, vec(g_ffn2[l]), f32(w_ffn2_in[l]), f32(w_ffn2_out[l]))
    return xf.reshape(batch, seq, D_MODEL)
```

```python
import functools
import math

import jax
import jax.numpy as jnp
import numpy as np
from jax import lax
from jax.experimental import pallas as pl
from jax.experimental.pallas import tpu as pltpu

F32 = jnp.float32
BF16 = jnp.bfloat16

D_MODEL = 1024
D_FF = 2816
FFN_RES = 0.5
EPS = 1e-6
NEG = -1e30
LOG2E = math.log2(math.e)
ROPE_BASE = 10000.0

RET_HEADS = 4
RET_DK = 256
RET_DV = 512
DIFF_HEADS = 8
DIFF_D = 64
MEM_HEADS = 4
MEM_D = 256

V7X_VMEM_LIMIT_BYTES = 56 * 1024 * 1024
LANES = 128

TM = 512
TM_FFN = 512
TM_RET = 1024
FFN_CHUNK = 256
RET_C = 256
DIFF_TQ = 256
DIFF_HEADS_PER_STEP = 2


def _params(sem):
    return pltpu.CompilerParams(dimension_semantics=sem,
                                vmem_limit_bytes=V7X_VMEM_LIMIT_BYTES)


def _rms(x):
    return x * lax.rsqrt(jnp.mean(x * x, axis=-1, keepdims=True) + EPS)


def _dot(a, b):
    return jnp.dot(a, b, preferred_element_type=F32)


def _dot_nt(a, b):
    return lax.dot_general(a, b, (((1,), (1,)), ((), ())), preferred_element_type=F32)


def _dot_tn(a, b):
    return lax.dot_general(a, b, (((0,), (0,)), ((), ())), preferred_element_type=F32)


STAGE_BYTES = 1024 * 1024
STAGE_SLOTS = 4


def _stage_rows(width):
    return max(16, (STAGE_BYTES // (4 * width)) // 16 * 16)


def _weight_scratch(rows, width, slots=STAGE_SLOTS):
    return [pltpu.VMEM((rows, width), BF16),
            pltpu.VMEM((slots, _stage_rows(width), width), F32),
            pltpu.SemaphoreType.DMA((slots,))]


def _load_weight(w_hbm, col0, w_vmem, stage, sem):
    rows, width = w_vmem.shape
    slots, step = stage.shape[0], stage.shape[1]
    starts = list(range(0, rows, step))

    def copy(c):
        n = min(step, rows - starts[c])
        return pltpu.make_async_copy(
            w_hbm.at[pl.ds(starts[c], n), pl.ds(col0, width)],
            stage.at[c % slots, pl.ds(0, n), :], sem.at[c % slots])

    for c in range(min(slots - 1, len(starts))):
        copy(c).start()
    for c in range(len(starts)):
        if c + slots - 1 < len(starts):
            copy(c + slots - 1).start()
        copy(c).wait()
        n = min(step, rows - starts[c])
        w_vmem[starts[c]:starts[c] + n, :] = stage[c % slots, 0:n, :].astype(BF16)


_ANY = pl.BlockSpec(memory_space=pl.ANY)


FFN_RING = 3


def _ffn_kernel(*refs, emit_next):
    if emit_next:
        x_ref, g_ref, win_hbm, wout_hbm, gn_ref, o_ref, h_ref = refs[:7]
    else:
        x_ref, g_ref, win_hbm, wout_hbm, o_ref = refs[:5]
    win_ref, wout_ref, sg_ref, su_ref, so_ref, sem = refs[-6:]
    n_chunks = D_FF // FFN_CHUNK

    def copies(j):
        lo, slot = j * FFN_CHUNK, j % FFN_RING
        return (pltpu.make_async_copy(win_hbm.at[:, pl.ds(lo, FFN_CHUNK)],
                                      sg_ref.at[slot], sem.at[slot, 0]),
                pltpu.make_async_copy(win_hbm.at[:, pl.ds(D_FF + lo, FFN_CHUNK)],
                                      su_ref.at[slot], sem.at[slot, 1]),
                pltpu.make_async_copy(wout_hbm.at[pl.ds(lo, FFN_CHUNK), :],
                                      so_ref.at[slot], sem.at[slot, 2]))

    def fetch_chunk(j):
        if j == 0:
            for c in range(min(FFN_RING - 1, n_chunks)):
                for cp in copies(c):
                    cp.start()
        if j + FFN_RING - 1 < n_chunks:
            for cp in copies(j + FFN_RING - 1):
                cp.start()
        for cp in copies(j):
            cp.wait()
        lo, slot = j * FFN_CHUNK, j % FFN_RING
        win_ref[:, lo:lo + FFN_CHUNK] = sg_ref[slot].astype(BF16)
        win_ref[:, D_FF + lo:D_FF + lo + FFN_CHUNK] = su_ref[slot].astype(BF16)
        wout_ref[lo:lo + FFN_CHUNK, :] = so_ref[slot].astype(BF16)

    def body(before_chunk):
        x = x_ref[...]
        hb = (_rms(x) * g_ref[...]).astype(BF16)
        acc = jnp.zeros(x.shape, F32)
        for j in range(n_chunks):
            before_chunk(j)
            lo = j * FFN_CHUNK
            a = _dot(hb, win_ref[:, lo:lo + FFN_CHUNK])
            b = _dot(hb, win_ref[:, D_FF + lo:D_FF + lo + FFN_CHUNK])
            act = (a * jax.nn.sigmoid(a) * b).astype(BF16)
            acc = acc + _dot(act, wout_ref[lo:lo + FFN_CHUNK, :])
        y = x + FFN_RES * acc
        o_ref[...] = y
        if emit_next:
            h_ref[...] = (_rms(y) * gn_ref[...]).astype(BF16)

    @pl.when(pl.program_id(0) == 0)
    def _():
        body(fetch_chunk)

    @pl.when(pl.program_id(0) != 0)
    def _():
        body(lambda j: None)


def _ffn(x, g, w_in, w_out, g_next=None):
    t = x.shape[0]
    emit = g_next is not None
    row = pl.BlockSpec((TM_FFN, D_MODEL), lambda i: (i, 0))
    vec = pl.BlockSpec((1, D_MODEL), lambda i: (0, 0))
    in_specs = [row, vec, _ANY, _ANY]
    args = [x, g, w_in, w_out]
    out_shape = [jax.ShapeDtypeStruct((t, D_MODEL), F32)]
    out_specs = [row]
    if emit:
        in_specs.append(vec)
        args.append(g_next)
        out_shape.append(jax.ShapeDtypeStruct((t, D_MODEL), BF16))
        out_specs.append(row)
    return pl.pallas_call(
        functools.partial(_ffn_kernel, emit_next=emit),
        grid=(t // TM_FFN,),
        in_specs=in_specs,
        out_specs=out_specs,
        out_shape=out_shape,
        scratch_shapes=[pltpu.VMEM(w_in.shape, BF16), pltpu.VMEM(w_out.shape, BF16),
                        pltpu.VMEM((FFN_RING, D_MODEL, FFN_CHUNK), F32),
                        pltpu.VMEM((FFN_RING, D_MODEL, FFN_CHUNK), F32),
                        pltpu.VMEM((FFN_RING, FFN_CHUNK, D_MODEL), F32),
                        pltpu.SemaphoreType.DMA((FFN_RING, 3))],
        compiler_params=_params(("arbitrary",)),
        name="ffn_next" if emit else "ffn",
    )(*args)


def _group_norm_128(y, gain, group):
    cols = y.shape[1]
    outs = []
    if 2 * group == LANES:
        lane = lax.broadcasted_iota(jnp.int32, (1, LANES), 1)
        first = lane < group
        for c in range(cols // LANES):
            blk = y[:, c * LANES:(c + 1) * LANES]
            sq = blk * blk
            s_lo = jnp.sum(jnp.where(first, sq, 0.0), axis=-1, keepdims=True)
            s_hi = jnp.sum(jnp.where(first, 0.0, sq), axis=-1, keepdims=True)
            r = jnp.where(first, lax.rsqrt(s_lo * (1.0 / group) + EPS),
                          lax.rsqrt(s_hi * (1.0 / group) + EPS))
            outs.append(blk * r)
    else:
        for c in range(cols // group):
            blk = y[:, c * group:(c + 1) * group]
            outs.append(_rms(blk))
    return jnp.concatenate(outs, axis=1) * gain


RET_QK_W = RET_HEADS * RET_DK
RET_V_W = RET_HEADS * RET_DV
PROJ_RET_W = 2 * RET_QK_W + 2 * RET_V_W
DIFF_W = DIFF_HEADS * 2 * DIFF_D
MEM_W = MEM_HEADS * MEM_D
PROJ_MIX_W = 2 * DIFF_W + DIFF_W + MEM_W + 3 * D_MODEL
PROJ_CHUNK = 512
QK_CHUNK = 256

def _proj_ret_kernel(irregular_ref, first_ref, h_ref, w_hbm, pos_ref, inv_ref, qk_ref, v_ref, g_ref,
                     w_ref, w_stage, w_sem, base_cos, base_sin, cos_ref, sin_ref):
    tm = h_ref.shape[0]
    inv = inv_ref[...]
    offs = lax.broadcasted_iota(jnp.int32, (tm, 1), 0)

    @pl.when(pl.program_id(0) == 0)
    def _():
        _load_weight(w_hbm, 0, w_ref, w_stage, w_sem)
        ang = offs.astype(F32) * inv
        base_cos[...] = jnp.cos(ang)
        base_sin[...] = jnp.sin(ang)
        cos_ref[...] = jnp.zeros_like(cos_ref)
        sin_ref[...] = jnp.zeros_like(sin_ref)

    irregular = irregular_ref[pl.program_id(0)] != 0

    @pl.when(irregular)
    def _():
        diag = (lax.broadcasted_iota(jnp.int32, (LANES, LANES), 0)
                == lax.broadcasted_iota(jnp.int32, (LANES, LANES), 1))
        for r in range(tm // LANES):
            p = pos_ref[r:r + 1, :].astype(F32)
            col = jnp.sum(jnp.where(diag, p, 0.0), axis=1, keepdims=True)
            ang = col * inv
            cos_ref[r * LANES:(r + 1) * LANES, :] = jnp.cos(ang)
            sin_ref[r * LANES:(r + 1) * LANES, :] = jnp.sin(ang)

    h = h_ref[...]
    p0 = jnp.full(inv.shape, first_ref[pl.program_id(0)], jnp.int32)
    a0 = p0.astype(F32) * inv
    c0, s0 = jnp.cos(a0), jnp.sin(a0)
    bc, bs = base_cos[...], base_sin[...]
    consecutive = jnp.logical_not(irregular)
    cos = jnp.where(consecutive, c0 * bc - s0 * bs, cos_ref[...])
    sin = jnp.where(consecutive, s0 * bc + c0 * bs, sin_ref[...])
    half = RET_DK // 2

    def rot(lhs, hd):
        lo = hd * RET_DK
        y = _dot(lhs, w_ref[:, lo:lo + RET_DK])
        x1 = y[:, :half]
        x2 = y[:, half:]
        r1 = x1 * cos - x2 * sin
        r2 = x1 * sin + x2 * cos
        if hd >= RET_HEADS:
            r1 = r1 * (RET_DK ** -0.5)
            r2 = r2 * (RET_DK ** -0.5)
        r1 = r1.astype(BF16)
        r2 = r2.astype(BF16)
        qk_ref[:, lo:lo + half] = r1
        qk_ref[:, lo + half:lo + RET_DK] = r2
        return r1, r2

    def val(lhs, c):
        lo = c * PROJ_CHUNK
        v_ref[:, lo:lo + PROJ_CHUNK] = _dot(
            lhs, w_ref[:, 2 * RET_QK_W + lo:2 * RET_QK_W + lo + PROJ_CHUNK]).astype(BF16)

    def gate(lhs, c):
        lo = c * PROJ_CHUNK
        wlo = 2 * RET_QK_W + RET_V_W + lo
        y = _dot(lhs, w_ref[:, wlo:wlo + PROJ_CHUNK])
        g_ref[:, lo:lo + PROJ_CHUNK] = (y * jax.nn.sigmoid(y)).astype(BF16)

    for hd in range(2 * RET_HEADS):
        rot(h, hd)
        gate(h, hd // 2) if hd % 2 == 0 else val(h, hd // 2)


def _proj_ret(h, w, pos, inv):
    t = h.shape[0]
    tiles = pos.reshape(t // TM_RET, TM_RET)
    first = tiles[:, 0].astype(jnp.int32)
    irregular = jnp.any(tiles != tiles[:, :1] + jnp.arange(TM_RET, dtype=pos.dtype)[None, :],
                        axis=1).astype(jnp.int32)
    row = lambda width: pl.BlockSpec((TM_RET, width), lambda i, flags, first: (i, 0))
    return pl.pallas_call(
        _proj_ret_kernel,
        grid_spec=pltpu.PrefetchScalarGridSpec(
            num_scalar_prefetch=2,
            grid=(t // TM_RET,),
            in_specs=[row(D_MODEL), _ANY,
                      pl.BlockSpec((TM_RET // LANES, LANES), lambda i, flags, first: (i, 0)),
                      pl.BlockSpec((1, RET_DK // 2), lambda i, flags, first: (0, 0))],
            out_specs=[row(2 * RET_QK_W), row(RET_V_W), row(RET_V_W)],
            scratch_shapes=_weight_scratch(D_MODEL, PROJ_RET_W, slots=2 * STAGE_SLOTS)
            + [pltpu.VMEM((TM_RET, RET_DK // 2), F32)] * 4),
        out_shape=[jax.ShapeDtypeStruct((t, 2 * RET_QK_W), BF16),
                   jax.ShapeDtypeStruct((t, RET_V_W), BF16),
                   jax.ShapeDtypeStruct((t, RET_V_W), BF16)],
        compiler_params=_params(("arbitrary",)),
        name="proj_ret",
    )(irregular, first, h, w, pos.reshape(t // LANES, LANES), inv)


def _proj_mix_kernel(h_ref, w_hbm, gqk_ref, gmq_ref, mem_ref, gmem_ref, gmk_ref, wkv_hbm,
                     qk_ref, v_ref, mo_ref, gt_ref,
                     w_ref, w_stage, w_sem, wkv_ref, wkv_stage, wkv_sem, mk_ref, mv_ref,
                     *, tiles_per_row):
    @pl.when(pl.program_id(0) == 0)
    def _():
        _load_weight(w_hbm, PROJ_RET_W, w_ref, w_stage, w_sem)
        _load_weight(wkv_hbm, 0, wkv_ref, wkv_stage, wkv_sem)

    @pl.when(pl.program_id(0) % tiles_per_row == 0)
    def _():
        mn = (_rms(mem_ref[...]) * gmem_ref[...]).astype(BF16)
        kv = _dot(mn, wkv_ref[...])
        mk_ref[...] = _group_norm_128(kv[:, :MEM_W], gmk_ref[...], MEM_D).astype(BF16)
        mv_ref[...] = kv[:, MEM_W:].astype(BF16)

    h = h_ref[...]
    off_qk = 0
    off_v = off_qk + 2 * DIFF_W
    off_mq = off_v + DIFF_W
    off_gt = off_mq + MEM_W

    mem_q, mem_p = {}, {}

    def qk_norm(lhs, c):
        lo = c * QK_CHUNK
        y = _dot(lhs, w_ref[:, off_qk + lo:off_qk + lo + QK_CHUNK])
        out = _group_norm_128(y, gqk_ref[:, lo:lo + QK_CHUNK], DIFF_D).astype(BF16)
        qk_ref[:, lo:lo + QK_CHUNK] = out
        return out

    def val(lhs, c):
        lo = c * PROJ_CHUNK
        out = _dot(lhs, w_ref[:, off_v + lo:off_v + lo + PROJ_CHUNK]).astype(BF16)
        v_ref[:, lo:lo + PROJ_CHUNK] = out
        return out

    def gate(lhs, c):
        lo = c * PROJ_CHUNK
        y = _dot(lhs, w_ref[:, off_gt + lo:off_gt + lo + PROJ_CHUNK])
        out = jax.nn.sigmoid(y).astype(BF16)
        gt_ref[:, lo:lo + PROJ_CHUNK] = out
        return out

    def mem_query(lhs, hd):
        sl = slice(hd * MEM_D, (hd + 1) * MEM_D)
        y = _dot(lhs, w_ref[:, off_mq + hd * MEM_D:off_mq + (hd + 1) * MEM_D])
        mem_q[hd] = (_rms(y) * gmq_ref[:, sl]).astype(BF16)
        return mem_q[hd]

    def mem_scores(lhs, hd):
        sl = slice(hd * MEM_D, (hd + 1) * MEM_D)
        s = _dot_nt(mem_q[hd], mk_ref[:, sl])
        p = jnp.exp2(s - jnp.max(s, axis=-1, keepdims=True))
        mem_p[hd] = (p.astype(BF16), jnp.sum(p, axis=-1, keepdims=True))
        return mem_p[hd][0]

    def mem_values(lhs, hd):
        sl = slice(hd * MEM_D, (hd + 1) * MEM_D)
        p, l = mem_p[hd]
        out = (_dot(p, mv_ref[:, sl]) / l).astype(BF16)
        mo_ref[:, sl] = out
        return out

    order = [(gate, 0), (mem_query, 0), (qk_norm, 0), (mem_scores, 0), (qk_norm, 1),
             (mem_values, 0), (gate, 1), (mem_query, 1), (qk_norm, 2), (mem_scores, 1),
             (qk_norm, 3), (mem_values, 1), (gate, 2), (mem_query, 2), (qk_norm, 4),
             (mem_scores, 2), (qk_norm, 5), (mem_values, 2), (gate, 3), (mem_query, 3),
             (qk_norm, 6), (mem_scores, 3), (qk_norm, 7), (mem_values, 3), (gate, 4),
             (gate, 5), (val, 0), (val, 1)]
    for fn, idx in order:
        fn(h, idx)


def _proj_mix(h, w, gqk, gmq, mem2d, g_mem, gmk, w_kv, seq):
    t = h.shape[0]
    mlen = mem2d.shape[0] // (t // seq)
    tiles_per_batch = seq // TM
    row = lambda width: pl.BlockSpec((TM, width), lambda i: (i, 0))
    vec = lambda width: pl.BlockSpec((1, width), lambda i: (0, 0))
    mem = pl.BlockSpec((mlen, D_MODEL), lambda i: (i // tiles_per_batch, 0))
    return pl.pallas_call(
        functools.partial(_proj_mix_kernel, tiles_per_row=tiles_per_batch),
        grid=(t // TM,),
        in_specs=[row(D_MODEL), _ANY, vec(2 * DIFF_W), vec(MEM_W), mem, vec(D_MODEL), vec(MEM_W),
                  _ANY],
        out_specs=[row(2 * DIFF_W), row(DIFF_W), row(MEM_W), row(3 * D_MODEL)],
        out_shape=[jax.ShapeDtypeStruct((t, 2 * DIFF_W), BF16),
                   jax.ShapeDtypeStruct((t, DIFF_W), BF16),
                   jax.ShapeDtypeStruct((t, MEM_W), BF16),
                   jax.ShapeDtypeStruct((t, 3 * D_MODEL), BF16)],
        scratch_shapes=_weight_scratch(D_MODEL, PROJ_MIX_W, slots=STAGE_SLOTS + 2)
        + _weight_scratch(*w_kv.shape, slots=STAGE_SLOTS - 1)
        + [pltpu.VMEM((mlen, MEM_W), BF16), pltpu.VMEM((mlen, MEM_W), BF16)],
        compiler_params=_params(("arbitrary",)),
        name="proj_mix",
    )(h, w, gqk, gmq, mem2d, g_mem, gmk, w_kv)


def _diff_kernel(q_ref, k_ref, v_ref, lq1_ref, lk1_ref, lq2_ref, lk2_ref, go_ref, o_ref,
                 *, lam_init, seq):
    tq = DIFF_TQ
    lane = lax.broadcasted_iota(jnp.int32, (1, 2 * DIFF_D), 1)
    first_map = lane < DIFF_D
    r = lax.broadcasted_iota(jnp.int32, (tq, tq), 0)
    cidx = lax.broadcasted_iota(jnp.int32, (tq, tq), 1)
    keep = cidx <= r
    keep = jnp.concatenate([keep, keep], axis=0)
    lam = (jnp.exp(jnp.sum(lq1_ref[...] * lk1_ref[...], axis=-1, keepdims=True))
           - jnp.exp(jnp.sum(lq2_ref[...] * lk2_ref[...], axis=-1, keepdims=True))
           + lam_init)
    go = go_ref[...]
    nq = seq // tq
    hw = 2 * DIFF_D
    heads = DIFF_HEADS_PER_STEP
    ones = jnp.ones((seq, hw), BF16)
    v_ext = [jnp.concatenate([v_ref[:, hd * hw:(hd + 1) * hw], ones], axis=1)
             for hd in range(heads)]

    def finish(hd, i, o_ext):
        r_ = o_ext[:, :hw] / o_ext[:, hw:]
        d = r_[:tq] - lam * r_[tq:]
        o_ref[i * tq:(i + 1) * tq, hd * hw:(hd + 1) * hw] = (_rms(d) * go).astype(BF16)

    prev = [None] * heads
    for i in list(range(nq)) + [None]:
        tiles = [[] for _ in range(heads)]
        run_max = [None] * heads
        acc = [None] * heads
        qs = [None] * heads
        if i is not None:
            for hd in range(heads):
                q = q_ref[i * tq:(i + 1) * tq, hd * hw:(hd + 1) * hw]
                zero = jnp.zeros_like(q)
                qs[hd] = jnp.concatenate([jnp.where(first_map, q, zero),
                                          jnp.where(first_map, zero, q)], axis=0)
        n_cur = 0 if i is None else i + 1
        n_prev = 0 if prev[0] is None else len(prev[0][1])
        for j in range(max(n_cur, n_prev)):
            for hd in range(heads):
                if j < n_cur:
                    s = _dot_nt(qs[hd], k_ref[j * tq:(j + 1) * tq, hd * hw:(hd + 1) * hw])
                    if j == i:
                        s = jnp.where(keep, s, NEG)
                    tiles[hd].append(s)
                    run_max[hd] = s if run_max[hd] is None else jnp.maximum(run_max[hd], s)
                if j < n_prev:
                    p = jnp.exp2(prev[hd][1][j] - prev[hd][2]).astype(BF16)
                    pv = _dot(p, v_ext[hd][j * tq:(j + 1) * tq, :])
                    acc[hd] = pv if acc[hd] is None else acc[hd] + pv
        for hd in range(heads):
            if prev[hd] is not None:
                finish(hd, prev[hd][0], acc[hd])
            prev[hd] = (None if i is None else
                        (i, tiles[hd], jnp.max(run_max[hd], axis=-1, keepdims=True)))


def _diff_attention(qk, v, lq1, lk1, lq2, lk2, g_out, batch, seq, lam_init):
    t = batch * seq
    width = DIFF_HEADS_PER_STEP * 2 * DIFF_D
    groups = DIFF_HEADS // DIFF_HEADS_PER_STEP
    vec64 = pl.BlockSpec((1, DIFF_D), lambda b, h: (0, 0))
    head = pl.BlockSpec((seq, width), lambda b, h: (b, h))
    return pl.pallas_call(
        functools.partial(_diff_kernel, lam_init=lam_init, seq=seq),
        grid=(batch, groups),
        in_specs=[head,
                  pl.BlockSpec((seq, width), lambda b, h: (b, groups + h)),
                  head,
                  vec64, vec64, vec64, vec64,
                  pl.BlockSpec((1, 2 * DIFF_D), lambda b, h: (0, 0))],
        out_specs=head,
        out_shape=jax.ShapeDtypeStruct((t, DIFF_HEADS * 2 * DIFF_D), BF16),
        compiler_params=_params(("parallel", "parallel")),
        name="diff_attention",
    )(qk, qk, v, lq1, lk1, lq2, lk2, g_out)


def _ret_merge_kernel(x_ref, qk_ref, v_ref, g_ref, dif_ref, mo_ref, gt_ref, dec_ref, qd_ref, kd_ref,
                      wr_hbm, wd_hbm, wm_hbm, wo_hbm, o_ref,
                      st_ref, ret_ref, wr_ref, wd_ref, wm_ref, wo_ref, stage, sem, *, tiles_per_row):
    i = pl.program_id(0)

    @pl.when(i == 0)
    def _():
        for src, dst in ((wr_hbm, wr_ref), (wd_hbm, wd_ref), (wm_hbm, wm_ref), (wo_hbm, wo_ref)):
            _load_weight(src, 0, dst, stage, sem)

    @pl.when(i % tiles_per_row == 0)
    def _():
        st_ref[...] = jnp.zeros_like(st_ref)

    tm = x_ref.shape[0]
    for c in range(tm // RET_C):
        lo = c * RET_C
        for hd in range(RET_HEADS):
            qd = qd_ref[hd]
            q = qk_ref[lo:lo + RET_C, hd * RET_DK:(hd + 1) * RET_DK]
            k = qk_ref[lo:lo + RET_C, RET_QK_W + hd * RET_DK:RET_QK_W + (hd + 1) * RET_DK]
            v = v_ref[lo:lo + RET_C, hd * RET_DV:(hd + 1) * RET_DV]
            st = st_ref[hd]
            s = _dot_nt(q, k) * dec_ref[hd]
            out = _dot(s.astype(BF16), v) + _dot(q, st.astype(BF16)) * qd
            upd = _dot_tn((k.astype(F32) * kd_ref[hd]).astype(BF16), v)
            st_ref[hd] = qd[RET_C - 1:RET_C, :] * st + upd
            gate = g_ref[lo:lo + RET_C, hd * RET_DV:(hd + 1) * RET_DV].astype(F32)
            ret_ref[lo:lo + RET_C, hd * RET_DV:(hd + 1) * RET_DV] = (_rms(out) * gate).astype(BF16)

    gt = gt_ref[...].astype(F32)
    y = (gt[:, :D_MODEL] * _dot(ret_ref[...], wr_ref[...])
         + gt[:, D_MODEL:2 * D_MODEL] * _dot(dif_ref[...], wd_ref[...])
         + gt[:, 2 * D_MODEL:] * _dot(mo_ref[...], wm_ref[...]))
    o_ref[...] = x_ref[...] + _dot(y.astype(BF16), wo_ref[...])


def _ret_merge(x, qk, v, gate, dif, mo, gates, dec, qd, kd, wr, wd, wm, wo, seq):
    t = x.shape[0]
    row = lambda w: pl.BlockSpec((TM, w), lambda i: (i, 0))
    const = lambda shape: pl.BlockSpec(shape, lambda i: (0,) * len(shape))
    return pl.pallas_call(
        functools.partial(_ret_merge_kernel, tiles_per_row=seq // TM),
        grid=(t // TM,),
        in_specs=[row(D_MODEL), row(2 * RET_QK_W), row(RET_V_W), row(RET_V_W), row(D_MODEL),
                  row(D_MODEL), row(3 * D_MODEL),
                  const(dec.shape), const(qd.shape), const(kd.shape),
                  _ANY, _ANY, _ANY, _ANY],
        out_specs=row(D_MODEL),
        out_shape=jax.ShapeDtypeStruct((t, D_MODEL), F32),
        scratch_shapes=[pltpu.VMEM((RET_HEADS, RET_DK, RET_DV), F32),
                        pltpu.VMEM((TM, RET_V_W), BF16),
                        pltpu.VMEM(wr.shape, BF16), pltpu.VMEM(wd.shape, BF16),
                        pltpu.VMEM(wm.shape, BF16), pltpu.VMEM(wo.shape, BF16)]
        + _weight_scratch(D_MODEL, D_MODEL, slots=STAGE_SLOTS + 1)[1:],
        compiler_params=_params(("arbitrary",)),
        name="ret_merge",
    )(x, qk, v, gate, dif, mo, gates, dec, qd, kd, wr, wd, wm, wo)


def _retention_constants():
    h = np.arange(RET_HEADS, dtype=np.float64)
    log_g = np.log1p(-(2.0 ** (-5.0 - h)))
    idx = np.arange(RET_C, dtype=np.float64)
    dist = idx[:, None] - idx[None, :]
    dec = np.where(dist >= 0, np.exp(log_g[:, None, None] * np.maximum(dist, 0.0)), 0.0)
    qd = np.exp(log_g[:, None] * (idx + 1.0))[:, :, None]
    kd = np.exp(log_g[:, None] * (RET_C - 1.0 - idx))[:, :, None]
    return (jnp.asarray(dec, F32), jnp.asarray(qd, F32), jnp.asarray(kd, F32))


def kernel(x, mem, positions, g_ffn1, w_ffn1_in, w_ffn1_out, g_mix, w_in, g_diff_q, g_diff_k,
           lam_q1, lam_k1, lam_q2, lam_k2, g_diff_out, g_mem_q, g_mem_k, g_mem, w_mem_kv,
           w_br_ret, w_br_diff, w_br_mem, w_o, g_ffn2, w_ffn2_in, w_ffn2_out):
    batch, seq, _ = x.shape
    mlen = mem.shape[1]
    t = batch * seq
    depth = g_ffn1.shape[0]
    half = RET_DK // 2
    inv = jnp.asarray(ROPE_BASE ** (-np.arange(half, dtype=np.float64) / half), F32)[None, :]
    dec, qd, kd = _retention_constants()
    pos = positions.reshape(t)
    mem2d = mem.reshape(batch * mlen, D_MODEL)
    xf = x.reshape(t, D_MODEL)
    f32 = lambda w: w.astype(F32)
    vec = lambda g: g.astype(F32)[None, :]

    for l in range(depth):
        lam_init = 0.8 - 0.6 * math.exp(-0.3 * l)
        x1, h = _ffn(xf, vec(g_ffn1[l]), f32(w_ffn1_in[l]), f32(w_ffn1_out[l]), vec(g_mix[l]))

        w = f32(w_in[l])
        rqk, rv, rg = _proj_ret(h, w, pos, inv)
        qk_gain = jnp.concatenate([jnp.tile(g_diff_q[l].astype(F32), 2 * DIFF_HEADS) * (DIFF_D ** -0.5 * LOG2E),
                                   jnp.tile(g_diff_k[l].astype(F32), 2 * DIFF_HEADS)])[None, :]
        mq_gain = (jnp.tile(g_mem_q[l].astype(F32), MEM_HEADS) * (MEM_D ** -0.5 * LOG2E))[None, :]
        mk_gain = jnp.tile(g_mem_k[l].astype(F32), MEM_HEADS)[None, :]
        dqk, dv, mo, gates = _proj_mix(h, w, qk_gain, mq_gain, mem2d, vec(g_mem[l]), mk_gain,
                                       f32(w_mem_kv[l]), seq)

        go = (g_diff_out[l].astype(F32) * (1.0 - lam_init))[None, :]
        dif = _diff_attention(dqk, dv, vec(lam_q1[l]), vec(lam_k1[l]), vec(lam_q2[l]),
                              vec(lam_k2[l]), go, batch, seq, lam_init)

        x2 = _ret_merge(x1, rqk, rv, rg, dif, mo, gates, dec, qd, kd, f32(w_br_ret[l]),
                        f32(w_br_diff[l]), f32(w_br_mem[l]), f32(w_o[l]), seq)
        (xf,) = _ffn(x2, vec(g_ffn2[l]), f32(w_ffn2_in[l]), f32(w_ffn2_out[l]))
    return xf.reshape(batch, seq, D_MODEL)
```
